```python
import jax, jax.numpy as jnp
from jax import lax
import numpy as np

D_MODEL = 2048
BATCH = 4
SEQ = 2048
DEPTH = 1
DEC_BATCH = 128
DEC_SEQ = 1
PAST_LEN = 2048
PAGE_SIZE = 128

N_HEADS = 8
HEAD_DIM = 128
ATTN_WIDTH = N_HEADS * HEAD_DIM
Q_BLOCK = 128
D_RNN = 1024
N_RNN_BLOCKS = 8
RNN_BLOCK = D_RNN // N_RNN_BLOCKS
CONV_WIDTH = 4
LRU_C = 8.0
D_FF = ((8 * D_MODEL + 3 * 256 - 1) // (3 * 256)) * 256
IN_COLS = 3 * ATTN_WIDTH + N_HEADS + 2 * D_RNN + 2 * D_MODEL
EPS = 1e-6

kernel_name = 'hybrid_fox_rglru_decode_step'


def rms_norm(x, g):
    xf = x.astype(jnp.float32)
    inv = lax.rsqrt(jnp.mean(xf * xf, axis=-1, keepdims=True) + EPS)
    return (xf * inv).astype(x.dtype) * g


def split_columns(z):
    sizes = (ATTN_WIDTH, ATTN_WIDTH, ATTN_WIDTH, N_HEADS, D_RNN, D_RNN, D_MODEL, D_MODEL)
    idx = np.cumsum(sizes)[:-1].tolist()
    return jnp.split(z, idx, axis=-1)


def fox_scores(q, k, c_q, c_k):
    s = jnp.einsum('bqhd,bkhd->bhqk', q, k).astype(jnp.float32) * (HEAD_DIM ** -0.5)
    bias = jnp.swapaxes(c_q, 1, 2)[:, :, :, None] - jnp.swapaxes(c_k, 1, 2)[:, :, None, :]
    return s + bias


def fox_prompt(q, k, v, logf):
    B, S = q.shape[0], q.shape[1]
    nb = S // Q_BLOCK
    c = jnp.cumsum(logf, axis=1)
    k_pos = jnp.arange(S)
    qb = jnp.swapaxes(q.reshape(B, nb, Q_BLOCK, N_HEADS, HEAD_DIM), 0, 1)
    cb = jnp.swapaxes(c.reshape(B, nb, Q_BLOCK, N_HEADS), 0, 1)
    pb = k_pos.reshape(nb, Q_BLOCK)

    def one_block(args):
        q_blk, c_blk, q_pos = args
        s = fox_scores(q_blk, k, c_blk, c)
        mask = k_pos[None, :] <= q_pos[:, None]
        p = jax.nn.softmax(jnp.where(mask, s, -jnp.inf), axis=-1)
        return jnp.einsum('bhqk,bkhd->bqhd', p.astype(v.dtype), v)

    out = lax.map(one_block, (qb, cb, pb))
    return jnp.swapaxes(out, 0, 1).reshape(B, S, N_HEADS, HEAD_DIM)


def fox_sample(q, k, v, logf, cache_k, cache_v, cache_logf, page_table):
    DB, T = q.shape[0], q.shape[1]
    past_k = cache_k[page_table].reshape(DB, -1, N_HEADS, HEAD_DIM)
    past_v = cache_v[page_table].reshape(DB, -1, N_HEADS, HEAD_DIM)
    past_lf = cache_logf[page_table].reshape(DB, -1, N_HEADS).astype(jnp.float32)
    c_past = jnp.cumsum(past_lf, axis=1)
    c_new = c_past[:, -1:] + jnp.cumsum(logf, axis=1)
    P = past_k.shape[1]
    s_past = fox_scores(q, past_k, c_new, c_past)
    s_new = fox_scores(q, k, c_new, c_new)
    pos = jnp.arange(T)
    s_new = jnp.where(pos[None, :] <= pos[:, None], s_new, -jnp.inf)
    p = jax.nn.softmax(jnp.concatenate([s_past, s_new], axis=-1), axis=-1)
    p_past, p_new = p[..., :P].astype(v.dtype), p[..., P:].astype(v.dtype)
    return (jnp.einsum('bhqk,bkhd->bqhd', p_past, past_v)
            + jnp.einsum('bhqk,bkhd->bqhd', p_new, v))


def rglru_branch(xr, yr, conv_buf, h0, conv_w, conv_b, w_rg, b_rg, w_ig, b_ig, lru_lambda):
    B, T = xr.shape[0], xr.shape[1]
    xc = jnp.concatenate([conv_buf.astype(xr.dtype), xr], axis=1)
    conv = conv_b
    for i in range(CONV_WIDTH):
        conv = conv + xc[:, i:i + T] * conv_w[i]
    new_buf = xc[:, T:]
    xb = conv.reshape(B, T, N_RNN_BLOCKS, RNN_BLOCK)
    r = jax.nn.sigmoid(jnp.einsum('btnd,nde->btne', xb, w_rg).reshape(B, T, D_RNN) + b_rg)
    i_g = jax.nn.sigmoid(jnp.einsum('btnd,nde->btne', xb, w_ig).reshape(B, T, D_RNN) + b_ig)
    log_a = -LRU_C * r.astype(jnp.float32) * jax.nn.softplus(-lru_lambda.astype(jnp.float32))
    a = jnp.exp(log_a)
    b = jnp.sqrt(-jnp.expm1(2.0 * log_a)) * (i_g * conv).astype(jnp.float32)
    b = b.at[:, 0].add(a[:, 0] * h0.astype(jnp.float32))

    def combine(left, right):
        a_l, b_l = left
        a_r, b_r = right
        return a_l * a_r, a_r * b_l + b_r

    _, h = lax.associative_scan(combine, (a, b), axis=1)
    out = h.astype(xr.dtype) * jax.nn.gelu(yr)
    return out, h[:, -1], new_buf


def trunk_layer(x, attn_fn, conv_buf, h0, g_pre_mix, w_in, b_f, conv_w, conv_b, w_rg, b_rg,
                w_ig, b_ig, lru_lambda, w_o_attn, w_o_rnn, w_out, g_post_mix, g_pre_ffn,
                w_gate, w_up, w_down, g_post_ffn):
    B, T, _ = x.shape
    xn = rms_norm(x, g_pre_mix)
    q, k, v, f, xr, yr, ga, gr = split_columns(xn @ w_in)
    q = q.reshape(B, T, N_HEADS, HEAD_DIM)
    k = k.reshape(B, T, N_HEADS, HEAD_DIM)
    v = v.reshape(B, T, N_HEADS, HEAD_DIM)
    logf = jax.nn.log_sigmoid((f + b_f).astype(jnp.float32))
    attn = attn_fn(q, k, v, logf)
    rnn, h_last, new_buf = rglru_branch(xr, yr, conv_buf, h0, conv_w, conv_b, w_rg, b_rg,
                                        w_ig, b_ig, lru_lambda)
    o = (jax.nn.sigmoid(ga) * (attn.reshape(B, T, ATTN_WIDTH) @ w_o_attn)
         + jax.nn.sigmoid(gr) * (rnn @ w_o_rnn))
    h = x + rms_norm(o @ w_out, g_post_mix)
    hn = rms_norm(h, g_pre_ffn)
    ff = (jax.nn.silu(hn @ w_gate) * (hn @ w_up)) @ w_down
    y = h + rms_norm(ff, g_post_ffn)
    return y, k, v, logf, h_last, new_buf


def setup_inputs(seed: int = 0) -> dict:
    key = jax.random.key(seed)
    ks = iter(jax.random.split(key, 40))
    nrm = lambda shape, scale: jax.random.normal(next(ks), shape, jnp.float32) * scale
    n_pages = PAST_LEN // PAGE_SIZE
    n_used = DEC_BATCH * n_pages
    n_phys = (5 * n_used + 3) // 4
    page_table = jax.random.permutation(next(ks), n_phys)[:n_used].reshape(DEC_BATCH, n_pages).astype(jnp.int32)
    a0 = jax.random.uniform(next(ks), (DEPTH, D_RNN), jnp.float32, 0.9, 0.999)
    a_base = a0 ** (1.0 / LRU_C)
    lru_lambda = jnp.log(a_base) - jnp.log1p(-a_base)
    return {
        'x_prompt': nrm((BATCH, SEQ, D_MODEL), 1.0),
        'x_sample': nrm((DEC_BATCH, DEC_SEQ, D_MODEL), 1.0),
        'cache_k': nrm((DEPTH, n_phys, PAGE_SIZE, N_HEADS, HEAD_DIM), 1.0),
        'cache_v': nrm((DEPTH, n_phys, PAGE_SIZE, N_HEADS, HEAD_DIM), 1.0),
        'cache_logf': jax.nn.log_sigmoid(2.0 + nrm((DEPTH, n_phys, PAGE_SIZE, N_HEADS), 0.5)),
        'state_h': nrm((DEPTH, DEC_BATCH, D_RNN), 0.5),
        'state_conv': nrm((DEPTH, DEC_BATCH, CONV_WIDTH - 1, D_RNN), 1.0),
        'page_table': page_table,
        'g_pre_mix': 1.0 + nrm((DEPTH, D_MODEL), 0.05),
        'w_in': nrm((DEPTH, D_MODEL, IN_COLS), D_MODEL ** -0.5),
        'b_f': 2.0 + nrm((DEPTH, N_HEADS), 0.1),
        'conv_w': nrm((DEPTH, CONV_WIDTH, D_RNN), CONV_WIDTH ** -0.5),
        'conv_b': nrm((DEPTH, D_RNN), 0.01),
        'w_rg': nrm((DEPTH, N_RNN_BLOCKS, RNN_BLOCK, RNN_BLOCK), RNN_BLOCK ** -0.5),
        'b_rg': nrm((DEPTH, D_RNN), 0.01),
        'w_ig': nrm((DEPTH, N_RNN_BLOCKS, RNN_BLOCK, RNN_BLOCK), RNN_BLOCK ** -0.5),
        'b_ig': nrm((DEPTH, D_RNN), 0.01),
        'lru_lambda': lru_lambda,
        'w_o_attn': nrm((DEPTH, ATTN_WIDTH, D_MODEL), ATTN_WIDTH ** -0.5),
        'w_o_rnn': nrm((DEPTH, D_RNN, D_MODEL), D_RNN ** -0.5),
        'w_out': nrm((DEPTH, D_MODEL, D_MODEL), D_MODEL ** -0.5),
        'g_post_mix': 1.0 + nrm((DEPTH, D_MODEL), 0.05),
        'g_pre_ffn': 1.0 + nrm((DEPTH, D_MODEL), 0.05),
        'w_gate': nrm((DEPTH, D_MODEL, D_FF), D_MODEL ** -0.5),
        'w_up': nrm((DEPTH, D_MODEL, D_FF), D_MODEL ** -0.5),
        'w_down': nrm((DEPTH, D_FF, D_MODEL), D_FF ** -0.5),
        'g_post_ffn': 1.0 + nrm((DEPTH, D_MODEL), 0.05),
    }


def reference(x_prompt, x_sample, cache_k, cache_v, cache_logf, state_h, state_conv, page_table,
              g_pre_mix, w_in, b_f, conv_w, conv_b, w_rg, b_rg, w_ig, b_ig, lru_lambda,
              w_o_attn, w_o_rnn, w_out, g_post_mix, g_pre_ffn, w_gate, w_up, w_down, g_post_ffn):
    xp, xs = x_prompt, x_sample
    kp, vp, lp, hp, cp = [], [], [], [], []
    ksm, vsm, lsm, hsm, csm = [], [], [], [], []
    for l in range(DEPTH):
        w = (g_pre_mix[l], w_in[l], b_f[l], conv_w[l], conv_b[l], w_rg[l], b_rg[l], w_ig[l],
             b_ig[l], lru_lambda[l], w_o_attn[l], w_o_rnn[l], w_out[l], g_post_mix[l],
             g_pre_ffn[l], w_gate[l], w_up[l], w_down[l], g_post_ffn[l])
        zero_buf = jnp.zeros((xp.shape[0], CONV_WIDTH - 1, D_RNN), xp.dtype)
        zero_h = jnp.zeros((xp.shape[0], D_RNN), jnp.float32)
        xp, k1, v1, lf1, h1, c1 = trunk_layer(xp, fox_prompt, zero_buf, zero_h, *w)
        ck, cv, cl = cache_k[l], cache_v[l], cache_logf[l]
        attn_fn = lambda q, k, v, lf, ck=ck, cv=cv, cl=cl: fox_sample(q, k, v, lf, ck, cv, cl, page_table)
        xs, k2, v2, lf2, h2, c2 = trunk_layer(xs, attn_fn, state_conv[l], state_h[l], *w)
        kp.append(k1); vp.append(v1); lp.append(lf1); hp.append(h1); cp.append(c1)
        ksm.append(k2); vsm.append(v2); lsm.append(lf2); hsm.append(h2); csm.append(c2)
    k_prompt, v_prompt, logf_prompt = jnp.stack(kp), jnp.stack(vp), jnp.stack(lp)
    h_prompt, conv_prompt = jnp.stack(hp), jnp.stack(cp)
    k_sample, v_sample, logf_sample = jnp.stack(ksm), jnp.stack(vsm), jnp.stack(lsm)
    h_sample, conv_sample = jnp.stack(hsm), jnp.stack(csm)
    return (xp, xs, k_prompt, v_prompt, logf_prompt, h_prompt, conv_prompt,
            k_sample, v_sample, logf_sample, h_sample, conv_sample)
```

```python
import functools
import math

import jax
import jax.numpy as jnp
from jax import lax
from jax.experimental import pallas as pl
from jax.experimental.pallas import tpu as pltpu

F32 = jnp.float32
BF16 = jnp.bfloat16

EPS = 1e-6
LRU_C = 8.0
CONV_WIDTH = 4
NEG_BIG = -1e30

SUBLANES = 8
LANES = 128
VMEM_LIMIT_BYTES = 56 * 1024 * 1024


def _params(*semantics):
    return pltpu.CompilerParams(dimension_semantics=semantics,
                                vmem_limit_bytes=VMEM_LIMIT_BYTES)


def _rms_norm(x, g):
    inv = lax.rsqrt(jnp.mean(x * x, axis=-1, keepdims=True) + EPS)
    return (x * inv) * g


def _log_sigmoid(x):
    return jnp.minimum(x, 0.0) - jnp.log1p(jnp.exp(-jnp.abs(x)))


def _softplus(x):
    return jnp.maximum(x, 0.0) + jnp.log1p(jnp.exp(-jnp.abs(x)))


def _gelu_tanh(x):
    c = math.sqrt(2.0 / math.pi)
    return x * (0.5 * (1.0 + jnp.tanh(c * (x + 0.044715 * (x * x * x)))))


def _dot(a, b):
    return jnp.dot(a, b, preferred_element_type=F32)


def _dot_nt(a, b):
    return lax.dot_general(a, b, (((1,), (1,)), ((), ())), preferred_element_type=F32)


def _in_proj_kernel(x_ref, g_ref, wq, wk, wv, wxr, wyr, wga, wgr, wf, bf_ref,
                    q_o, k_o, v_o, xr_o, yr_o, ga_o, gr_o, lf_o, xn_s):
    @pl.when(pl.program_id(1) == 0)
    def _():
        xn = _rms_norm(x_ref[...], g_ref[...]).astype(BF16)
        xn_s[...] = xn
        f = _dot(xn, wf[...]) + bf_ref[...]
        lf_o[...] = _log_sigmoid(f)[:, :lf_o.shape[1]]

    xn = xn_s[...]
    for w, o in ((wq, q_o), (wk, k_o), (wv, v_o), (wxr, xr_o), (wyr, yr_o),
                 (wga, ga_o), (wgr, gr_o)):
        o[...] = _dot(xn, w[...]).astype(o.dtype)


def _in_proj(x, g, w, tm, n_col_steps):
    rows, d_model = x.shape
    n_heads = w["n_heads"]
    widths = [w[name].shape[1] for name in ("q", "k", "v", "xr", "yr", "ga", "gr")]
    tiles = [wd // n_col_steps for wd in widths]
    grid = (rows // tm, n_col_steps)
    w_specs = [pl.BlockSpec((d_model, t), lambda i, j: (0, j)) for t in tiles]
    o_specs = [pl.BlockSpec((tm, t), lambda i, j: (i, j)) for t in tiles]
    out_dtypes = [BF16, F32, F32, F32, F32, F32, F32]
    out_shape = [jax.ShapeDtypeStruct((rows, wd), dt) for wd, dt in zip(widths, out_dtypes)]
    out_shape.append(jax.ShapeDtypeStruct((rows, n_heads), F32))
    o_specs.append(pl.BlockSpec((tm, n_heads), lambda i, j: (i, 0)))
    return pl.pallas_call(
        _in_proj_kernel,
        grid=grid,
        in_specs=[pl.BlockSpec((tm, d_model), lambda i, j: (i, 0)),
                  pl.BlockSpec((1, d_model), lambda i, j: (0, 0))]
                 + w_specs
                 + [pl.BlockSpec((d_model, LANES), lambda i, j: (0, 0)),
                    pl.BlockSpec((1, LANES), lambda i, j: (0, 0))],
        out_specs=o_specs,
        out_shape=out_shape,
        scratch_shapes=[pltpu.VMEM((tm, d_model), BF16)],
        compiler_params=_params("parallel", "arbitrary"),
        name="in_proj",
    )(x, g, w["q"], w["k"], w["v"], w["xr"], w["yr"], w["ga"], w["gr"], w["f"], w["b_f"])


def _cumsum_kernel(x_ref, o_ref):
    x = x_ref[...]
    n = x.shape[-1]
    lane = lax.broadcasted_iota(jnp.int32, x.shape, 1)
    shift = 1
    while shift < n:
        x = x + jnp.where(lane >= shift, pltpu.roll(x, shift, axis=1), 0.0)
        shift *= 2
    o_ref[...] = x


def _cumsum_lanes(x):
    rows, n = x.shape
    tr = min(rows, 128)
    return pl.pallas_call(
        _cumsum_kernel,
        grid=(rows // tr,),
        in_specs=[pl.BlockSpec((tr, n), lambda i: (i, 0))],
        out_specs=pl.BlockSpec((tr, n), lambda i: (i, 0)),
        out_shape=jax.ShapeDtypeStruct((rows, n), F32),
        compiler_params=_params("parallel"),
        name="cumsum",
    )(x)


def _attn_prompt_kernel(q_ref, k_ref, v_ref, cq_ref, ck_ref, o_ref, kb_s, vb_s, *, tq, scale):
    qi = pl.program_id(2)

    @pl.when(qi == 0)
    def _():
        kb_s[...] = k_ref[...].astype(BF16)
        vb_s[...] = v_ref[...].astype(BF16)

    q = q_ref[...]
    cq = cq_ref[0, 0]
    head_dim = q.shape[1]

    def block(kj, carry, masked):
        m, l, acc = carry
        start = pl.multiple_of(kj * tq, tq)
        kb = kb_s[pl.ds(start, tq), :]
        vb = vb_s[pl.ds(start, tq), :]
        s = _dot_nt(q, kb) * scale + (cq - ck_ref[0, 0, kj])
        if masked:
            row = lax.broadcasted_iota(jnp.int32, (tq, tq), 0)
            col = lax.broadcasted_iota(jnp.int32, (tq, tq), 1)
            s = jnp.where(col <= row, s, NEG_BIG)
        m_new = jnp.maximum(m, jnp.max(s, axis=-1, keepdims=True))
        p = jnp.exp(s - m_new)
        alpha = jnp.exp(m - m_new)
        l = alpha * l + jnp.sum(p, axis=-1, keepdims=True)
        acc = alpha * acc + _dot(p.astype(BF16), vb)
        return m_new, l, acc

    init = (jnp.full((tq, 1), NEG_BIG, F32), jnp.zeros((tq, 1), F32),
            jnp.zeros((tq, head_dim), F32))
    carry = lax.fori_loop(0, qi, lambda kj, c: block(kj, c, False), init)
    _, l, acc = block(qi, carry, True)
    o_ref[...] = (acc / l).astype(o_ref.dtype)


def _attn_prompt(q, k, v, c, batch, seq, n_heads, head_dim, tq):
    nq = seq // tq
    cq = c.reshape(batch, n_heads, seq, 1)
    ck = c.reshape(batch, n_heads, nq, 1, tq)
    return pl.pallas_call(
        functools.partial(_attn_prompt_kernel, tq=tq, scale=head_dim ** -0.5),
        grid=(batch, n_heads, nq),
        in_specs=[pl.BlockSpec((tq, head_dim), lambda b, h, i: (b * nq + i, h)),
                  pl.BlockSpec((seq, head_dim), lambda b, h, i: (b, h)),
                  pl.BlockSpec((seq, head_dim), lambda b, h, i: (b, h)),
                  pl.BlockSpec((1, 1, tq, 1), lambda b, h, i: (b, h, i, 0)),
                  pl.BlockSpec((1, 1, nq, 1, tq), lambda b, h, i: (b, h, 0, 0, 0))],
        out_specs=pl.BlockSpec((tq, head_dim), lambda b, h, i: (b * nq + i, h)),
        out_shape=jax.ShapeDtypeStruct(q.shape, BF16),
        scratch_shapes=[pltpu.VMEM((seq, head_dim), BF16), pltpu.VMEM((seq, head_dim), BF16)],
        compiler_params=_params("parallel", "parallel", "arbitrary"),
        name="attn_prompt",
    )(q, k, v, cq, ck)


def _gather_pages_kernel(pt_ref, *refs):
    del pt_ref
    n_pages = len(refs) - 1
    o_ref = refs[-1]
    for p in range(n_pages):
        o_ref[0, p] = refs[p][0]


def _gather_logf_pages(cache_lf_packed, page_table):
    batch, n_pages = page_table.shape
    blk = cache_lf_packed.shape[1:]

    def page_spec(p):
        return pl.BlockSpec((1,) + blk, lambda b, pt: (pt[b, p], 0, 0))

    return pl.pallas_call(
        _gather_pages_kernel,
        grid_spec=pltpu.PrefetchScalarGridSpec(
            num_scalar_prefetch=1,
            grid=(batch,),
            in_specs=[page_spec(p) for p in range(n_pages)],
            out_specs=pl.BlockSpec((1, n_pages) + blk, lambda b, pt: (b, 0, 0, 0)),
        ),
        out_shape=jax.ShapeDtypeStruct((batch, n_pages) + blk, F32),
        compiler_params=_params("arbitrary"),
        name="gather_logf",
    )(page_table, *([cache_lf_packed] * n_pages))


def _attn_sample_kernel(pt_ref, q_ref, kn_ref, vn_ref, lfn_ref, cp_ref, *refs,
                        n_pages, scale):
    del pt_ref
    k_refs = refs[:n_pages]
    v_refs = refs[n_pages:2 * n_pages]
    o_ref = refs[2 * n_pages]
    s_s = refs[2 * n_pages + 1]
    n_heads, head_dim = o_ref.shape[1], o_ref.shape[2]
    width = n_heads * head_dim
    page = k_refs[0].shape[1]

    head_of_row = lax.broadcasted_iota(jnp.int32, (n_heads, width), 0)
    head_of_col = lax.broadcasted_iota(jnp.int32, (n_heads, width), 1) // head_dim
    on_diag = head_of_row == head_of_col

    q_row = q_ref[0].astype(F32)
    qb = jnp.where(on_diag, jnp.broadcast_to(q_row, (n_heads, width)), 0.0)
    qb16 = qb.astype(BF16)

    for p in range(n_pages):
        s_s[:, p * page:(p + 1) * page] = _dot_nt(qb16, k_refs[p][0].astype(BF16))

    c_past = cp_ref[0]
    n_past = c_past.shape[1]
    c_new = c_past[:, n_past - 1:n_past] + lfn_ref[0]
    s_past = s_s[...] * scale + (c_new - c_past)

    kn = kn_ref[0].astype(BF16).astype(F32)
    vn = vn_ref[0].astype(BF16).astype(F32)
    s_new = jnp.sum(qb * kn, axis=-1, keepdims=True) * scale

    m = jnp.maximum(jnp.max(s_past, axis=-1, keepdims=True), s_new)
    p_past = jnp.exp(s_past - m)
    p_new = jnp.exp(s_new - m)
    l = jnp.sum(p_past, axis=-1, keepdims=True) + p_new
    p16 = p_past.astype(BF16)

    acc = p_new.astype(BF16).astype(F32) * vn
    for p in range(n_pages):
        acc = acc + _dot(p16[:, p * page:(p + 1) * page], v_refs[p][0].astype(BF16))
    acc = jnp.where(on_diag, acc, 0.0) / l

    out = acc[:, 0:head_dim]
    for h in range(1, n_heads):
        out = out + acc[:, h * head_dim:(h + 1) * head_dim]
    o_ref[0] = out.astype(o_ref.dtype)


def _attn_sample(q, k_new, v_new, lf_new, c_past, cache_k, cache_v, page_table,
                 n_heads, head_dim):
    batch, n_pages = page_table.shape
    page, width = cache_k.shape[1], cache_k.shape[2]
    n_past = c_past.shape[2]

    def row_spec(shape):
        return pl.BlockSpec((1,) + shape, lambda b, pt: (b, 0, 0))

    def page_spec(p):
        return pl.BlockSpec((1, page, width), lambda b, pt: (pt[b, p], 0, 0))

    return pl.pallas_call(
        functools.partial(_attn_sample_kernel, n_pages=n_pages, scale=head_dim ** -0.5),
        grid_spec=pltpu.PrefetchScalarGridSpec(
            num_scalar_prefetch=1,
            grid=(batch,),
            in_specs=[row_spec((1, width)), row_spec((1, width)), row_spec((1, width)),
                      row_spec((n_heads, 1)), row_spec((n_heads, n_past))]
                     + [page_spec(p) for p in range(n_pages)] * 2,
            out_specs=row_spec((n_heads, head_dim)),
            scratch_shapes=[pltpu.VMEM((n_heads, n_past), F32)],
        ),
        out_shape=jax.ShapeDtypeStruct((batch, n_heads, head_dim), BF16),
        compiler_params=_params("arbitrary"),
        name="attn_sample",
    )(page_table, q, k_new, v_new, lf_new, c_past,
      *([cache_k] * n_pages), *([cache_v] * n_pages))


def _rglru_gates(conv, r_pre, i_pre, b_rg, b_ig, neg_c_softplus):
    r = jax.nn.sigmoid(r_pre + b_rg)
    i_g = jax.nn.sigmoid(i_pre + b_ig)
    log_a = r * neg_c_softplus
    a = jnp.exp(log_a)
    b = jnp.sqrt(-jnp.tanh(log_a) * (1.0 + a * a)) * (i_g * conv)
    return a, b


def _rglru_prompt_kernel(xr_ref, yr_ref, cw_ref, cb_ref, wrg_ref, wig_ref, brg_ref, big_ref,
                         lam_ref, o_ref, hl_ref, xc_s, conv_s, rp_s, ip_s, a_s, b_s, h_s):
    c = pl.program_id(0)
    nb, tc, n_blocks, bw = xr_ref.shape
    halo = CONV_WIDTH - 1
    rows = nb * tc

    @pl.when(c == 0)
    def _():
        xc_s[:, 0:halo] = jnp.zeros((nb, halo, n_blocks, bw), F32)
        h_s[...] = jnp.zeros_like(h_s)

    @pl.when(c > 0)
    def _():
        xc_s[:, 0:halo] = xc_s[:, tc:tc + halo]

    xc_s[:, halo:halo + tc] = xr_ref[...]

    conv = cb_ref[...] + xc_s[:, 0:tc] * cw_ref[0]
    for i in range(1, CONV_WIDTH):
        conv = conv + xc_s[:, i:i + tc] * cw_ref[i]

    conv_s[...] = conv.reshape(rows * n_blocks, bw)
    for n in range(n_blocks):
        xn = conv_s[pl.ds(n, rows, stride=n_blocks), :].astype(BF16)
        rp_s[pl.ds(n, rows, stride=n_blocks), :] = _dot(xn, wrg_ref[n])
        ip_s[pl.ds(n, rows, stride=n_blocks), :] = _dot(xn, wig_ref[n])

    neg_c_softplus = -LRU_C * _softplus(-lam_ref[...])
    a, b = _rglru_gates(conv,
                        rp_s[...].reshape(nb, tc, n_blocks, bw),
                        ip_s[...].reshape(nb, tc, n_blocks, bw),
                        brg_ref[...], big_ref[...], neg_c_softplus)
    a_s[...] = a
    b_s[...] = b

    def step(t, hs):
        new = tuple(a_s[j, t] * hs[j] + b_s[j, t] for j in range(nb))
        for j in range(nb):
            b_s[j, t] = new[j]
        return new

    hs = lax.fori_loop(0, tc, step, tuple(h_s[j] for j in range(nb)), unroll=8)
    for j in range(nb):
        h_s[j] = hs[j]

    o_ref[...] = b_s[...] * _gelu_tanh(yr_ref[...])
    hl_ref[...] = h_s[...]


def _rglru_prompt(xr, yr, w, batch, seq, tc):
    n_blocks, bw = w["n_rnn_blocks"], w["rnn_block"]
    xr4 = xr.reshape(batch, seq, n_blocks, bw)
    yr4 = yr.reshape(batch, seq, n_blocks, bw)
    rows = batch * tc
    halo = CONV_WIDTH - 1

    def const(shape):
        return pl.BlockSpec(shape, lambda c: (0,) * len(shape))

    seq_spec = pl.BlockSpec((batch, tc, n_blocks, bw), lambda c: (0, c, 0, 0))
    out, h_last = pl.pallas_call(
        _rglru_prompt_kernel,
        grid=(seq // tc,),
        in_specs=[seq_spec, seq_spec,
                  const((CONV_WIDTH, n_blocks, bw)), const((n_blocks, bw)),
                  const((n_blocks, bw, bw)), const((n_blocks, bw, bw)),
                  const((n_blocks, bw)), const((n_blocks, bw)), const((n_blocks, bw))],
        out_specs=[seq_spec, const((batch, n_blocks, bw))],
        out_shape=[jax.ShapeDtypeStruct(xr4.shape, F32),
                   jax.ShapeDtypeStruct((batch, n_blocks, bw), F32)],
        scratch_shapes=[pltpu.VMEM((batch, tc + halo, n_blocks, bw), F32),
                        pltpu.VMEM((rows * n_blocks, bw), F32),
                        pltpu.VMEM((rows * n_blocks, bw), F32),
                        pltpu.VMEM((rows * n_blocks, bw), F32),
                        pltpu.VMEM((batch, tc, n_blocks, bw), F32),
                        pltpu.VMEM((batch, tc, n_blocks, bw), F32),
                        pltpu.VMEM((batch, n_blocks, bw), F32)],
        compiler_params=_params("arbitrary"),
        name="rglru_prompt",
    )(xr4, yr4, w["conv_w3"], w["conv_b2"], w["w_rg"], w["w_ig"], w["b_rg2"], w["b_ig2"],
      w["lam2"])
    return out.reshape(batch * seq, n_blocks * bw), h_last.reshape(batch, n_blocks * bw)


def _rglru_sample_kernel(xr_ref, yr_ref, buf_ref, h0_ref, cw_ref, cb_ref, wrg_ref, wig_ref,
                         brg_ref, big_ref, lam_ref, o_ref, h_ref, nbuf_ref):
    halo = CONV_WIDTH - 1
    n_blocks, bw = wrg_ref.shape[0], wrg_ref.shape[1]
    xr = xr_ref[...]
    conv = cb_ref[...] + xr * cw_ref[halo:halo + 1, :]
    for i in range(halo):
        conv = conv + buf_ref[i] * cw_ref[i:i + 1, :]
    r_parts, i_parts = [], []
    for n in range(n_blocks):
        xn = conv[:, n * bw:(n + 1) * bw].astype(BF16)
        r_parts.append(_dot(xn, wrg_ref[n]))
        i_parts.append(_dot(xn, wig_ref[n]))
    neg_c_softplus = -LRU_C * _softplus(-lam_ref[...])
    a, b = _rglru_gates(conv, jnp.concatenate(r_parts, axis=1), jnp.concatenate(i_parts, axis=1),
                        brg_ref[...], big_ref[...], neg_c_softplus)
    h = a * h0_ref[...] + b
    h_ref[...] = h
    o_ref[...] = h * _gelu_tanh(yr_ref[...])
    for i in range(halo - 1):
        nbuf_ref[i] = buf_ref[i + 1]
    nbuf_ref[halo - 1] = xr


def _rglru_sample(xr, yr, conv_buf_t, h0, w):
    rows, width = xr.shape
    halo = CONV_WIDTH - 1
    return pl.pallas_call(
        _rglru_sample_kernel,
        out_shape=[jax.ShapeDtypeStruct((rows, width), F32),
                   jax.ShapeDtypeStruct((rows, width), F32),
                   jax.ShapeDtypeStruct((halo, rows, width), F32)],
        compiler_params=pltpu.CompilerParams(vmem_limit_bytes=VMEM_LIMIT_BYTES),
        name="rglru_sample",
    )(xr, yr, conv_buf_t, h0, w["conv_w"], w["conv_b"], w["w_rg"], w["w_ig"],
      w["b_rg"], w["b_ig"], w["lam"])


def _out_proj_kernel(attn_ref, rnn_ref, ga_ref, gr_ref, x_ref, woa_ref, wor_ref, wout_ref,
                     gpm_ref, gpf_ref, h_o, hn_o):
    o = (jax.nn.sigmoid(ga_ref[...]) * _dot(attn_ref[...].astype(BF16), woa_ref[...])
         + jax.nn.sigmoid(gr_ref[...]) * _dot(rnn_ref[...].astype(BF16), wor_ref[...]))
    mix = _dot(o.astype(BF16), wout_ref[...])
    h = x_ref[...] + _rms_norm(mix, gpm_ref[...])
    h_o[...] = h
    hn_o[...] = _rms_norm(h, gpf_ref[...]).astype(hn_o.dtype)


def _out_proj(attn, rnn, ga, gr, x, w, tm):
    rows, d_model = x.shape

    def row(width):
        return pl.BlockSpec((tm, width), lambda i: (i, 0))

    def const(shape):
        return pl.BlockSpec(shape, lambda i: (0, 0), pipeline_mode=pl.Buffered(1))

    return pl.pallas_call(
        _out_proj_kernel,
        grid=(rows // tm,),
        in_specs=[row(attn.shape[1]), row(rnn.shape[1]), row(d_model), row(d_model), row(d_model),
                  const(w["o_attn"].shape), const(w["o_rnn"].shape), const(w["out"].shape),
                  const((1, d_model)), const((1, d_model))],
        out_specs=[row(d_model), row(d_model)],
        out_shape=[jax.ShapeDtypeStruct((rows, d_model), F32),
                   jax.ShapeDtypeStruct((rows, d_model), BF16)],
        compiler_params=_params("parallel"),
        name="out_proj",
    )(attn, rnn, ga, gr, x, w["o_attn"], w["o_rnn"], w["out"], w["g_post_mix"], w["g_pre_ffn"])


def _ffn_kernel(hn_ref, h_ref, wg_ref, wu_ref, wd_ref, g_ref, y_o, acc_s):
    j = pl.program_id(1)
    hn = hn_ref[...]
    gate = _dot(hn, wg_ref[...])
    up = _dot(hn, wu_ref[...])
    part = _dot((jax.nn.silu(gate) * up).astype(BF16), wd_ref[...])

    @pl.when(j == 0)
    def _():
        acc_s[...] = part

    @pl.when(j > 0)
    def _():
        acc_s[...] += part

    @pl.when(j == pl.num_programs(1) - 1)
    def _():
        y_o[...] = h_ref[...] + _rms_norm(acc_s[...], g_ref[...])


def _ffn(hn, h, w, tm, tf):
    rows, d_model = h.shape
    d_ff = w["gate"].shape[1]
    return pl.pallas_call(
        _ffn_kernel,
        grid=(rows // tm, d_ff // tf),
        in_specs=[pl.BlockSpec((tm, d_model), lambda i, j: (i, 0)),
                  pl.BlockSpec((tm, d_model), lambda i, j: (i, 0)),
                  pl.BlockSpec((d_model, tf), lambda i, j: (0, j)),
                  pl.BlockSpec((d_model, tf), lambda i, j: (0, j)),
                  pl.BlockSpec((tf, d_model), lambda i, j: (j, 0)),
                  pl.BlockSpec((1, d_model), lambda i, j: (0, 0))],
        out_specs=pl.BlockSpec((tm, d_model), lambda i, j: (i, 0)),
        out_shape=jax.ShapeDtypeStruct((rows, d_model), F32),
        scratch_shapes=[pltpu.VMEM((tm, d_model), F32)],
        compiler_params=_params("parallel", "arbitrary"),
        name="ffn",
    )(hn, h, w["gate"], w["up"], w["down"], w["g_post_ffn"])


def _layer_weights(l, n_heads, head_dim, g_pre_mix, w_in, b_f, conv_w, conv_b, w_rg, b_rg, w_ig,
                   b_ig, lru_lambda, w_o_attn, w_o_rnn, w_out, g_post_mix, g_pre_ffn, w_gate,
                   w_up, w_down, g_post_ffn):
    d_model = w_in.shape[1]
    n_blocks, bw = w_rg.shape[1], w_rg.shape[2]
    d_rnn = n_blocks * bw
    aw = n_heads * head_dim
    wi = w_in[l]
    edges = [0, aw, 2 * aw, 3 * aw, 3 * aw + n_heads, 3 * aw + n_heads + d_rnn,
             3 * aw + n_heads + 2 * d_rnn, 3 * aw + n_heads + 2 * d_rnn + d_model,
             3 * aw + n_heads + 2 * d_rnn + 2 * d_model]
    cols = {name: wi[:, edges[i]:edges[i + 1]].astype(BF16)
            for i, name in enumerate(("q", "k", "v", "f", "xr", "yr", "ga", "gr"))}
    cols["f"] = jnp.pad(cols["f"], ((0, 0), (0, LANES - n_heads)))
    cols["b_f"] = jnp.pad(b_f[l].reshape(1, n_heads), ((0, 0), (0, LANES - n_heads)))
    cols["n_heads"] = n_heads
    rnn = {
        "n_rnn_blocks": n_blocks, "rnn_block": bw,
        "conv_w": conv_w[l], "conv_b": conv_b[l].reshape(1, d_rnn),
        "b_rg": b_rg[l].reshape(1, d_rnn), "b_ig": b_ig[l].reshape(1, d_rnn),
        "lam": lru_lambda[l].reshape(1, d_rnn),
        "conv_w3": conv_w[l].reshape(CONV_WIDTH, n_blocks, bw),
        "conv_b2": conv_b[l].reshape(n_blocks, bw),
        "b_rg2": b_rg[l].reshape(n_blocks, bw), "b_ig2": b_ig[l].reshape(n_blocks, bw),
        "lam2": lru_lambda[l].reshape(n_blocks, bw),
        "w_rg": w_rg[l].astype(BF16), "w_ig": w_ig[l].astype(BF16),
    }
    mix = {"o_attn": w_o_attn[l].astype(BF16), "o_rnn": w_o_rnn[l].astype(BF16),
           "out": w_out[l].astype(BF16),
           "g_post_mix": g_post_mix[l].reshape(1, d_model),
           "g_pre_ffn": g_pre_ffn[l].reshape(1, d_model)}
    ffn = {"gate": w_gate[l].astype(BF16), "up": w_up[l].astype(BF16),
           "down": w_down[l].astype(BF16), "g_post_ffn": g_post_ffn[l].reshape(1, d_model)}
    return g_pre_mix[l].reshape(1, d_model), cols, rnn, mix, ffn


def kernel(x_prompt, x_sample, cache_k, cache_v, cache_logf, state_h, state_conv, page_table,
           g_pre_mix, w_in, b_f, conv_w, conv_b, w_rg, b_rg, w_ig, b_ig, lru_lambda,
           w_o_attn, w_o_rnn, w_out, g_post_mix, g_pre_ffn, w_gate, w_up, w_down, g_post_ffn):
    batch, seq, d_model = x_prompt.shape
    dec_batch, dec_seq, _ = x_sample.shape
    assert dec_seq == 1, "sample group carries one new token per sequence"
    depth, n_phys, page, n_heads, head_dim = cache_k.shape
    aw = n_heads * head_dim
    d_rnn = state_h.shape[2]
    n_pages = page_table.shape[1]
    assert page * n_heads == SUBLANES * LANES, "one log-forget page must fill one vreg tile"

    xp = x_prompt.reshape(batch * seq, d_model)
    xs = x_sample.reshape(dec_batch, d_model)
    outs = {name: [] for name in ("kp", "vp", "lp", "hp", "cp", "ks", "vs", "ls", "hs", "cs")}

    for l in range(depth):
        g_in, w_cols, w_rnn, w_mix, w_ffn = _layer_weights(
            l, n_heads, head_dim, g_pre_mix, w_in, b_f, conv_w, conv_b, w_rg, b_rg, w_ig, b_ig,
            lru_lambda, w_o_attn, w_o_rnn, w_out, g_post_mix, g_pre_ffn, w_gate, w_up, w_down,
            g_post_ffn)

        q, k, v, xr, yr, ga, gr, lf = _in_proj(xp, g_in, w_cols, tm=512, n_col_steps=4)
        lf_t = jnp.swapaxes(lf.reshape(batch, seq, n_heads), 1, 2).reshape(batch * n_heads, seq)
        c = _cumsum_lanes(lf_t)
        attn = _attn_prompt(q, k, v, c, batch, seq, n_heads, head_dim, tq=256)
        rnn, h_last = _rglru_prompt(xr, yr, w_rnn, batch, seq, tc=128)
        h, hn = _out_proj(attn, rnn, ga, gr, xp, w_mix, tm=256)
        xp_next = _ffn(hn, h, w_ffn, tm=512, tf=512)
        outs["kp"].append(k.reshape(batch, seq, n_heads, head_dim))
        outs["vp"].append(v.reshape(batch, seq, n_heads, head_dim))
        outs["lp"].append(lf.reshape(batch, seq, n_heads))
        outs["hp"].append(h_last)
        outs["cp"].append(xr.reshape(batch, seq, d_rnn)[:, seq - (CONV_WIDTH - 1):])

        q, k, v, xr, yr, ga, gr, lf = _in_proj(xs, g_in, w_cols, tm=dec_batch, n_col_steps=4)
        lf_pages = _gather_logf_pages(cache_logf[l].reshape(n_phys, SUBLANES, LANES), page_table)
        lf_past = jnp.swapaxes(lf_pages.reshape(dec_batch, n_pages * page, n_heads), 1, 2)
        c_past = _cumsum_lanes(lf_past.reshape(dec_batch * n_heads, n_pages * page))
        attn = _attn_sample(
            q.reshape(dec_batch, 1, aw), k.reshape(dec_batch, 1, aw), v.reshape(dec_batch, 1, aw),
            lf.reshape(dec_batch, n_heads, 1), c_past.reshape(dec_batch, n_heads, n_pages * page),
            cache_k[l].reshape(n_phys, page, aw), cache_v[l].reshape(n_phys, page, aw),
            page_table, n_heads, head_dim).reshape(dec_batch, aw)
        rnn, h_new, buf_new = _rglru_sample(xr, yr, jnp.swapaxes(state_conv[l], 0, 1),
                                            state_h[l], w_rnn)
        h, hn = _out_proj(attn, rnn, ga, gr, xs, w_mix, tm=dec_batch)
        xs_next = _ffn(hn, h, w_ffn, tm=dec_batch, tf=512)
        outs["ks"].append(k.reshape(dec_batch, 1, n_heads, head_dim))
        outs["vs"].append(v.reshape(dec_batch, 1, n_heads, head_dim))
        outs["ls"].append(lf.reshape(dec_batch, 1, n_heads))
        outs["hs"].append(h_new)
        outs["cs"].append(jnp.swapaxes(buf_new, 0, 1))

        xp, xs = xp_next, xs_next

    st = {name: jnp.stack(vals) for name, vals in outs.items()}
    return (xp.reshape(batch, seq, d_model), xs.reshape(dec_batch, 1, d_model),
            st["kp"], st["vp"], st["lp"], st["hp"], st["cp"],
            st["ks"], st["vs"], st["ls"], st["hs"], st["cs"])
```

```python
import functools
import math

import jax
import jax.numpy as jnp
from jax import lax
from jax.experimental import pallas as pl
from jax.experimental.pallas import tpu as pltpu

F32 = jnp.float32
BF16 = jnp.bfloat16

EPS = 1e-6
LRU_C = 8.0
CONV_WIDTH = 4
NEG_BIG = -1e30

SUBLANES = 8
LANES = 128
VMEM_LIMIT_BYTES = 56 * 1024 * 1024


def _params(*semantics):
    return pltpu.CompilerParams(dimension_semantics=semantics,
                                vmem_limit_bytes=VMEM_LIMIT_BYTES)


def _rms_norm(x, g):
    inv = lax.rsqrt(jnp.mean(x * x, axis=-1, keepdims=True) + EPS)
    return (x * inv) * g


def _log_sigmoid(x):
    return jnp.minimum(x, 0.0) - jnp.log1p(jnp.exp(-jnp.abs(x)))


def _softplus(x):
    return jnp.maximum(x, 0.0) + jnp.log1p(jnp.exp(-jnp.abs(x)))


def _gelu_tanh(x):
    c = math.sqrt(2.0 / math.pi)
    return x * (0.5 * (1.0 + jnp.tanh(c * (x + 0.044715 * (x * x * x)))))


def _dot(a, b):
    return jnp.dot(a, b, preferred_element_type=F32)


def _dot_nt(a, b):
    return lax.dot_general(a, b, (((1,), (1,)), ((), ())), preferred_element_type=F32)


def _in_proj_kernel(x_ref, g_ref, wq, wk, wv, wxr, wyr, wga, wgr, wf, bf_ref,
                    q_o, k16_o, v16_o, k32_o, v32_o, xr_o, yr_o, ga_o, gr_o, lf_o, xn_s):
    j = pl.program_id(1)

    @pl.when(j == 0)
    def _():
        xn = _rms_norm(x_ref[...], g_ref[...]).astype(BF16)
        xn_s[...] = xn
        f = _dot(xn, wf[...]) + bf_ref[...]
        lf_o[...] = _log_sigmoid(f)[:, :lf_o.shape[1]]

    xn = xn_s[...]
    for w, o in ((wq, q_o), (wxr, xr_o), (wyr, yr_o), (wga, ga_o), (wgr, gr_o)):
        o[...] = _dot(xn, w[...]).astype(o.dtype)

    tm = xn.shape[0]
    n_heads = k32_o.shape[0] // tm
    head_dim = k32_o.shape[1]
    heads_per_step = wk.shape[1] // head_dim
    for w, o16, o32 in ((wk, k16_o, k32_o), (wv, v16_o, v32_o)):
        res = _dot(xn, w[...])
        o16[...] = res.astype(o16.dtype)
        for hh in range(heads_per_step):
            head = j * heads_per_step + hh
            o32[pl.ds(head, tm, stride=n_heads), :] = res[:, hh * head_dim:(hh + 1) * head_dim]


def _in_proj(x, g, w, tm, n_col_steps):
    rows, d_model = x.shape
    n_heads, head_dim = w["n_heads"], w["head_dim"]
    names = ("q", "k", "v", "xr", "yr", "ga", "gr")
    tiles = {name: w[name].shape[1] // n_col_steps for name in names}
    grid = (rows // tm, n_col_steps)

    def w_spec(name):
        return pl.BlockSpec((d_model, tiles[name]), lambda i, j: (0, j))

    def o_spec(name):
        return pl.BlockSpec((tm, tiles[name]), lambda i, j: (i, j))

    def o_shape(name, dtype):
        return jax.ShapeDtypeStruct((rows, w[name].shape[1]), dtype)

    flat_spec = pl.BlockSpec((tm * n_heads, head_dim), lambda i, j: (i, 0))
    flat_shape = jax.ShapeDtypeStruct((rows * n_heads, head_dim), F32)
    return pl.pallas_call(
        _in_proj_kernel,
        grid=grid,
        in_specs=[pl.BlockSpec((tm, d_model), lambda i, j: (i, 0)),
                  pl.BlockSpec((1, d_model), lambda i, j: (0, 0))]
                 + [w_spec(name) for name in names]
                 + [pl.BlockSpec((d_model, LANES), lambda i, j: (0, 0)),
                    pl.BlockSpec((1, LANES), lambda i, j: (0, 0))],
        out_specs=[o_spec("q"), o_spec("k"), o_spec("v"), flat_spec, flat_spec,
                   o_spec("xr"), o_spec("yr"), o_spec("ga"), o_spec("gr"),
                   pl.BlockSpec((tm, n_heads), lambda i, j: (i, 0))],
        out_shape=[o_shape("q", BF16), o_shape("k", BF16), o_shape("v", BF16),
                   flat_shape, flat_shape,
                   o_shape("xr", F32), o_shape("yr", F32), o_shape("ga", F32), o_shape("gr", F32),
                   jax.ShapeDtypeStruct((rows, n_heads), F32)],
        scratch_shapes=[pltpu.VMEM((tm, d_model), BF16)],
        compiler_params=_params("parallel", "arbitrary"),
        name="in_proj",
    )(x, g, w["q"], w["k"], w["v"], w["xr"], w["yr"], w["ga"], w["gr"], w["f"], w["b_f"])


def _cumsum_kernel(x_ref, o_ref):
    x = x_ref[...]
    n = x.shape[-1]
    lane = lax.broadcasted_iota(jnp.int32, x.shape, 1)
    shift = 1
    while shift < n:
        x = x + jnp.where(lane >= shift, pltpu.roll(x, shift, axis=1), 0.0)
        shift *= 2
    o_ref[...] = x


def _cumsum_lanes(x):
    rows, n = x.shape
    tr = min(rows, 128)
    return pl.pallas_call(
        _cumsum_kernel,
        grid=(rows // tr,),
        in_specs=[pl.BlockSpec((tr, n), lambda i: (i, 0))],
        out_specs=pl.BlockSpec((tr, n), lambda i: (i, 0)),
        out_shape=jax.ShapeDtypeStruct((rows, n), F32),
        compiler_params=_params("parallel"),
        name="cumsum",
    )(x)


def _attn_prompt_kernel(q_ref, k_ref, v_ref, cq_ref, ck_ref, o_ref, *, tq, scale):
    seq = q_ref.shape[0]
    row = lax.broadcasted_iota(jnp.int32, (tq, tq), 0)
    col = lax.broadcasted_iota(jnp.int32, (tq, tq), 1)
    visible = col <= row
    for i in range(seq // tq):
        lo, hi = i * tq, (i + 1) * tq
        s = (_dot_nt(q_ref[lo:hi, :], k_ref[0:hi, :]) * scale
             + (cq_ref[0, 0, lo:hi, :] - ck_ref[0, 0, :, 0:hi]))
        diag = jnp.where(visible, s[:, lo:hi], NEG_BIG)
        s = diag if i == 0 else jnp.concatenate([s[:, 0:lo], diag], axis=1)
        m = jnp.max(s, axis=-1, keepdims=True)
        p = jnp.exp(s - m)
        l = jnp.sum(p, axis=-1, keepdims=True)
        o_ref[lo:hi, :] = (_dot(p.astype(BF16), v_ref[0:hi, :]) / l).astype(o_ref.dtype)


def _attn_prompt(q, k, v, c, batch, seq, n_heads, head_dim, tq):
    cq = c.reshape(batch, n_heads, seq, 1)
    ck = c.reshape(batch, n_heads, 1, seq)
    head_spec = pl.BlockSpec((seq, head_dim), lambda b, h: (b, h))
    return pl.pallas_call(
        functools.partial(_attn_prompt_kernel, tq=tq, scale=head_dim ** -0.5),
        grid=(batch, n_heads),
        in_specs=[head_spec, head_spec, head_spec,
                  pl.BlockSpec((1, 1, seq, 1), lambda b, h: (b, h, 0, 0)),
                  pl.BlockSpec((1, 1, 1, seq), lambda b, h: (b, h, 0, 0))],
        out_specs=head_spec,
        out_shape=jax.ShapeDtypeStruct(q.shape, BF16),
        compiler_params=_params("parallel", "parallel"),
        name="attn_prompt",
    )(q, k, v, cq, ck)


def _gather_pages_kernel(pt_ref, *refs):
    del pt_ref
    n_pages = len(refs) - 1
    o_ref = refs[-1]
    for p in range(n_pages):
        o_ref[0, p] = refs[p][0]


def _gather_logf_pages(cache_lf_packed, page_table):
    batch, n_pages = page_table.shape
    blk = cache_lf_packed.shape[1:]

    def page_spec(p):
        return pl.BlockSpec((1,) + blk, lambda b, pt: (pt[b, p], 0, 0))

    return pl.pallas_call(
        _gather_pages_kernel,
        grid_spec=pltpu.PrefetchScalarGridSpec(
            num_scalar_prefetch=1,
            grid=(batch,),
            in_specs=[page_spec(p) for p in range(n_pages)],
            out_specs=pl.BlockSpec((1, n_pages) + blk, lambda b, pt: (b, 0, 0, 0)),
        ),
        out_shape=jax.ShapeDtypeStruct((batch, n_pages) + blk, F32),
        compiler_params=_params("arbitrary"),
        name="gather_logf",
    )(page_table, *([cache_lf_packed] * n_pages))


def _attn_sample_kernel(pt_ref, q_ref, kn_ref, vn_ref, lfn_ref, cp_ref, *refs,
                        n_pages, scale):
    del pt_ref
    k_refs = refs[:n_pages]
    v_refs = refs[n_pages:2 * n_pages]
    o_ref = refs[2 * n_pages]
    s_s = refs[2 * n_pages + 1]
    page, n_heads, head_dim = k_refs[0].shape[1:]
    pw = page * n_heads
    total = n_pages * pw

    head_of_row = lax.broadcasted_iota(jnp.int32, (n_heads, pw), 0)
    lane = lax.broadcasted_iota(jnp.int32, (n_heads, pw), 1)
    on_diag = head_of_row == lane % n_heads

    q = q_ref[0]
    c_past = cp_ref[0]
    is_last = lane == (page - 1) * n_heads + head_of_row
    c_last = jnp.sum(jnp.where(is_last, c_past[:, total - pw:total], 0.0), axis=-1, keepdims=True)
    c_new = c_last + lfn_ref[0]

    for p in range(n_pages):
        sl = slice(p * pw, (p + 1) * pw)
        kf = k_refs[p][0].reshape(pw, head_dim).astype(BF16)
        s = _dot_nt(q, kf) * scale + (c_new - c_past[:, sl])
        s_s[:, sl] = jnp.where(on_diag, s, NEG_BIG)

    kn = kn_ref[0].astype(BF16).astype(F32)
    vn = vn_ref[0].astype(BF16).astype(F32)
    s_new = jnp.sum(q.astype(F32) * kn, axis=-1, keepdims=True) * scale

    s_past = s_s[...]
    m = jnp.maximum(jnp.max(s_past, axis=-1, keepdims=True), s_new)
    p_past = jnp.exp(s_past - m)
    p_new = jnp.exp(s_new - m)
    l = jnp.sum(p_past, axis=-1, keepdims=True) + p_new
    p16 = p_past.astype(BF16)

    acc = p_new.astype(BF16).astype(F32) * vn
    for p in range(n_pages):
        vf = v_refs[p][0].reshape(pw, head_dim).astype(BF16)
        acc = acc + _dot(p16[:, p * pw:(p + 1) * pw], vf)
    o_ref[0] = (acc / l).astype(o_ref.dtype)


def _attn_sample(q, k_new, v_new, lf_new, c_past, cache_k, cache_v, page_table):
    batch, n_pages = page_table.shape
    _, page, n_heads, head_dim = cache_k.shape
    total = n_pages * page * n_heads

    def row_spec(shape):
        return pl.BlockSpec((1,) + shape, lambda b, pt: (b, 0, 0))

    def page_spec(p):
        return pl.BlockSpec((1, page, n_heads, head_dim), lambda b, pt: (pt[b, p], 0, 0, 0))

    return pl.pallas_call(
        functools.partial(_attn_sample_kernel, n_pages=n_pages, scale=head_dim ** -0.5),
        grid_spec=pltpu.PrefetchScalarGridSpec(
            num_scalar_prefetch=1,
            grid=(batch,),
            in_specs=[row_spec((n_heads, head_dim))] * 3
                     + [row_spec((n_heads, 1)), row_spec((1, total))]
                     + [page_spec(p) for p in range(n_pages)] * 2,
            out_specs=row_spec((n_heads, head_dim)),
            scratch_shapes=[pltpu.VMEM((n_heads, total), F32)],
        ),
        out_shape=jax.ShapeDtypeStruct((batch, n_heads, head_dim), BF16),
        compiler_params=_params("arbitrary"),
        name="attn_sample",
    )(page_table, q, k_new, v_new, lf_new, c_past,
      *([cache_k] * n_pages), *([cache_v] * n_pages))


def _rglru_gates(conv, r_pre, i_pre, b_rg, b_ig, neg_c_softplus):
    r = jax.nn.sigmoid(r_pre + b_rg)
    i_g = jax.nn.sigmoid(i_pre + b_ig)
    log_a = r * neg_c_softplus
    a = jnp.exp(log_a)
    b = jnp.sqrt(-jnp.tanh(log_a) * (1.0 + a * a)) * (i_g * conv)
    return a, b


def _rglru_prompt_kernel(xr_ref, yr_ref, cw_ref, cb_ref, wrg_ref, wig_ref, brg_ref, big_ref,
                         lam_ref, o_ref, hl_ref, xc_s, conv_s, rp_s, ip_s, a_s, b_s, h_s):
    c = pl.program_id(0)
    nb, tc, n_blocks, bw = xr_ref.shape
    halo = CONV_WIDTH - 1
    rows = nb * tc

    @pl.when(c == 0)
    def _():
        xc_s[:, 0:halo] = jnp.zeros((nb, halo, n_blocks, bw), F32)
        h_s[...] = jnp.zeros_like(h_s)

    @pl.when(c > 0)
    def _():
        xc_s[:, 0:halo] = xc_s[:, tc:tc + halo]

    xc_s[:, halo:halo + tc] = xr_ref[...]

    conv = cb_ref[...] + xc_s[:, 0:tc] * cw_ref[0]
    for i in range(1, CONV_WIDTH):
        conv = conv + xc_s[:, i:i + tc] * cw_ref[i]

    conv_s[...] = conv.reshape(rows * n_blocks, bw)
    for n in range(n_blocks):
        xn = conv_s[pl.ds(n, rows, stride=n_blocks), :].astype(BF16)
        rp_s[pl.ds(n, rows, stride=n_blocks), :] = _dot(xn, wrg_ref[n])
        ip_s[pl.ds(n, rows, stride=n_blocks), :] = _dot(xn, wig_ref[n])

    neg_c_softplus = -LRU_C * _softplus(-lam_ref[...])
    a, b = _rglru_gates(conv,
                        rp_s[...].reshape(nb, tc, n_blocks, bw),
                        ip_s[...].reshape(nb, tc, n_blocks, bw),
                        brg_ref[...], big_ref[...], neg_c_softplus)
    a_s[...] = a
    b_s[...] = b

    def step(t, hs):
        new = tuple(a_s[j, t] * hs[j] + b_s[j, t] for j in range(nb))
        for j in range(nb):
            b_s[j, t] = new[j]
        return new

    hs = lax.fori_loop(0, tc, step, tuple(h_s[j] for j in range(nb)), unroll=8)
    for j in range(nb):
        h_s[j] = hs[j]

    o_ref[...] = b_s[...] * _gelu_tanh(yr_ref[...])
    hl_ref[...] = h_s[...]


def _rglru_prompt(xr, yr, w, batch, seq, tc):
    n_blocks, bw = w["n_rnn_blocks"], w["rnn_block"]
    xr4 = xr.reshape(batch, seq, n_blocks, bw)
    yr4 = yr.reshape(batch, seq, n_blocks, bw)
    rows = batch * tc
    halo = CONV_WIDTH - 1

    def const(shape):
        return pl.BlockSpec(shape, lambda c: (0,) * len(shape))

    seq_spec = pl.BlockSpec((batch, tc, n_blocks, bw), lambda c: (0, c, 0, 0))
    out, h_last = pl.pallas_call(
        _rglru_prompt_kernel,
        grid=(seq // tc,),
        in_specs=[seq_spec, seq_spec,
                  const((CONV_WIDTH, n_blocks, bw)), const((n_blocks, bw)),
                  const((n_blocks, bw, bw)), const((n_blocks, bw, bw)),
                  const((n_blocks, bw)), const((n_blocks, bw)), const((n_blocks, bw))],
        out_specs=[seq_spec, const((batch, n_blocks, bw))],
        out_shape=[jax.ShapeDtypeStruct(xr4.shape, F32),
                   jax.ShapeDtypeStruct((batch, n_blocks, bw), F32)],
        scratch_shapes=[pltpu.VMEM((batch, tc + halo, n_blocks, bw), F32),
                        pltpu.VMEM((rows * n_blocks, bw), F32),
                        pltpu.VMEM((rows * n_blocks, bw), F32),
                        pltpu.VMEM((rows * n_blocks, bw), F32),
                        pltpu.VMEM((batch, tc, n_blocks, bw), F32),
                        pltpu.VMEM((batch, tc, n_blocks, bw), F32),
                        pltpu.VMEM((batch, n_blocks, bw), F32)],
        compiler_params=_params("arbitrary"),
        name="rglru_prompt",
    )(xr4, yr4, w["conv_w3"], w["conv_b2"], w["w_rg"], w["w_ig"], w["b_rg2"], w["b_ig2"],
      w["lam2"])
    return out.reshape(batch * seq, n_blocks * bw), h_last.reshape(batch, n_blocks * bw)


def _rglru_sample_kernel(xr_ref, yr_ref, buf_ref, h0_ref, cw_ref, cb_ref, wrg_ref, wig_ref,
                         brg_ref, big_ref, lam_ref, o_ref, h_ref, nbuf_ref):
    halo = CONV_WIDTH - 1
    n_blocks, bw = wrg_ref.shape[0], wrg_ref.shape[1]
    xr = xr_ref[...]
    conv = cb_ref[...] + xr * cw_ref[halo:halo + 1, :]
    for i in range(halo):
        conv = conv + buf_ref[i] * cw_ref[i:i + 1, :]
    r_parts, i_parts = [], []
    for n in range(n_blocks):
        xn = conv[:, n * bw:(n + 1) * bw].astype(BF16)
        r_parts.append(_dot(xn, wrg_ref[n]))
        i_parts.append(_dot(xn, wig_ref[n]))
    neg_c_softplus = -LRU_C * _softplus(-lam_ref[...])
    a, b = _rglru_gates(conv, jnp.concatenate(r_parts, axis=1), jnp.concatenate(i_parts, axis=1),
                        brg_ref[...], big_ref[...], neg_c_softplus)
    h = a * h0_ref[...] + b
    h_ref[...] = h
    o_ref[...] = h * _gelu_tanh(yr_ref[...])
    for i in range(halo - 1):
        nbuf_ref[i] = buf_ref[i + 1]
    nbuf_ref[halo - 1] = xr


def _rglru_sample(xr, yr, conv_buf_t, h0, w):
    rows, width = xr.shape
    halo = CONV_WIDTH - 1
    return pl.pallas_call(
        _rglru_sample_kernel,
        out_shape=[jax.ShapeDtypeStruct((rows, width), F32),
                   jax.ShapeDtypeStruct((rows, width), F32),
                   jax.ShapeDtypeStruct((halo, rows, width), F32)],
        compiler_params=pltpu.CompilerParams(vmem_limit_bytes=VMEM_LIMIT_BYTES),
        name="rglru_sample",
    )(xr, yr, conv_buf_t, h0, w["conv_w"], w["conv_b"], w["w_rg"], w["w_ig"],
      w["b_rg"], w["b_ig"], w["lam"])


def _out_proj_kernel(attn_ref, rnn_ref, ga_ref, gr_ref, x_ref, woa_ref, wor_ref, wout_ref,
                     gpm_ref, gpf_ref, h_o, hn_o):
    o = (jax.nn.sigmoid(ga_ref[...]) * _dot(attn_ref[...].astype(BF16), woa_ref[...])
         + jax.nn.sigmoid(gr_ref[...]) * _dot(rnn_ref[...].astype(BF16), wor_ref[...]))
    mix = _dot(o.astype(BF16), wout_ref[...])
    h = x_ref[...] + _rms_norm(mix, gpm_ref[...])
    h_o[...] = h
    hn_o[...] = _rms_norm(h, gpf_ref[...]).astype(hn_o.dtype)


def _out_proj(attn, rnn, ga, gr, x, w, tm):
    rows, d_model = x.shape

    def row(width):
        return pl.BlockSpec((tm, width), lambda i: (i, 0))

    def const(shape):
        return pl.BlockSpec(shape, lambda i: (0, 0), pipeline_mode=pl.Buffered(1))

    return pl.pallas_call(
        _out_proj_kernel,
        grid=(rows // tm,),
        in_specs=[row(attn.shape[1]), row(rnn.shape[1]), row(d_model), row(d_model), row(d_model),
                  const(w["o_attn"].shape), const(w["o_rnn"].shape), const(w["out"].shape),
                  const((1, d_model)), const((1, d_model))],
        out_specs=[row(d_model), row(d_model)],
        out_shape=[jax.ShapeDtypeStruct((rows, d_model), F32),
                   jax.ShapeDtypeStruct((rows, d_model), BF16)],
        compiler_params=_params("parallel"),
        name="out_proj",
    )(attn, rnn, ga, gr, x, w["o_attn"], w["o_rnn"], w["out"], w["g_post_mix"], w["g_pre_ffn"])


def _ffn_kernel(hn_ref, h_ref, wg_ref, wu_ref, wd_ref, g_ref, y_o, acc_s):
    j = pl.program_id(1)
    hn = hn_ref[...]
    gate = _dot(hn, wg_ref[...])
    up = _dot(hn, wu_ref[...])
    part = _dot((jax.nn.silu(gate) * up).astype(BF16), wd_ref[...])

    @pl.when(j == 0)
    def _():
        acc_s[...] = part

    @pl.when(j > 0)
    def _():
        acc_s[...] += part

    @pl.when(j == pl.num_programs(1) - 1)
    def _():
        y_o[...] = h_ref[...] + _rms_norm(acc_s[...], g_ref[...])


def _ffn(hn, h, w, tm, tf):
    rows, d_model = h.shape
    d_ff = w["gate"].shape[1]
    return pl.pallas_call(
        _ffn_kernel,
        grid=(rows // tm, d_ff // tf),
        in_specs=[pl.BlockSpec((tm, d_model), lambda i, j: (i, 0)),
                  pl.BlockSpec((tm, d_model), lambda i, j: (i, 0)),
                  pl.BlockSpec((d_model, tf), lambda i, j: (0, j)),
                  pl.BlockSpec((d_model, tf), lambda i, j: (0, j)),
                  pl.BlockSpec((tf, d_model), lambda i, j: (j, 0)),
                  pl.BlockSpec((1, d_model), lambda i, j: (0, 0))],
        out_specs=pl.BlockSpec((tm, d_model), lambda i, j: (i, 0)),
        out_shape=jax.ShapeDtypeStruct((rows, d_model), F32),
        scratch_shapes=[pltpu.VMEM((tm, d_model), F32)],
        compiler_params=_params("parallel", "arbitrary"),
        name="ffn",
    )(hn, h, w["gate"], w["up"], w["down"], w["g_post_ffn"])


def _layer_weights(l, n_heads, head_dim, g_pre_mix, w_in, b_f, conv_w, conv_b, w_rg, b_rg, w_ig,
                   b_ig, lru_lambda, w_o_attn, w_o_rnn, w_out, g_post_mix, g_pre_ffn, w_gate,
                   w_up, w_down, g_post_ffn):
    d_model = w_in.shape[1]
    n_blocks, bw = w_rg.shape[1], w_rg.shape[2]
    d_rnn = n_blocks * bw
    aw = n_heads * head_dim
    wi = w_in[l]
    edges = [0, aw, 2 * aw, 3 * aw, 3 * aw + n_heads, 3 * aw + n_heads + d_rnn,
             3 * aw + n_heads + 2 * d_rnn, 3 * aw + n_heads + 2 * d_rnn + d_model,
             3 * aw + n_heads + 2 * d_rnn + 2 * d_model]
    cols = {name: wi[:, edges[i]:edges[i + 1]].astype(BF16)
            for i, name in enumerate(("q", "k", "v", "f", "xr", "yr", "ga", "gr"))}
    cols["f"] = jnp.pad(cols["f"], ((0, 0), (0, LANES - n_heads)))
    cols["b_f"] = jnp.pad(b_f[l].reshape(1, n_heads), ((0, 0), (0, LANES - n_heads)))
    cols["n_heads"] = n_heads
    cols["head_dim"] = head_dim
    rnn = {
        "n_rnn_blocks": n_blocks, "rnn_block": bw,
        "conv_w": conv_w[l], "conv_b": conv_b[l].reshape(1, d_rnn),
        "b_rg": b_rg[l].reshape(1, d_rnn), "b_ig": b_ig[l].reshape(1, d_rnn),
        "lam": lru_lambda[l].reshape(1, d_rnn),
        "conv_w3": conv_w[l].reshape(CONV_WIDTH, n_blocks, bw),
        "conv_b2": conv_b[l].reshape(n_blocks, bw),
        "b_rg2": b_rg[l].reshape(n_blocks, bw), "b_ig2": b_ig[l].reshape(n_blocks, bw),
        "lam2": lru_lambda[l].reshape(n_blocks, bw),
        "w_rg": w_rg[l].astype(BF16), "w_ig": w_ig[l].astype(BF16),
    }
    mix = {"o_attn": w_o_attn[l].astype(BF16), "o_rnn": w_o_rnn[l].astype(BF16),
           "out": w_out[l].astype(BF16),
           "g_post_mix": g_post_mix[l].reshape(1, d_model),
           "g_pre_ffn": g_pre_ffn[l].reshape(1, d_model)}
    ffn = {"gate": w_gate[l].astype(BF16), "up": w_up[l].astype(BF16),
           "down": w_down[l].astype(BF16), "g_post_ffn": g_post_ffn[l].reshape(1, d_model)}
    return g_pre_mix[l].reshape(1, d_model), cols, rnn, mix, ffn


def kernel(x_prompt, x_sample, cache_k, cache_v, cache_logf, state_h, state_conv, page_table,
           g_pre_mix, w_in, b_f, conv_w, conv_b, w_rg, b_rg, w_ig, b_ig, lru_lambda,
           w_o_attn, w_o_rnn, w_out, g_post_mix, g_pre_ffn, w_gate, w_up, w_down, g_post_ffn):
    batch, seq, d_model = x_prompt.shape
    dec_batch, dec_seq, _ = x_sample.shape
    assert dec_seq == 1, "sample group carries one new token per sequence"
    depth, n_phys, page, n_heads, head_dim = cache_k.shape
    aw = n_heads * head_dim
    d_rnn = state_h.shape[2]
    n_pages = page_table.shape[1]
    assert page * n_heads == SUBLANES * LANES, "one log-forget page must fill one vreg tile"

    xp = x_prompt.reshape(batch * seq, d_model)
    xs = x_sample.reshape(dec_batch, d_model)
    outs = {name: [] for name in ("kp", "vp", "lp", "hp", "cp", "ks", "vs", "ls", "hs", "cs")}

    for l in range(depth):
        g_in, w_cols, w_rnn, w_mix, w_ffn = _layer_weights(
            l, n_heads, head_dim, g_pre_mix, w_in, b_f, conv_w, conv_b, w_rg, b_rg, w_ig, b_ig,
            lru_lambda, w_o_attn, w_o_rnn, w_out, g_post_mix, g_pre_ffn, w_gate, w_up, w_down,
            g_post_ffn)

        q, k16, v16, k, v, xr, yr, ga, gr, lf = _in_proj(xp, g_in, w_cols, tm=512, n_col_steps=4)
        lf_t = jnp.swapaxes(lf.reshape(batch, seq, n_heads), 1, 2).reshape(batch * n_heads, seq)
        c = _cumsum_lanes(lf_t)
        attn = _attn_prompt(q, k16, v16, c, batch, seq, n_heads, head_dim, tq=256)
        rnn, h_last = _rglru_prompt(xr, yr, w_rnn, batch, seq, tc=128)
        h, hn = _out_proj(attn, rnn, ga, gr, xp, w_mix, tm=256)
        xp_next = _ffn(hn, h, w_ffn, tm=512, tf=512)
        outs["kp"].append(k.reshape(batch, seq, n_heads, head_dim))
        outs["vp"].append(v.reshape(batch, seq, n_heads, head_dim))
        outs["lp"].append(lf.reshape(batch, seq, n_heads))
        outs["hp"].append(h_last)
        outs["cp"].append(xr.reshape(batch, seq, d_rnn)[:, seq - (CONV_WIDTH - 1):])

        q, _, _, k, v, xr, yr, ga, gr, lf = _in_proj(xs, g_in, w_cols, tm=dec_batch, n_col_steps=4)
        k = k.reshape(dec_batch, n_heads, head_dim)
        v = v.reshape(dec_batch, n_heads, head_dim)
        n_past = n_pages * page
        lf_pages = _gather_logf_pages(cache_logf[l].reshape(n_phys, SUBLANES, LANES), page_table)
        lf_past = jnp.swapaxes(lf_pages.reshape(dec_batch, n_past, n_heads), 1, 2)
        c_past = _cumsum_lanes(lf_past.reshape(dec_batch * n_heads, n_past))
        c_past = jnp.swapaxes(c_past.reshape(dec_batch, n_heads, n_past), 1, 2)
        attn = _attn_sample(
            q.reshape(dec_batch, n_heads, head_dim), k, v, lf.reshape(dec_batch, n_heads, 1),
            c_past.reshape(dec_batch, 1, n_past * n_heads), cache_k[l], cache_v[l],
            page_table).reshape(dec_batch, aw)
        rnn, h_new, buf_new = _rglru_sample(xr, yr, jnp.swapaxes(state_conv[l], 0, 1),
                                            state_h[l], w_rnn)
        h, hn = _out_proj(attn, rnn, ga, gr, xs, w_mix, tm=dec_batch)
        xs_next = _ffn(hn, h, w_ffn, tm=dec_batch, tf=512)
        outs["ks"].append(k.reshape(dec_batch, 1, n_heads, head_dim))
        outs["vs"].append(v.reshape(dec_batch, 1, n_heads, head_dim))
        outs["ls"].append(lf.reshape(dec_batch, 1, n_heads))
        outs["hs"].append(h_new)
        outs["cs"].append(jnp.swapaxes(buf_new, 0, 1))

        xp, xs = xp_next, xs_next

    st = {name: jnp.stack(vals) for name, vals in outs.items()}
    return (xp.reshape(batch, seq, d_model), xs.reshape(dec_batch, 1, d_model),
            st["kp"], st["vp"], st["lp"], st["hp"], st["cp"],
            st["ks"], st["vs"], st["ls"], st["hs"], st["cs"])
```

```python
import functools
import math

import jax
import jax.numpy as jnp
from jax import lax
from jax.experimental import pallas as pl
from jax.experimental.pallas import tpu as pltpu

F32 = jnp.float32
BF16 = jnp.bfloat16

EPS = 1e-6
LRU_C = 8.0
CONV_WIDTH = 4
NEG_BIG = -1e30

SUBLANES = 8
LANES = 128
VMEM_LIMIT_BYTES = 56 * 1024 * 1024


def _params(*semantics):
    return pltpu.CompilerParams(dimension_semantics=semantics,
                                vmem_limit_bytes=VMEM_LIMIT_BYTES)


def _rms_norm(x, g):
    inv = lax.rsqrt(jnp.mean(x * x, axis=-1, keepdims=True) + EPS)
    return (x * inv) * g


def _log_sigmoid(x):
    return jnp.minimum(x, 0.0) - jnp.log1p(jnp.exp(-jnp.abs(x)))


def _softplus(x):
    return jnp.maximum(x, 0.0) + jnp.log1p(jnp.exp(-jnp.abs(x)))


def _gelu_tanh(x):
    c = math.sqrt(2.0 / math.pi)
    return x * (0.5 * (1.0 + jnp.tanh(c * (x + 0.044715 * (x * x * x)))))


def _dot(a, b):
    return jnp.dot(a, b, preferred_element_type=F32)


def _dot_nt(a, b):
    return lax.dot_general(a, b, (((1,), (1,)), ((), ())), preferred_element_type=F32)


def _prefix_sum(x, axis, start=1):
    idx = lax.broadcasted_iota(jnp.int32, x.shape, axis)
    shift = start
    while shift < x.shape[axis]:
        x = x + jnp.where(idx >= shift, pltpu.roll(x, shift, axis=axis), 0.0)
        shift *= 2
    return x


def _in_proj_kernel(x_ref, g_ref, wq, wk, wv, wxr, wyr, wga, wgr, wf, bf_ref,
                    q_o, k16_o, v16_o, k32_o, v32_o, xr_o, yr_o, ga_o, gr_o, lf_o, *rest,
                    tiles_per_seq):
    i, j = pl.program_id(0), pl.program_id(1)
    xn_s = rest[-1]

    @pl.when(j == 0)
    def _():
        xn = _rms_norm(x_ref[...], g_ref[...]).astype(BF16)
        xn_s[...] = xn
        lf = _log_sigmoid(_dot(xn, wf[...]) + bf_ref[...])
        lf_o[...] = lf[:, :lf_o.shape[1]]
        if tiles_per_seq is not None:
            c_o, carry_s = rest[0], rest[1]

            @pl.when(lax.rem(i, tiles_per_seq) == 0)
            def _():
                carry_s[...] = jnp.zeros_like(carry_s)

            c = _prefix_sum(lf, axis=0) + carry_s[...]
            c_o[...] = c[:, :c_o.shape[1]]
            carry_s[...] = c[c.shape[0] - 1:, :]

    xn = xn_s[...]
    for w, o in ((wq, q_o), (wxr, xr_o), (wyr, yr_o), (wga, ga_o), (wgr, gr_o)):
        o[...] = _dot(xn, w[...]).astype(o.dtype)

    tm = xn.shape[0]
    n_heads = k32_o.shape[0] // tm
    head_dim = k32_o.shape[1]
    heads_per_step = wk.shape[1] // head_dim
    for w, o16, o32 in ((wk, k16_o, k32_o), (wv, v16_o, v32_o)):
        res = _dot(xn, w[...])
        o16[...] = res.astype(o16.dtype)
        for hh in range(heads_per_step):
            head = j * heads_per_step + hh
            o32[pl.ds(head, tm, stride=n_heads), :] = res[:, hh * head_dim:(hh + 1) * head_dim]


def _in_proj(x, g, w, tm, n_col_steps, seq=None):
    rows, d_model = x.shape
    n_heads, head_dim = w["n_heads"], w["head_dim"]
    cols = {}
    for arr, names in (("qkv", ("q", "k", "v")), ("rest", ("xr", "yr", "ga", "gr"))):
        start = 0
        for name in names:
            cols[name] = (arr, start, w["widths"][name])
            start += w["widths"][name]
    order = ("q", "k", "v", "xr", "yr", "ga", "gr")
    tiles = {name: cols[name][2] // n_col_steps for name in order}
    grid = (rows // tm, n_col_steps)

    def w_spec(name):
        first_block = cols[name][1] // tiles[name]
        return pl.BlockSpec((d_model, tiles[name]), lambda i, j: (0, first_block + j))

    def o_spec(name):
        return pl.BlockSpec((tm, tiles[name]), lambda i, j: (i, j))

    def o_shape(name, dtype):
        return jax.ShapeDtypeStruct((rows, cols[name][2]), dtype)

    def per_head():
        return pl.BlockSpec((tm, n_heads), lambda i, j: (i, 0))

    flat_spec = pl.BlockSpec((tm * n_heads, head_dim), lambda i, j: (i, 0))
    flat_shape = jax.ShapeDtypeStruct((rows * n_heads, head_dim), F32)
    out_specs = [o_spec("q"), o_spec("k"), o_spec("v"), flat_spec, flat_spec,
                 o_spec("xr"), o_spec("yr"), o_spec("ga"), o_spec("gr"), per_head()]
    out_shape = [o_shape("q", BF16), o_shape("k", BF16), o_shape("v", BF16),
                 flat_shape, flat_shape,
                 o_shape("xr", F32), o_shape("yr", F32), o_shape("ga", F32), o_shape("gr", F32),
                 jax.ShapeDtypeStruct((rows, n_heads), F32)]
    scratch = [pltpu.VMEM((tm, d_model), BF16)]
    if seq is not None:
        out_specs.append(per_head())
        out_shape.append(jax.ShapeDtypeStruct((rows, n_heads), F32))
        scratch.insert(0, pltpu.VMEM((1, LANES), F32))
    return pl.pallas_call(
        functools.partial(_in_proj_kernel,
                          tiles_per_seq=None if seq is None else seq // tm),
        grid=grid,
        in_specs=[pl.BlockSpec((tm, d_model), lambda i, j: (i, 0)),
                  pl.BlockSpec((1, d_model), lambda i, j: (0, 0))]
                 + [w_spec(name) for name in order]
                 + [pl.BlockSpec((d_model, LANES), lambda i, j: (0, 0)),
                    pl.BlockSpec((1, LANES), lambda i, j: (0, 0))],
        out_specs=out_specs,
        out_shape=out_shape,
        scratch_shapes=scratch,
        compiler_params=_params("arbitrary", "arbitrary"),
        name="in_proj",
    )(x, g, *[w[cols[name][0]] for name in order], w["f"], w["b_f"])


def _attn_prompt_kernel(q_ref, k_ref, v_ref, c_ref, ct_ref, o_ref, *, tq, scale):
    seq = q_ref.shape[0]
    h = pl.program_id(1)
    head_lane = lax.broadcasted_iota(jnp.int32, c_ref.shape, 1)
    cq = jnp.sum(jnp.where(head_lane == h, c_ref[...], 0.0), axis=-1, keepdims=True)
    ck = ct_ref[0, pl.ds(h, 1), :]
    row = lax.broadcasted_iota(jnp.int32, (tq, tq), 0)
    col = lax.broadcasted_iota(jnp.int32, (tq, tq), 1)
    visible = col <= row
    for i in range(seq // tq):
        lo, hi = i * tq, (i + 1) * tq
        s = (_dot_nt(q_ref[lo:hi, :], k_ref[0:hi, :]) * scale
             + (cq[lo:hi, :] - ck[:, 0:hi]))
        diag = jnp.where(visible, s[:, lo:hi], NEG_BIG)
        s = diag if i == 0 else jnp.concatenate([s[:, 0:lo], diag], axis=1)
        m = jnp.max(s, axis=-1, keepdims=True)
        p = jnp.exp(s - m)
        l = jnp.sum(p, axis=-1, keepdims=True)
        o_ref[lo:hi, :] = (_dot(p.astype(BF16), v_ref[0:hi, :]) / l).astype(o_ref.dtype)


def _attn_prompt(q, k, v, c, batch, seq, n_heads, head_dim, tq):
    ct = jnp.swapaxes(c.reshape(batch, seq, n_heads), 1, 2)
    head_spec = pl.BlockSpec((seq, head_dim), lambda b, h: (b, h))
    return pl.pallas_call(
        functools.partial(_attn_prompt_kernel, tq=tq, scale=head_dim ** -0.5),
        grid=(batch, n_heads),
        in_specs=[head_spec, head_spec, head_spec,
                  pl.BlockSpec((seq, n_heads), lambda b, h: (b, 0)),
                  pl.BlockSpec((1, n_heads, seq), lambda b, h: (b, 0, 0))],
        out_specs=head_spec,
        out_shape=jax.ShapeDtypeStruct(q.shape, BF16),
        compiler_params=_params("parallel", "parallel"),
        name="attn_prompt",
    )(q, k, v, c, ct)


def _paged_logf_sums(lf_pages, n_heads):
    n_pages, pw = lf_pages.shape
    within = _prefix_sum(lf_pages, axis=1, start=n_heads)
    lane = lax.broadcasted_iota(jnp.int32, (n_pages, LANES), 1)
    totals = jnp.where(lane >= LANES - n_heads, within[:, pw - LANES:], 0.0)
    shift = n_heads
    while shift < LANES:
        totals = totals + pltpu.roll(totals, LANES - shift, axis=1)
        shift *= 2
    upto = _prefix_sum(totals, axis=0)
    before = upto - totals
    sums = within + jnp.concatenate([before] * (pw // LANES), axis=1)
    return sums, upto[n_pages - 1:, :]


def _attn_sample_kernel(pt_ref, q_ref, kn_ref, vn_ref, lfn_ref, *refs, n_pages, scale):
    del pt_ref
    k_refs = refs[:n_pages]
    v_refs = refs[n_pages:2 * n_pages]
    lf_refs = refs[2 * n_pages:3 * n_pages]
    o_ref = refs[3 * n_pages]
    s_s = refs[3 * n_pages + 1]
    page, n_heads, head_dim = k_refs[0].shape[1:]
    pw = page * n_heads

    head_of_row = lax.broadcasted_iota(jnp.int32, (n_heads, pw), 0)
    lane = lax.broadcasted_iota(jnp.int32, (n_heads, pw), 1)
    on_diag = head_of_row == lane % n_heads

    q = q_ref[0]
    c_past, c_total = _paged_logf_sums(
        jnp.concatenate([lf_refs[p][0] for p in range(n_pages)], axis=0), n_heads)
    own_lane = lane[:, :LANES] == head_of_row[:, :LANES]
    c_last = jnp.sum(jnp.where(own_lane, c_total, 0.0), axis=-1, keepdims=True)
    c_new = c_last + lfn_ref[0]

    for p in range(n_pages):
        sl = slice(p * pw, (p + 1) * pw)
        kf = k_refs[p][0].reshape(pw, head_dim).astype(BF16)
        s = _dot_nt(q, kf) * scale + (c_new - c_past[p:p + 1, :])
        s_s[:, sl] = jnp.where(on_diag, s, NEG_BIG)

    kn = kn_ref[0].astype(BF16).astype(F32)
    vn = vn_ref[0].astype(BF16).astype(F32)
    s_new = jnp.sum(q.astype(F32) * kn, axis=-1, keepdims=True) * scale

    s_past = s_s[...]
    m = jnp.maximum(jnp.max(s_past, axis=-1, keepdims=True), s_new)
    p_past = jnp.exp(s_past - m)
    p_new = jnp.exp(s_new - m)
    l = jnp.sum(p_past, axis=-1, keepdims=True) + p_new
    p16 = p_past.astype(BF16)

    acc = p_new.astype(BF16).astype(F32) * vn
    for p in range(n_pages):
        vf = v_refs[p][0].reshape(pw, head_dim).astype(BF16)
        acc = acc + _dot(p16[:, p * pw:(p + 1) * pw], vf)
    o_ref[0] = (acc / l).astype(o_ref.dtype)


def _attn_sample(q, k_new, v_new, lf_new, cache_k, cache_v, cache_lf, page_table):
    batch, n_pages = page_table.shape
    _, page, n_heads, head_dim = cache_k.shape
    total = n_pages * page * n_heads

    def row_spec(shape):
        return pl.BlockSpec((1,) + shape, lambda b, pt: (b, 0, 0))

    def page_spec(p):
        return pl.BlockSpec((1, page, n_heads, head_dim), lambda b, pt: (pt[b, p], 0, 0, 0))

    def lf_spec(p):
        return pl.BlockSpec((1, 1, page * n_heads), lambda b, pt: (pt[b, p], 0, 0))

    return pl.pallas_call(
        functools.partial(_attn_sample_kernel, n_pages=n_pages, scale=head_dim ** -0.5),
        grid_spec=pltpu.PrefetchScalarGridSpec(
            num_scalar_prefetch=1,
            grid=(batch,),
            in_specs=[row_spec((n_heads, head_dim))] * 3 + [row_spec((n_heads, 1))]
                     + [page_spec(p) for p in range(n_pages)] * 2
                     + [lf_spec(p) for p in range(n_pages)],
            out_specs=row_spec((n_heads, head_dim)),
            scratch_shapes=[pltpu.VMEM((n_heads, total), F32)],
        ),
        out_shape=jax.ShapeDtypeStruct((batch, n_heads, head_dim), BF16),
        compiler_params=_params("arbitrary"),
        name="attn_sample",
    )(page_table, q, k_new, v_new, lf_new,
      *([cache_k] * n_pages), *([cache_v] * n_pages), *([cache_lf] * n_pages))


def _rglru_gates(conv, r_pre, i_pre, b_rg, b_ig, neg_c_softplus):
    r = jax.nn.sigmoid(r_pre + b_rg)
    i_g = jax.nn.sigmoid(i_pre + b_ig)
    log_a = r * neg_c_softplus
    a = jnp.exp(log_a)
    b = jnp.sqrt(-jnp.tanh(log_a) * (1.0 + a * a)) * (i_g * conv)
    return a, b


def _rglru_prompt_kernel(xr_ref, yr_ref, cw_ref, cb_ref, wrg_ref, wig_ref, brg_ref, big_ref,
                         lam_ref, o_ref, hl_ref, xc_s, conv_s, rp_s, ip_s, a_s, b_s, h_s):
    c = pl.program_id(0)
    nb, tc, n_blocks, bw = xr_ref.shape
    halo = CONV_WIDTH - 1
    rows = nb * tc

    @pl.when(c == 0)
    def _():
        xc_s[:, 0:halo] = jnp.zeros((nb, halo, n_blocks, bw), F32)
        h_s[...] = jnp.zeros_like(h_s)

    @pl.when(c > 0)
    def _():
        xc_s[:, 0:halo] = xc_s[:, tc:tc + halo]

    xc_s[:, halo:halo + tc] = xr_ref[...]

    conv = cb_ref[...] + xc_s[:, 0:tc] * cw_ref[0]
    for i in range(1, CONV_WIDTH):
        conv = conv + xc_s[:, i:i + tc] * cw_ref[i]

    conv_s[...] = conv.reshape(rows * n_blocks, bw)
    for n in range(n_blocks):
        xn = conv_s[pl.ds(n, rows, stride=n_blocks), :].astype(BF16)
        rp_s[pl.ds(n, rows, stride=n_blocks), :] = _dot(xn, wrg_ref[n])
        ip_s[pl.ds(n, rows, stride=n_blocks), :] = _dot(xn, wig_ref[n])

    neg_c_softplus = -LRU_C * _softplus(-lam_ref[...])
    a, b = _rglru_gates(conv,
                        rp_s[...].reshape(nb, tc, n_blocks, bw),
                        ip_s[...].reshape(nb, tc, n_blocks, bw),
                        brg_ref[...], big_ref[...], neg_c_softplus)
    a_s[...] = a
    b_s[...] = b

    def step(t, hs):
        new = tuple(a_s[j, t] * hs[j] + b_s[j, t] for j in range(nb))
        for j in range(nb):
            b_s[j, t] = new[j]
        return new

    hs = lax.fori_loop(0, tc, step, tuple(h_s[j] for j in range(nb)), unroll=8)
    for j in range(nb):
        h_s[j] = hs[j]

    conv_s[...] = (b_s[...] * _gelu_tanh(yr_ref[...])).reshape(rows * n_blocks, bw)
    for n in range(n_blocks):
        o_ref[:, :, n * bw:(n + 1) * bw] = (
            conv_s[pl.ds(n, rows, stride=n_blocks), :].reshape(nb, tc, bw).astype(o_ref.dtype))
    hl_ref[...] = h_s[...]


def _rglru_prompt(xr, yr, w, batch, seq, tc):
    n_blocks, bw = w["n_rnn_blocks"], w["rnn_block"]
    xr4 = xr.reshape(batch, seq, n_blocks, bw)
    yr4 = yr.reshape(batch, seq, n_blocks, bw)
    rows = batch * tc
    halo = CONV_WIDTH - 1

    def const(shape):
        return pl.BlockSpec(shape, lambda c: (0,) * len(shape))

    seq_spec = pl.BlockSpec((batch, tc, n_blocks, bw), lambda c: (0, c, 0, 0))
    out, h_last = pl.pallas_call(
        _rglru_prompt_kernel,
        grid=(seq // tc,),
        in_specs=[seq_spec, seq_spec,
                  const((CONV_WIDTH, n_blocks, bw)), const((n_blocks, bw)),
                  const((n_blocks, bw, bw)), const((n_blocks, bw, bw)),
                  const((n_blocks, bw)), const((n_blocks, bw)), const((n_blocks, bw))],
        out_specs=[pl.BlockSpec((batch, tc, n_blocks * bw), lambda c: (0, c, 0)),
                   const((batch, n_blocks, bw))],
        out_shape=[jax.ShapeDtypeStruct((batch, seq, n_blocks * bw), BF16),
                   jax.ShapeDtypeStruct((batch, n_blocks, bw), F32)],
        scratch_shapes=[pltpu.VMEM((batch, tc + halo, n_blocks, bw), F32),
                        pltpu.VMEM((rows * n_blocks, bw), F32),
                        pltpu.VMEM((rows * n_blocks, bw), F32),
                        pltpu.VMEM((rows * n_blocks, bw), F32),
                        pltpu.VMEM((batch, tc, n_blocks, bw), F32),
                        pltpu.VMEM((batch, tc, n_blocks, bw), F32),
                        pltpu.VMEM((batch, n_blocks, bw), F32)],
        compiler_params=_params("arbitrary"),
        name="rglru_prompt",
    )(xr4, yr4, w["conv_w3"], w["conv_b2"], w["w_rg"], w["w_ig"], w["b_rg2"], w["b_ig2"],
      w["lam2"])
    return out.reshape(batch * seq, n_blocks * bw), h_last.reshape(batch, n_blocks * bw)


def _rglru_sample_kernel(xr_ref, yr_ref, buf_ref, h0_ref, cw_ref, cb_ref, wrg_ref, wig_ref,
                         brg_ref, big_ref, lam_ref, o_ref, h_ref, nbuf_ref):
    halo = CONV_WIDTH - 1
    n_blocks, bw = wrg_ref.shape[0], wrg_ref.shape[1]
    xr = xr_ref[...]
    conv = cb_ref[...] + xr * cw_ref[halo:halo + 1, :]
    for i in range(halo):
        conv = conv + buf_ref[i] * cw_ref[i:i + 1, :]
    r_parts, i_parts = [], []
    for n in range(n_blocks):
        xn = conv[:, n * bw:(n + 1) * bw].astype(BF16)
        r_parts.append(_dot(xn, wrg_ref[n]))
        i_parts.append(_dot(xn, wig_ref[n]))
    neg_c_softplus = -LRU_C * _softplus(-lam_ref[...])
    a, b = _rglru_gates(conv, jnp.concatenate(r_parts, axis=1), jnp.concatenate(i_parts, axis=1),
                        brg_ref[...], big_ref[...], neg_c_softplus)
    h = a * h0_ref[...] + b
    h_ref[...] = h
    o_ref[...] = h * _gelu_tanh(yr_ref[...])
    for i in range(halo - 1):
        nbuf_ref[i] = buf_ref[i + 1]
    nbuf_ref[halo - 1] = xr


def _rglru_sample(xr, yr, conv_buf_t, h0, w):
    rows, width = xr.shape
    halo = CONV_WIDTH - 1
    return pl.pallas_call(
        _rglru_sample_kernel,
        out_shape=[jax.ShapeDtypeStruct((rows, width), F32),
                   jax.ShapeDtypeStruct((rows, width), F32),
                   jax.ShapeDtypeStruct((halo, rows, width), F32)],
        compiler_params=pltpu.CompilerParams(vmem_limit_bytes=VMEM_LIMIT_BYTES),
        name="rglru_sample",
    )(xr, yr, conv_buf_t, h0, w["conv_w"], w["conv_b"], w["w_rg"], w["w_ig"],
      w["b_rg"], w["b_ig"], w["lam"])


def _out_proj_kernel(attn_ref, rnn_ref, ga_ref, gr_ref, x_ref, woa_ref, wor_ref, wout_ref,
                     gpm_ref, gpf_ref, h_o, hn_o):
    o = (jax.nn.sigmoid(ga_ref[...]) * _dot(attn_ref[...].astype(BF16), woa_ref[...])
         + jax.nn.sigmoid(gr_ref[...]) * _dot(rnn_ref[...].astype(BF16), wor_ref[...]))
    mix = _dot(o.astype(BF16), wout_ref[...])
    h = x_ref[...] + _rms_norm(mix, gpm_ref[...])
    h_o[...] = h
    hn_o[...] = _rms_norm(h, gpf_ref[...]).astype(hn_o.dtype)


def _out_proj(attn, rnn, ga, gr, x, w, tm):
    rows, d_model = x.shape

    def row(width):
        return pl.BlockSpec((tm, width), lambda i: (i, 0))

    def const(shape):
        return pl.BlockSpec(shape, lambda i: (0, 0), pipeline_mode=pl.Buffered(1))

    return pl.pallas_call(
        _out_proj_kernel,
        grid=(rows // tm,),
        in_specs=[row(attn.shape[1]), row(rnn.shape[1]), row(d_model), row(d_model), row(d_model),
                  const(w["o_attn"].shape), const(w["o_rnn"].shape), const(w["out"].shape),
                  const((1, d_model)), const((1, d_model))],
        out_specs=[row(d_model), row(d_model)],
        out_shape=[jax.ShapeDtypeStruct((rows, d_model), F32),
                   jax.ShapeDtypeStruct((rows, d_model), BF16)],
        compiler_params=_params("parallel"),
        name="out_proj",
    )(attn, rnn, ga, gr, x, w["o_attn"], w["o_rnn"], w["out"], w["g_post_mix"], w["g_pre_ffn"])


def _ffn_kernel(hn_ref, h_ref, wg_ref, wu_ref, wd_ref, g_ref, y_o, acc_s):
    j = pl.program_id(1)

    @pl.when(j == 0)
    def _():
        acc_s[...] = jnp.zeros_like(acc_s)

    hn = hn_ref[...]
    gate = _dot(hn, wg_ref[...])
    up = _dot(hn, wu_ref[...])
    acc_s[...] += _dot((jax.nn.silu(gate) * up).astype(BF16), wd_ref[...])

    @pl.when(j == pl.num_programs(1) - 1)
    def _():
        y_o[...] = h_ref[...] + _rms_norm(acc_s[...], g_ref[...])


def _ffn(hn, h, w, tm, tf):
    rows, d_model = h.shape
    d_ff = w["gate"].shape[1]
    return pl.pallas_call(
        _ffn_kernel,
        grid=(rows // tm, d_ff // tf),
        in_specs=[pl.BlockSpec((tm, d_model), lambda i, j: (i, 0)),
                  pl.BlockSpec((tm, d_model), lambda i, j: (i, 0)),
                  pl.BlockSpec((d_model, tf), lambda i, j: (0, j)),
                  pl.BlockSpec((d_model, tf), lambda i, j: (0, j)),
                  pl.BlockSpec((tf, d_model), lambda i, j: (j, 0)),
                  pl.BlockSpec((1, d_model), lambda i, j: (0, 0))],
        out_specs=pl.BlockSpec((tm, d_model), lambda i, j: (i, 0)),
        out_shape=jax.ShapeDtypeStruct((rows, d_model), F32),
        scratch_shapes=[pltpu.VMEM((tm, d_model), F32)],
        compiler_params=_params("parallel", "arbitrary"),
        name="ffn",
    )(hn, h, w["gate"], w["up"], w["down"], w["g_post_ffn"])


def _layer_weights(l, n_heads, head_dim, g_pre_mix, w_in, b_f, conv_w, conv_b, w_rg, b_rg, w_ig,
                   b_ig, lru_lambda, w_o_attn, w_o_rnn, w_out, g_post_mix, g_pre_ffn, w_gate,
                   w_up, w_down, g_post_ffn):
    d_model = w_in.shape[1]
    n_blocks, bw = w_rg.shape[1], w_rg.shape[2]
    d_rnn = n_blocks * bw
    aw = n_heads * head_dim
    wi = w_in[l]
    f_lo, f_hi = 3 * aw, 3 * aw + n_heads
    cols = {
        "qkv": wi[:, :f_lo].astype(BF16),
        "rest": wi[:, f_hi:].astype(BF16),
        "f": jnp.pad(wi[:, f_lo:f_hi].astype(BF16), ((0, 0), (0, LANES - n_heads))),
        "b_f": jnp.pad(b_f[l].reshape(1, n_heads), ((0, 0), (0, LANES - n_heads))),
        "widths": {"q": aw, "k": aw, "v": aw, "xr": d_rnn, "yr": d_rnn,
                   "ga": d_model, "gr": d_model},
        "n_heads": n_heads,
        "head_dim": head_dim,
    }
    rnn = {
        "n_rnn_blocks": n_blocks, "rnn_block": bw,
        "conv_w": conv_w[l], "conv_b": conv_b[l].reshape(1, d_rnn),
        "b_rg": b_rg[l].reshape(1, d_rnn), "b_ig": b_ig[l].reshape(1, d_rnn),
        "lam": lru_lambda[l].reshape(1, d_rnn),
        "conv_w3": conv_w[l].reshape(CONV_WIDTH, n_blocks, bw),
        "conv_b2": conv_b[l].reshape(n_blocks, bw),
        "b_rg2": b_rg[l].reshape(n_blocks, bw), "b_ig2": b_ig[l].reshape(n_blocks, bw),
        "lam2": lru_lambda[l].reshape(n_blocks, bw),
        "w_rg": w_rg[l].astype(BF16), "w_ig": w_ig[l].astype(BF16),
    }
    mix = {"o_attn": w_o_attn[l].astype(BF16), "o_rnn": w_o_rnn[l].astype(BF16),
           "out": w_out[l].astype(BF16),
           "g_post_mix": g_post_mix[l].reshape(1, d_model),
           "g_pre_ffn": g_pre_ffn[l].reshape(1, d_model)}
    ffn = {"gate": w_gate[l].astype(BF16), "up": w_up[l].astype(BF16),
           "down": w_down[l].astype(BF16), "g_post_ffn": g_post_ffn[l].reshape(1, d_model)}
    return g_pre_mix[l].reshape(1, d_model), cols, rnn, mix, ffn


def kernel(x_prompt, x_sample, cache_k, cache_v, cache_logf, state_h, state_conv, page_table,
           g_pre_mix, w_in, b_f, conv_w, conv_b, w_rg, b_rg, w_ig, b_ig, lru_lambda,
           w_o_attn, w_o_rnn, w_out, g_post_mix, g_pre_ffn, w_gate, w_up, w_down, g_post_ffn):
    batch, seq, d_model = x_prompt.shape
    dec_batch, dec_seq, _ = x_sample.shape
    assert dec_seq == 1, "sample group carries one new token per sequence"
    depth, n_phys, page, n_heads, head_dim = cache_k.shape
    aw = n_heads * head_dim
    d_rnn = state_h.shape[2]
    n_pages = page_table.shape[1]
    assert LANES % n_heads == 0 and (page * n_heads) % LANES == 0

    xp = x_prompt.reshape(batch * seq, d_model)
    xs = x_sample.reshape(dec_batch, d_model)
    outs = {name: [] for name in ("kp", "vp", "lp", "hp", "cp", "ks", "vs", "ls", "hs", "cs")}

    for l in range(depth):
        g_in, w_cols, w_rnn, w_mix, w_ffn = _layer_weights(
            l, n_heads, head_dim, g_pre_mix, w_in, b_f, conv_w, conv_b, w_rg, b_rg, w_ig, b_ig,
            lru_lambda, w_o_attn, w_o_rnn, w_out, g_post_mix, g_pre_ffn, w_gate, w_up, w_down,
            g_post_ffn)

        q, k16, v16, k, v, xr, yr, ga, gr, lf, c = _in_proj(xp, g_in, w_cols, tm=512,
                                                            n_col_steps=4, seq=seq)
        attn = _attn_prompt(q, k16, v16, c, batch, seq, n_heads, head_dim, tq=256)
        rnn, h_last = _rglru_prompt(xr, yr, w_rnn, batch, seq, tc=128)
        h, hn = _out_proj(attn, rnn, ga, gr, xp, w_mix, tm=256)
        xp_next = _ffn(hn, h, w_ffn, tm=512, tf=512)
        outs["kp"].append(k.reshape(batch, seq, n_heads, head_dim))
        outs["vp"].append(v.reshape(batch, seq, n_heads, head_dim))
        outs["lp"].append(lf.reshape(batch, seq, n_heads))
        outs["hp"].append(h_last)
        outs["cp"].append(xr.reshape(batch, seq, d_rnn)[:, seq - (CONV_WIDTH - 1):])

        q, _, _, k, v, xr, yr, ga, gr, lf = _in_proj(xs, g_in, w_cols, tm=dec_batch, n_col_steps=4)
        k = k.reshape(dec_batch, n_heads, head_dim)
        v = v.reshape(dec_batch, n_heads, head_dim)
        attn = _attn_sample(
            q.reshape(dec_batch, n_heads, head_dim), k, v, lf.reshape(dec_batch, n_heads, 1),
            cache_k[l], cache_v[l], cache_logf[l].reshape(n_phys, 1, page * n_heads),
            page_table).reshape(dec_batch, aw)
        rnn, h_new, buf_new = _rglru_sample(xr, yr, jnp.swapaxes(state_conv[l], 0, 1),
                                            state_h[l], w_rnn)
        h, hn = _out_proj(attn, rnn, ga, gr, xs, w_mix, tm=dec_batch)
        xs_next = _ffn(hn, h, w_ffn, tm=dec_batch, tf=512)
        outs["ks"].append(k.reshape(dec_batch, 1, n_heads, head_dim))
        outs["vs"].append(v.reshape(dec_batch, 1, n_heads, head_dim))
        outs["ls"].append(lf.reshape(dec_batch, 1, n_heads))
        outs["hs"].append(h_new)
        outs["cs"].append(jnp.swapaxes(buf_new, 0, 1))

        xp, xs = xp_next, xs_next

    st = {name: jnp.stack(vals) for name, vals in outs.items()}
    return (xp.reshape(batch, seq, d_model), xs.reshape(dec_batch, 1, d_model),
            st["kp"], st["vp"], st["lp"], st["hp"], st["cp"],
            st["ks"], st["vs"], st["ls"], st["hs"], st["cs"])
```

```python
import functools
import math

import jax
import jax.numpy as jnp
from jax import lax
from jax.experimental import pallas as pl
from jax.experimental.pallas import tpu as pltpu

F32 = jnp.float32
BF16 = jnp.bfloat16

EPS = 1e-6
LRU_C = 8.0
CONV_WIDTH = 4
NEG_BIG = -1e30

SUBLANES = 8
LANES = 128
VMEM_LIMIT_BYTES = 56 * 1024 * 1024


def _params(*semantics):
    return pltpu.CompilerParams(dimension_semantics=semantics,
                                vmem_limit_bytes=VMEM_LIMIT_BYTES)


def _rms_norm(x, g):
    inv = lax.rsqrt(jnp.mean(x * x, axis=-1, keepdims=True) + EPS)
    return (x * inv) * g


def _log_sigmoid(x):
    return jnp.minimum(x, 0.0) - jnp.log1p(jnp.exp(-jnp.abs(x)))


def _softplus(x):
    return jnp.maximum(x, 0.0) + jnp.log1p(jnp.exp(-jnp.abs(x)))


def _gelu_tanh(x):
    c = math.sqrt(2.0 / math.pi)
    return x * (0.5 * (1.0 + jnp.tanh(c * (x + 0.044715 * (x * x * x)))))


def _dot(a, b):
    return jnp.dot(a, b, preferred_element_type=F32)


def _dot_nt(a, b):
    return lax.dot_general(a, b, (((1,), (1,)), ((), ())), preferred_element_type=F32)


def _prefix_sum(x, axis, start=1):
    idx = lax.broadcasted_iota(jnp.int32, x.shape, axis)
    shift = start
    while shift < x.shape[axis]:
        x = x + jnp.where(idx >= shift, pltpu.roll(x, shift, axis=axis), 0.0)
        shift *= 2
    return x


def _in_proj_kernel(x_ref, g_ref, wq, wk, wv, wxr, wyr, wga, wgr, wf, bf_ref,
                    q_o, k16_o, v16_o, k32_o, v32_o, xr_o, yr_o, ga_o, gr_o, lf_o, *rest,
                    tiles_per_seq, inv_scale):
    i, j = pl.program_id(0), pl.program_id(1)
    xn_s = rest[-1]
    n_heads = lf_o.shape[1]

    @pl.when(j == 0)
    def _():
        xn = _rms_norm(x_ref[...], g_ref[...]).astype(BF16)
        xn_s[...] = xn
        lf = _log_sigmoid(_dot(xn, wf[...]) + bf_ref[...])
        lf_o[...] = lf[:, :n_heads]
        if tiles_per_seq is not None:
            c_o, carry_s = rest[0], rest[1]

            @pl.when(lax.rem(i, tiles_per_seq) == 0)
            def _():
                carry_s[...] = jnp.zeros_like(carry_s)

            c = _prefix_sum(lf, axis=0) + carry_s[...]
            carry_s[...] = c[c.shape[0] - 1:, :]
            hi, mid, lo = _split3_bf16(c * inv_scale)
            lane = lax.broadcasted_iota(jnp.int32, c.shape, 1)
            packed = jnp.where(
                lane < n_heads, hi,
                jnp.where(lane < 2 * n_heads, pltpu.roll(mid, n_heads, axis=1),
                          jnp.where(lane < 3 * n_heads, pltpu.roll(lo, 2 * n_heads, axis=1),
                                    jnp.where(lane == 3 * n_heads, 1.0, 0.0))))
            c_o[...] = packed.astype(c_o.dtype)

    xn = xn_s[...]
    for w, o in ((wq, q_o), (wxr, xr_o), (wyr, yr_o), (wga, ga_o), (wgr, gr_o)):
        o[...] = _dot(xn, w[...]).astype(o.dtype)

    tm = xn.shape[0]
    n_heads = k32_o.shape[0] // tm
    head_dim = k32_o.shape[1]
    heads_per_step = wk.shape[1] // head_dim
    for w, o16, o32 in ((wk, k16_o, k32_o), (wv, v16_o, v32_o)):
        res = _dot(xn, w[...])
        o16[...] = res.astype(o16.dtype)
        for hh in range(heads_per_step):
            head = j * heads_per_step + hh
            o32[pl.ds(head, tm, stride=n_heads), :] = res[:, hh * head_dim:(hh + 1) * head_dim]


def _in_proj(x, g, w, tm, n_col_steps, seq=None):
    rows, d_model = x.shape
    n_heads, head_dim = w["n_heads"], w["head_dim"]
    cols = {}
    for arr, names in (("qkv", ("q", "k", "v")), ("rest", ("xr", "yr", "ga", "gr"))):
        start = 0
        for name in names:
            cols[name] = (arr, start, w["widths"][name])
            start += w["widths"][name]
    order = ("q", "k", "v", "xr", "yr", "ga", "gr")
    tiles = {name: cols[name][2] // n_col_steps for name in order}
    grid = (rows // tm, n_col_steps)

    def w_spec(name):
        first_block = cols[name][1] // tiles[name]
        return pl.BlockSpec((d_model, tiles[name]), lambda i, j: (0, first_block + j))

    def o_spec(name):
        return pl.BlockSpec((tm, tiles[name]), lambda i, j: (i, j))

    def o_shape(name, dtype):
        return jax.ShapeDtypeStruct((rows, cols[name][2]), dtype)

    def per_head():
        return pl.BlockSpec((tm, n_heads), lambda i, j: (i, 0))

    flat_spec = pl.BlockSpec((tm * n_heads, head_dim), lambda i, j: (i, 0))
    flat_shape = jax.ShapeDtypeStruct((rows * n_heads, head_dim), F32)
    out_specs = [o_spec("q"), o_spec("k"), o_spec("v"), flat_spec, flat_spec,
                 o_spec("xr"), o_spec("yr"), o_spec("ga"), o_spec("gr"), per_head()]
    out_shape = [o_shape("q", BF16), o_shape("k", BF16), o_shape("v", BF16),
                 flat_shape, flat_shape,
                 o_shape("xr", F32), o_shape("yr", F32), o_shape("ga", BF16), o_shape("gr", BF16),
                 jax.ShapeDtypeStruct((rows, n_heads), F32)]
    scratch = [pltpu.VMEM((tm, d_model), BF16)]
    if seq is not None:
        out_specs.append(pl.BlockSpec((tm, LANES), lambda i, j: (i, 0)))
        out_shape.append(jax.ShapeDtypeStruct((rows, LANES), BF16))
        scratch.insert(0, pltpu.VMEM((1, LANES), F32))
    return pl.pallas_call(
        functools.partial(_in_proj_kernel,
                          tiles_per_seq=None if seq is None else seq // tm,
                          inv_scale=head_dim ** 0.5),
        grid=grid,
        in_specs=[pl.BlockSpec((tm, d_model), lambda i, j: (i, 0)),
                  pl.BlockSpec((1, d_model), lambda i, j: (0, 0))]
                 + [w_spec(name) for name in order]
                 + [pl.BlockSpec((d_model, LANES), lambda i, j: (0, 0)),
                    pl.BlockSpec((1, LANES), lambda i, j: (0, 0))],
        out_specs=out_specs,
        out_shape=out_shape,
        scratch_shapes=scratch,
        compiler_params=_params("arbitrary", "arbitrary"),
        name="in_proj",
    )(x, g, *[w[cols[name][0]] for name in order], w["f"], w["b_f"])


def _split3_bf16(x):
    hi = x.astype(BF16)
    r = x - hi.astype(F32)
    mid = r.astype(BF16)
    lo = (r - mid.astype(F32)).astype(BF16)
    return hi.astype(F32), mid.astype(F32), lo.astype(F32)


def _attn_prompt_kernel(q_ref, k_ref, v_ref, c_ref, o_ref, qa_s, ka_s, s_s, *,
                        tq, scale, n_heads):
    seq, head_dim = q_ref.shape
    h = pl.program_id(1)
    src = lax.broadcasted_iota(jnp.int32, (LANES, LANES), 0)
    dst = lax.broadcasted_iota(jnp.int32, (LANES, LANES), 1)
    one_src = src == 3 * n_heads
    part_of_dst = jnp.where(dst < 3, dst, dst - 3) * n_heads + h
    sel_q = jnp.where((dst < 3) & (src == part_of_dst), 1.0,
                      jnp.where((dst >= 3) & (dst < 6) & one_src, 1.0, 0.0))
    sel_k = jnp.where((dst >= 3) & (dst < 6) & (src == part_of_dst), -1.0,
                      jnp.where((dst < 3) & one_src, 1.0, 0.0))
    c_rows = c_ref[...]
    qa_s[:, :head_dim] = q_ref[...]
    qa_s[:, head_dim:] = _dot(c_rows, sel_q.astype(BF16)).astype(BF16)
    ka_s[:, :head_dim] = k_ref[...]
    ka_s[:, head_dim:] = _dot(c_rows, sel_k.astype(BF16)).astype(BF16)

    sigma = scale * math.log2(math.e)
    row = lax.broadcasted_iota(jnp.int32, (tq, tq), 0)
    col = lax.broadcasted_iota(jnp.int32, (tq, tq), 1)
    visible = col <= row
    groups = tq // LANES
    for i in range(seq // tq):
        qa = qa_s[i * tq:(i + 1) * tq, :]
        m_part = jnp.full((tq, LANES), NEG_BIG, F32)
        for j in range(i + 1):
            s = _dot_nt(qa, ka_s[j * tq:(j + 1) * tq, :]) * sigma
            if j == i:
                s = jnp.where(visible, s, NEG_BIG)
            s_s[:, j * tq:(j + 1) * tq] = s
            for g in range(groups):
                m_part = jnp.maximum(m_part, s[:, g * LANES:(g + 1) * LANES])
        m = jnp.broadcast_to(jnp.max(m_part, axis=-1, keepdims=True), (tq, LANES))
        l_part = jnp.zeros((tq, LANES), F32)
        acc = jnp.zeros((tq, head_dim), F32)
        for j in range(i + 1):
            parts = []
            for g in range(groups):
                off = j * tq + g * LANES
                p = jnp.exp2(s_s[:, off:off + LANES] - m)
                l_part = l_part + p
                parts.append(p.astype(BF16))
            acc = acc + _dot(jnp.concatenate(parts, axis=1), v_ref[j * tq:(j + 1) * tq, :])
        l = jnp.sum(l_part, axis=-1, keepdims=True)
        o_ref[i * tq:(i + 1) * tq, :] = (acc / l).astype(o_ref.dtype)


def _attn_prompt(q, k, v, c, batch, seq, n_heads, head_dim, tq):
    head_spec = pl.BlockSpec((seq, head_dim), lambda b, h: (b, h))
    return pl.pallas_call(
        functools.partial(_attn_prompt_kernel, tq=tq, scale=head_dim ** -0.5, n_heads=n_heads),
        grid=(batch, n_heads),
        in_specs=[head_spec, head_spec, head_spec,
                  pl.BlockSpec((seq, LANES), lambda b, h: (b, 0))],
        out_specs=head_spec,
        out_shape=jax.ShapeDtypeStruct(q.shape, BF16),
        scratch_shapes=[pltpu.VMEM((seq, head_dim + LANES), BF16),
                        pltpu.VMEM((seq, head_dim + LANES), BF16),
                        pltpu.VMEM((tq, seq), F32)],
        compiler_params=_params("parallel", "parallel"),
        name="attn_prompt",
    )(q, k, v, c)


def _paged_logf_sums(lf_pages, n_heads):
    n_pages, pw = lf_pages.shape
    within = _prefix_sum(lf_pages, axis=1, start=n_heads)
    lane = lax.broadcasted_iota(jnp.int32, (n_pages, LANES), 1)
    totals = jnp.where(lane >= LANES - n_heads, within[:, pw - LANES:], 0.0)
    shift = n_heads
    while shift < LANES:
        totals = totals + pltpu.roll(totals, LANES - shift, axis=1)
        shift *= 2
    upto = _prefix_sum(totals, axis=0)
    before = upto - totals
    sums = within + jnp.concatenate([before] * (pw // LANES), axis=1)
    return sums, upto[n_pages - 1:, :]


def _attn_sample_kernel(pt_ref, q_ref, kn_ref, vn_ref, lfn_ref, *refs, n_pages, scale):
    del pt_ref
    k_refs = refs[:n_pages]
    v_refs = refs[n_pages:2 * n_pages]
    lf_refs = refs[2 * n_pages:3 * n_pages]
    o_ref = refs[3 * n_pages]
    s_s = refs[3 * n_pages + 1]
    page, n_heads, head_dim = k_refs[0].shape[1:]
    pw = page * n_heads

    head_of_row = lax.broadcasted_iota(jnp.int32, (n_heads, pw), 0)
    lane = lax.broadcasted_iota(jnp.int32, (n_heads, pw), 1)
    on_diag = head_of_row == lane % n_heads

    q = q_ref[0]
    c_past, c_total = _paged_logf_sums(
        jnp.concatenate([lf_refs[p][0] for p in range(n_pages)], axis=0), n_heads)
    own_lane = lane[:, :LANES] == head_of_row[:, :LANES]
    c_last = jnp.sum(jnp.where(own_lane, c_total, 0.0), axis=-1, keepdims=True)
    c_new = c_last + lfn_ref[0]

    for p in range(n_pages):
        sl = slice(p * pw, (p + 1) * pw)
        kf = k_refs[p][0].reshape(pw, head_dim).astype(BF16)
        s = _dot_nt(q, kf) * scale + (c_new - c_past[p:p + 1, :])
        s_s[:, sl] = jnp.where(on_diag, s, NEG_BIG)

    kn = kn_ref[0].astype(BF16).astype(F32)
    vn = vn_ref[0].astype(BF16).astype(F32)
    s_new = jnp.sum(q.astype(F32) * kn, axis=-1, keepdims=True) * scale

    s_past = s_s[...]
    m = jnp.maximum(jnp.max(s_past, axis=-1, keepdims=True), s_new)
    p_past = jnp.exp(s_past - m)
    p_new = jnp.exp(s_new - m)
    l = jnp.sum(p_past, axis=-1, keepdims=True) + p_new
    p16 = p_past.astype(BF16)

    acc = p_new.astype(BF16).astype(F32) * vn
    for p in range(n_pages):
        vf = v_refs[p][0].reshape(pw, head_dim).astype(BF16)
        acc = acc + _dot(p16[:, p * pw:(p + 1) * pw], vf)
    o_ref[0] = (acc / l).astype(o_ref.dtype)


def _attn_sample(q, k_new, v_new, lf_new, cache_k, cache_v, cache_lf, page_table):
    batch, n_pages = page_table.shape
    _, page, n_heads, head_dim = cache_k.shape
    total = n_pages * page * n_heads

    def row_spec(shape):
        return pl.BlockSpec((1,) + shape, lambda b, pt: (b, 0, 0))

    def page_spec(p):
        return pl.BlockSpec((1, page, n_heads, head_dim), lambda b, pt: (pt[b, p], 0, 0, 0))

    def lf_spec(p):
        return pl.BlockSpec((1, 1, page * n_heads), lambda b, pt: (pt[b, p], 0, 0))

    return pl.pallas_call(
        functools.partial(_attn_sample_kernel, n_pages=n_pages, scale=head_dim ** -0.5),
        grid_spec=pltpu.PrefetchScalarGridSpec(
            num_scalar_prefetch=1,
            grid=(batch,),
            in_specs=[row_spec((n_heads, head_dim))] * 3 + [row_spec((n_heads, 1))]
                     + [page_spec(p) for p in range(n_pages)] * 2
                     + [lf_spec(p) for p in range(n_pages)],
            out_specs=row_spec((n_heads, head_dim)),
            scratch_shapes=[pltpu.VMEM((n_heads, total), F32)],
        ),
        out_shape=jax.ShapeDtypeStruct((batch, n_heads, head_dim), BF16),
        compiler_params=_params("arbitrary"),
        name="attn_sample",
    )(page_table, q, k_new, v_new, lf_new,
      *([cache_k] * n_pages), *([cache_v] * n_pages), *([cache_lf] * n_pages))


def _rglru_gates(conv, r_pre, i_pre, b_rg, b_ig, neg_c_softplus):
    r = jax.nn.sigmoid(r_pre + b_rg)
    i_g = jax.nn.sigmoid(i_pre + b_ig)
    log_a = r * neg_c_softplus
    a = jnp.exp(log_a)
    b = jnp.sqrt(-jnp.tanh(log_a) * (1.0 + a * a)) * (i_g * conv)
    return a, b


def _rglru_prompt_kernel(xr_ref, yr_ref, cw_ref, cb_ref, wrg_ref, wig_ref, brg_ref, big_ref,
                         lam_ref, o_ref, hl_ref, xc_s, conv_s, rp_s, ip_s, a_s, b_s, h_s):
    c = pl.program_id(0)
    nb, tc, n_blocks, bw = xr_ref.shape
    halo = CONV_WIDTH - 1
    rows = nb * tc

    @pl.when(c == 0)
    def _():
        xc_s[:, 0:halo] = jnp.zeros((nb, halo, n_blocks, bw), F32)
        h_s[...] = jnp.zeros_like(h_s)

    @pl.when(c > 0)
    def _():
        xc_s[:, 0:halo] = xc_s[:, tc:tc + halo]

    xc_s[:, halo:halo + tc] = xr_ref[...]

    conv = cb_ref[...] + xc_s[:, 0:tc] * cw_ref[0]
    for i in range(1, CONV_WIDTH):
        conv = conv + xc_s[:, i:i + tc] * cw_ref[i]

    conv_s[...] = conv.reshape(rows * n_blocks, bw)
    for n in range(n_blocks):
        xn = conv_s[pl.ds(n, rows, stride=n_blocks), :].astype(BF16)
        rp_s[pl.ds(n, rows, stride=n_blocks), :] = _dot(xn, wrg_ref[n])
        ip_s[pl.ds(n, rows, stride=n_blocks), :] = _dot(xn, wig_ref[n])

    neg_c_softplus = -LRU_C * _softplus(-lam_ref[...])
    a, b = _rglru_gates(conv,
                        rp_s[...].reshape(nb, tc, n_blocks, bw),
                        ip_s[...].reshape(nb, tc, n_blocks, bw),
                        brg_ref[...], big_ref[...], neg_c_softplus)
    a_s[...] = a
    b_s[...] = b

    def step(t, hs):
        new = tuple(a_s[j, t] * hs[j] + b_s[j, t] for j in range(nb))
        for j in range(nb):
            b_s[j, t] = new[j]
        return new

    hs = lax.fori_loop(0, tc, step, tuple(h_s[j] for j in range(nb)), unroll=8)
    for j in range(nb):
        h_s[j] = hs[j]

    conv_s[...] = (b_s[...] * _gelu_tanh(yr_ref[...])).reshape(rows * n_blocks, bw)
    for n in range(n_blocks):
        o_ref[:, :, n * bw:(n + 1) * bw] = (
            conv_s[pl.ds(n, rows, stride=n_blocks), :].reshape(nb, tc, bw).astype(o_ref.dtype))
    hl_ref[...] = h_s[...]


def _rglru_prompt(xr, yr, w, batch, seq, tc):
    n_blocks, bw = w["n_rnn_blocks"], w["rnn_block"]
    xr4 = xr.reshape(batch, seq, n_blocks, bw)
    yr4 = yr.reshape(batch, seq, n_blocks, bw)
    rows = batch * tc
    halo = CONV_WIDTH - 1

    def const(shape):
        return pl.BlockSpec(shape, lambda c: (0,) * len(shape))

    seq_spec = pl.BlockSpec((batch, tc, n_blocks, bw), lambda c: (0, c, 0, 0))
    out, h_last = pl.pallas_call(
        _rglru_prompt_kernel,
        grid=(seq // tc,),
        in_specs=[seq_spec, seq_spec,
                  const((CONV_WIDTH, n_blocks, bw)), const((n_blocks, bw)),
                  const((n_blocks, bw, bw)), const((n_blocks, bw, bw)),
                  const((n_blocks, bw)), const((n_blocks, bw)), const((n_blocks, bw))],
        out_specs=[pl.BlockSpec((batch, tc, n_blocks * bw), lambda c: (0, c, 0)),
                   const((batch, n_blocks, bw))],
        out_shape=[jax.ShapeDtypeStruct((batch, seq, n_blocks * bw), BF16),
                   jax.ShapeDtypeStruct((batch, n_blocks, bw), F32)],
        scratch_shapes=[pltpu.VMEM((batch, tc + halo, n_blocks, bw), F32),
                        pltpu.VMEM((rows * n_blocks, bw), F32),
                        pltpu.VMEM((rows * n_blocks, bw), F32),
                        pltpu.VMEM((rows * n_blocks, bw), F32),
                        pltpu.VMEM((batch, tc, n_blocks, bw), F32),
                        pltpu.VMEM((batch, tc, n_blocks, bw), F32),
                        pltpu.VMEM((batch, n_blocks, bw), F32)],
        compiler_params=_params("arbitrary"),
        name="rglru_prompt",
    )(xr4, yr4, w["conv_w3"], w["conv_b2"], w["w_rg"], w["w_ig"], w["b_rg2"], w["b_ig2"],
      w["lam2"])
    return out.reshape(batch * seq, n_blocks * bw), h_last.reshape(batch, n_blocks * bw)


def _rglru_sample_kernel(xr_ref, yr_ref, buf_ref, h0_ref, cw_ref, cb_ref, wrg_ref, wig_ref,
                         brg_ref, big_ref, lam_ref, o_ref, h_ref, nbuf_ref):
    halo = CONV_WIDTH - 1
    n_blocks, bw = wrg_ref.shape[0], wrg_ref.shape[1]
    xr = xr_ref[...]
    conv = cb_ref[...] + xr * cw_ref[halo:halo + 1, :]
    for i in range(halo):
        conv = conv + buf_ref[i] * cw_ref[i:i + 1, :]
    r_parts, i_parts = [], []
    for n in range(n_blocks):
        xn = conv[:, n * bw:(n + 1) * bw].astype(BF16)
        r_parts.append(_dot(xn, wrg_ref[n]))
        i_parts.append(_dot(xn, wig_ref[n]))
    neg_c_softplus = -LRU_C * _softplus(-lam_ref[...])
    a, b = _rglru_gates(conv, jnp.concatenate(r_parts, axis=1), jnp.concatenate(i_parts, axis=1),
                        brg_ref[...], big_ref[...], neg_c_softplus)
    h = a * h0_ref[...] + b
    h_ref[...] = h
    o_ref[...] = h * _gelu_tanh(yr_ref[...])
    for i in range(halo - 1):
        nbuf_ref[i] = buf_ref[i + 1]
    nbuf_ref[halo - 1] = xr


def _rglru_sample(xr, yr, conv_buf_t, h0, w):
    rows, width = xr.shape
    halo = CONV_WIDTH - 1
    return pl.pallas_call(
        _rglru_sample_kernel,
        out_shape=[jax.ShapeDtypeStruct((rows, width), F32),
                   jax.ShapeDtypeStruct((rows, width), F32),
                   jax.ShapeDtypeStruct((halo, rows, width), F32)],
        compiler_params=pltpu.CompilerParams(vmem_limit_bytes=VMEM_LIMIT_BYTES),
        name="rglru_sample",
    )(xr, yr, conv_buf_t, h0, w["conv_w"], w["conv_b"], w["w_rg"], w["w_ig"],
      w["b_rg"], w["b_ig"], w["lam"])


def _out_proj_kernel(attn_ref, rnn_ref, ga_ref, gr_ref, x_ref, woa_ref, wor_ref, wout_ref,
                     gpm_ref, gpf_ref, h_o, hn_o):
    o = (jax.nn.sigmoid(ga_ref[...].astype(F32)) * _dot(attn_ref[...].astype(BF16), woa_ref[...])
         + jax.nn.sigmoid(gr_ref[...].astype(F32)) * _dot(rnn_ref[...].astype(BF16), wor_ref[...]))
    mix = _dot(o.astype(BF16), wout_ref[...])
    h = x_ref[...] + _rms_norm(mix, gpm_ref[...])
    h_o[...] = h
    hn_o[...] = _rms_norm(h, gpf_ref[...]).astype(hn_o.dtype)


def _out_proj(attn, rnn, ga, gr, x, w, tm):
    rows, d_model = x.shape

    def row(width):
        return pl.BlockSpec((tm, width), lambda i: (i, 0))

    def const(shape):
        return pl.BlockSpec(shape, lambda i: (0, 0), pipeline_mode=pl.Buffered(1))

    return pl.pallas_call(
        _out_proj_kernel,
        grid=(rows // tm,),
        in_specs=[row(attn.shape[1]), row(rnn.shape[1]), row(d_model), row(d_model), row(d_model),
                  const(w["o_attn"].shape), const(w["o_rnn"].shape), const(w["out"].shape),
                  const((1, d_model)), const((1, d_model))],
        out_specs=[row(d_model), row(d_model)],
        out_shape=[jax.ShapeDtypeStruct((rows, d_model), F32),
                   jax.ShapeDtypeStruct((rows, d_model), BF16)],
        compiler_params=_params("parallel"),
        name="out_proj",
    )(attn, rnn, ga, gr, x, w["o_attn"], w["o_rnn"], w["out"], w["g_post_mix"], w["g_pre_ffn"])


def _ffn_kernel(hn_ref, h_ref, wg_ref, wu_ref, wd_ref, g_ref, y_o, acc_s):
    j = pl.program_id(1)

    @pl.when(j == 0)
    def _():
        acc_s[...] = jnp.zeros_like(acc_s)

    hn = hn_ref[...]
    gate = _dot(hn, wg_ref[...])
    up = _dot(hn, wu_ref[...])
    acc_s[...] += _dot((jax.nn.silu(gate) * up).astype(BF16), wd_ref[...])

    @pl.when(j == pl.num_programs(1) - 1)
    def _():
        y_o[...] = h_ref[...] + _rms_norm(acc_s[...], g_ref[...])


def _ffn(hn, h, w, tm, tf):
    rows, d_model = h.shape
    d_ff = w["gate"].shape[1]
    return pl.pallas_call(
        _ffn_kernel,
        grid=(rows // tm, d_ff // tf),
        in_specs=[pl.BlockSpec((tm, d_model), lambda i, j: (i, 0)),
                  pl.BlockSpec((tm, d_model), lambda i, j: (i, 0)),
                  pl.BlockSpec((d_model, tf), lambda i, j: (0, j)),
                  pl.BlockSpec((d_model, tf), lambda i, j: (0, j)),
                  pl.BlockSpec((tf, d_model), lambda i, j: (j, 0)),
                  pl.BlockSpec((1, d_model), lambda i, j: (0, 0))],
        out_specs=pl.BlockSpec((tm, d_model), lambda i, j: (i, 0)),
        out_shape=jax.ShapeDtypeStruct((rows, d_model), F32),
        scratch_shapes=[pltpu.VMEM((tm, d_model), F32)],
        compiler_params=_params("parallel", "arbitrary"),
        name="ffn",
    )(hn, h, w["gate"], w["up"], w["down"], w["g_post_ffn"])


def _layer_weights(l, n_heads, head_dim, g_pre_mix, w_in, b_f, conv_w, conv_b, w_rg, b_rg, w_ig,
                   b_ig, lru_lambda, w_o_attn, w_o_rnn, w_out, g_post_mix, g_pre_ffn, w_gate,
                   w_up, w_down, g_post_ffn):
    d_model = w_in.shape[1]
    n_blocks, bw = w_rg.shape[1], w_rg.shape[2]
    d_rnn = n_blocks * bw
    aw = n_heads * head_dim
    wi = w_in[l]
    f_lo, f_hi = 3 * aw, 3 * aw + n_heads
    cols = {
        "qkv": wi[:, :f_lo].astype(BF16),
        "rest": wi[:, f_hi:].astype(BF16),
        "f": jnp.pad(wi[:, f_lo:f_hi].astype(BF16), ((0, 0), (0, LANES - n_heads))),
        "b_f": jnp.pad(b_f[l].reshape(1, n_heads), ((0, 0), (0, LANES - n_heads))),
        "widths": {"q": aw, "k": aw, "v": aw, "xr": d_rnn, "yr": d_rnn,
                   "ga": d_model, "gr": d_model},
        "n_heads": n_heads,
        "head_dim": head_dim,
    }
    rnn = {
        "n_rnn_blocks": n_blocks, "rnn_block": bw,
        "conv_w": conv_w[l], "conv_b": conv_b[l].reshape(1, d_rnn),
        "b_rg": b_rg[l].reshape(1, d_rnn), "b_ig": b_ig[l].reshape(1, d_rnn),
        "lam": lru_lambda[l].reshape(1, d_rnn),
        "conv_w3": conv_w[l].reshape(CONV_WIDTH, n_blocks, bw),
        "conv_b2": conv_b[l].reshape(n_blocks, bw),
        "b_rg2": b_rg[l].reshape(n_blocks, bw), "b_ig2": b_ig[l].reshape(n_blocks, bw),
        "lam2": lru_lambda[l].reshape(n_blocks, bw),
        "w_rg": w_rg[l].astype(BF16), "w_ig": w_ig[l].astype(BF16),
    }
    mix = {"o_attn": w_o_attn[l].astype(BF16), "o_rnn": w_o_rnn[l].astype(BF16),
           "out": w_out[l].astype(BF16),
           "g_post_mix": g_post_mix[l].reshape(1, d_model),
           "g_pre_ffn": g_pre_ffn[l].reshape(1, d_model)}
    ffn = {"gate": w_gate[l].astype(BF16), "up": w_up[l].astype(BF16),
           "down": w_down[l].astype(BF16), "g_post_ffn": g_post_ffn[l].reshape(1, d_model)}
    return g_pre_mix[l].reshape(1, d_model), cols, rnn, mix, ffn


def kernel(x_prompt, x_sample, cache_k, cache_v, cache_logf, state_h, state_conv, page_table,
           g_pre_mix, w_in, b_f, conv_w, conv_b, w_rg, b_rg, w_ig, b_ig, lru_lambda,
           w_o_attn, w_o_rnn, w_out, g_post_mix, g_pre_ffn, w_gate, w_up, w_down, g_post_ffn):
    batch, seq, d_model = x_prompt.shape
    dec_batch, dec_seq, _ = x_sample.shape
    assert dec_seq == 1, "sample group carries one new token per sequence"
    depth, n_phys, page, n_heads, head_dim = cache_k.shape
    aw = n_heads * head_dim
    d_rnn = state_h.shape[2]
    n_pages = page_table.shape[1]
    assert LANES % n_heads == 0 and (page * n_heads) % LANES == 0

    xp = x_prompt.reshape(batch * seq, d_model)
    xs = x_sample.reshape(dec_batch, d_model)
    outs = {name: [] for name in ("kp", "vp", "lp", "hp", "cp", "ks", "vs", "ls", "hs", "cs")}

    for l in range(depth):
        g_in, w_cols, w_rnn, w_mix, w_ffn = _layer_weights(
            l, n_heads, head_dim, g_pre_mix, w_in, b_f, conv_w, conv_b, w_rg, b_rg, w_ig, b_ig,
            lru_lambda, w_o_attn, w_o_rnn, w_out, g_post_mix, g_pre_ffn, w_gate, w_up, w_down,
            g_post_ffn)

        q, k16, v16, k, v, xr, yr, ga, gr, lf, c = _in_proj(xp, g_in, w_cols, tm=512,
                                                            n_col_steps=4, seq=seq)
        attn = _attn_prompt(q, k16, v16, c, batch, seq, n_heads, head_dim, tq=256)
        rnn, h_last = _rglru_prompt(xr, yr, w_rnn, batch, seq, tc=128)
        h, hn = _out_proj(attn, rnn, ga, gr, xp, w_mix, tm=256)
        xp_next = _ffn(hn, h, w_ffn, tm=512, tf=512)
        outs["kp"].append(k.reshape(batch, seq, n_heads, head_dim))
        outs["vp"].append(v.reshape(batch, seq, n_heads, head_dim))
        outs["lp"].append(lf.reshape(batch, seq, n_heads))
        outs["hp"].append(h_last)
        outs["cp"].append(xr.reshape(batch, seq, d_rnn)[:, seq - (CONV_WIDTH - 1):])

        q, _, _, k, v, xr, yr, ga, gr, lf = _in_proj(xs, g_in, w_cols, tm=dec_batch, n_col_steps=4)
        k = k.reshape(dec_batch, n_heads, head_dim)
        v = v.reshape(dec_batch, n_heads, head_dim)
        attn = _attn_sample(
            q.reshape(dec_batch, n_heads, head_dim), k, v, lf.reshape(dec_batch, n_heads, 1),
            cache_k[l], cache_v[l], cache_logf[l].reshape(n_phys, 1, page * n_heads),
            page_table).reshape(dec_batch, aw)
        rnn, h_new, buf_new = _rglru_sample(xr, yr, jnp.swapaxes(state_conv[l], 0, 1),
                                            state_h[l], w_rnn)
        h, hn = _out_proj(attn, rnn, ga, gr, xs, w_mix, tm=dec_batch)
        xs_next = _ffn(hn, h, w_ffn, tm=dec_batch, tf=512)
        outs["ks"].append(k.reshape(dec_batch, 1, n_heads, head_dim))
        outs["vs"].append(v.reshape(dec_batch, 1, n_heads, head_dim))
        outs["ls"].append(lf.reshape(dec_batch, 1, n_heads))
        outs["hs"].append(h_new)
        outs["cs"].append(jnp.swapaxes(buf_new, 0, 1))

        xp, xs = xp_next, xs_next

    st = {name: jnp.stack(vals) for name, vals in outs.items()}
    return (xp.reshape(batch, seq, d_model), xs.reshape(dec_batch, 1, d_model),
            st["kp"], st["vp"], st["lp"], st["hp"], st["cp"],
            st["ks"], st["vs"], st["ls"], st["hs"], st["cs"])
```

```python
import functools
import math

import jax
import jax.numpy as jnp
from jax import lax
from jax.experimental import pallas as pl
from jax.experimental.pallas import tpu as pltpu

F32 = jnp.float32
BF16 = jnp.bfloat16

EPS = 1e-6
LRU_C = 8.0
CONV_WIDTH = 4
NEG_BIG = -1e30

SUBLANES = 8
LANES = 128
VMEM_LIMIT_BYTES = 56 * 1024 * 1024


def _params(*semantics):
    return pltpu.CompilerParams(dimension_semantics=semantics,
                                vmem_limit_bytes=VMEM_LIMIT_BYTES)


def _rms_norm(x, g):
    inv = lax.rsqrt(jnp.mean(x * x, axis=-1, keepdims=True) + EPS)
    return (x * inv) * g


def _log_sigmoid(x):
    return jnp.minimum(x, 0.0) - jnp.log1p(jnp.exp(-jnp.abs(x)))


def _softplus(x):
    return jnp.maximum(x, 0.0) + jnp.log1p(jnp.exp(-jnp.abs(x)))


def _gelu_tanh(x):
    c = math.sqrt(2.0 / math.pi)
    return x * (0.5 * (1.0 + jnp.tanh(c * (x + 0.044715 * (x * x * x)))))


def _dot(a, b):
    return jnp.dot(a, b, preferred_element_type=F32)


def _dot_nt(a, b):
    return lax.dot_general(a, b, (((1,), (1,)), ((), ())), preferred_element_type=F32)


def _prefix_sum(x, axis, start=1):
    idx = lax.broadcasted_iota(jnp.int32, x.shape, axis)
    shift = start
    while shift < x.shape[axis]:
        x = x + jnp.where(idx >= shift, pltpu.roll(x, shift, axis=axis), 0.0)
        shift *= 2
    return x


def _in_proj_kernel(x_ref, g_ref, wq, wk, wv, wxr, wyr, wga, wgr, wf, bf_ref,
                    q_o, k16_o, v16_o, k32_o, v32_o, xr_o, yr_o, ga_o, gr_o, lf_o, *rest,
                    tiles_per_seq, inv_scale):
    i, j = pl.program_id(0), pl.program_id(1)
    xn_s = rest[-1]
    n_heads = lf_o.shape[1]

    @pl.when(j == 0)
    def _():
        xn = _rms_norm(x_ref[...], g_ref[...]).astype(BF16)
        xn_s[...] = xn
        lf = _log_sigmoid(_dot(xn, wf[...]) + bf_ref[...])
        lf_o[...] = lf[:, :n_heads]
        if tiles_per_seq is not None:
            c_o, carry_s = rest[0], rest[1]

            @pl.when(lax.rem(i, tiles_per_seq) == 0)
            def _():
                carry_s[...] = jnp.zeros_like(carry_s)

            c = _prefix_sum(lf, axis=0) + carry_s[...]
            carry_s[...] = c[c.shape[0] - 1:, :]
            hi, mid, lo = _split3_bf16(c * inv_scale)
            lane = lax.broadcasted_iota(jnp.int32, c.shape, 1)
            packed = jnp.where(
                lane < n_heads, hi,
                jnp.where(lane < 2 * n_heads, pltpu.roll(mid, n_heads, axis=1),
                          jnp.where(lane < 3 * n_heads, pltpu.roll(lo, 2 * n_heads, axis=1),
                                    jnp.where(lane == 3 * n_heads, 1.0, 0.0))))
            c_o[...] = packed.astype(c_o.dtype)

    xn = xn_s[...]
    for w, o in ((wq, q_o), (wxr, xr_o), (wyr, yr_o), (wga, ga_o), (wgr, gr_o)):
        o[...] = _dot(xn, w[...]).astype(o.dtype)

    tm = xn.shape[0]
    n_heads = k32_o.shape[0] // tm
    head_dim = k32_o.shape[1]
    heads_per_step = wk.shape[1] // head_dim
    for w, o16, o32 in ((wk, k16_o, k32_o), (wv, v16_o, v32_o)):
        res = _dot(xn, w[...])
        o16[...] = res.astype(o16.dtype)
        for hh in range(heads_per_step):
            head = j * heads_per_step + hh
            o32[pl.ds(head, tm, stride=n_heads), :] = res[:, hh * head_dim:(hh + 1) * head_dim]


def _in_proj(x, g, w, tm, n_col_steps, seq=None):
    rows, d_model = x.shape
    n_heads, head_dim = w["n_heads"], w["head_dim"]
    cols = {}
    for arr, names in (("qkv", ("q", "k", "v")), ("rest", ("xr", "yr", "ga", "gr"))):
        start = 0
        for name in names:
            cols[name] = (arr, start, w["widths"][name])
            start += w["widths"][name]
    order = ("q", "k", "v", "xr", "yr", "ga", "gr")
    tiles = {name: cols[name][2] // n_col_steps for name in order}
    grid = (rows // tm, n_col_steps)

    def w_spec(name):
        first_block = cols[name][1] // tiles[name]
        return pl.BlockSpec((d_model, tiles[name]), lambda i, j: (0, first_block + j))

    def o_spec(name):
        return pl.BlockSpec((tm, tiles[name]), lambda i, j: (i, j))

    def o_shape(name, dtype):
        return jax.ShapeDtypeStruct((rows, cols[name][2]), dtype)

    def per_head():
        return pl.BlockSpec((tm, n_heads), lambda i, j: (i, 0))

    flat_spec = pl.BlockSpec((tm * n_heads, head_dim), lambda i, j: (i, 0))
    flat_shape = jax.ShapeDtypeStruct((rows * n_heads, head_dim), F32)
    out_specs = [o_spec("q"), o_spec("k"), o_spec("v"), flat_spec, flat_spec,
                 o_spec("xr"), o_spec("yr"), o_spec("ga"), o_spec("gr"), per_head()]
    out_shape = [o_shape("q", BF16), o_shape("k", BF16), o_shape("v", BF16),
                 flat_shape, flat_shape,
                 o_shape("xr", F32), o_shape("yr", F32), o_shape("ga", BF16), o_shape("gr", BF16),
                 jax.ShapeDtypeStruct((rows, n_heads), F32)]
    scratch = [pltpu.VMEM((tm, d_model), BF16)]
    if seq is not None:
        out_specs.append(pl.BlockSpec((tm, LANES), lambda i, j: (i, 0)))
        out_shape.append(jax.ShapeDtypeStruct((rows, LANES), BF16))
        scratch.insert(0, pltpu.VMEM((1, LANES), F32))
    return pl.pallas_call(
        functools.partial(_in_proj_kernel,
                          tiles_per_seq=None if seq is None else seq // tm,
                          inv_scale=head_dim ** 0.5),
        grid=grid,
        in_specs=[pl.BlockSpec((tm, d_model), lambda i, j: (i, 0)),
                  pl.BlockSpec((1, d_model), lambda i, j: (0, 0))]
                 + [w_spec(name) for name in order]
                 + [pl.BlockSpec((d_model, LANES), lambda i, j: (0, 0)),
                    pl.BlockSpec((1, LANES), lambda i, j: (0, 0))],
        out_specs=out_specs,
        out_shape=out_shape,
        scratch_shapes=scratch,
        compiler_params=_params("arbitrary", "arbitrary"),
        name="in_proj",
    )(x, g, *[w[cols[name][0]] for name in order], w["f"], w["b_f"])


def _split3_bf16(x):
    hi = x.astype(BF16)
    r = x - hi.astype(F32)
    mid = r.astype(BF16)
    lo = (r - mid.astype(F32)).astype(BF16)
    return hi.astype(F32), mid.astype(F32), lo.astype(F32)


def _attn_prompt_kernel(q_ref, k_ref, v_ref, c_ref, o_ref, qa_s, ka_s, s_s, *,
                        tq, scale, n_heads):
    seq, head_dim = q_ref.shape
    h = pl.program_id(1)
    src = lax.broadcasted_iota(jnp.int32, (LANES, LANES), 0)
    dst = lax.broadcasted_iota(jnp.int32, (LANES, LANES), 1)
    one_src = src == 3 * n_heads
    part_of_dst = jnp.where(dst < 3, dst, dst - 3) * n_heads + h
    sel_q = jnp.where((dst < 3) & (src == part_of_dst), 1.0,
                      jnp.where((dst >= 3) & (dst < 6) & one_src, 1.0, 0.0))
    sel_k = jnp.where((dst >= 3) & (dst < 6) & (src == part_of_dst), -1.0,
                      jnp.where((dst < 3) & one_src, 1.0, 0.0))
    c_rows = c_ref[...]
    qa_s[:, :head_dim] = q_ref[...]
    qa_s[:, head_dim:] = _dot(c_rows, sel_q.astype(BF16)).astype(BF16)
    ka_s[:, :head_dim] = k_ref[...]
    ka_s[:, head_dim:] = _dot(c_rows, sel_k.astype(BF16)).astype(BF16)

    sigma = scale * math.log2(math.e)
    row = lax.broadcasted_iota(jnp.int32, (tq, tq), 0)
    col = lax.broadcasted_iota(jnp.int32, (tq, tq), 1)
    visible = col <= row
    groups = tq // LANES
    for i in range(seq // tq):
        qa = qa_s[i * tq:(i + 1) * tq, :]
        m_part = jnp.full((tq, LANES), NEG_BIG, F32)
        for j in range(i + 1):
            s = _dot_nt(qa, ka_s[j * tq:(j + 1) * tq, :]) * sigma
            if j == i:
                s = jnp.where(visible, s, NEG_BIG)
            s_s[:, j * tq:(j + 1) * tq] = s
            for g in range(groups):
                m_part = jnp.maximum(m_part, s[:, g * LANES:(g + 1) * LANES])
        m = jnp.broadcast_to(jnp.max(m_part, axis=-1, keepdims=True), (tq, LANES))
        l_part = jnp.zeros((tq, LANES), F32)
        acc = jnp.zeros((tq, head_dim), F32)
        for j in range(i + 1):
            parts = []
            for g in range(groups):
                off = j * tq + g * LANES
                p = jnp.exp2(s_s[:, off:off + LANES] - m)
                l_part = l_part + p
                parts.append(p.astype(BF16))
            acc = acc + _dot(jnp.concatenate(parts, axis=1), v_ref[j * tq:(j + 1) * tq, :])
        l = jnp.sum(l_part, axis=-1, keepdims=True)
        o_ref[i * tq:(i + 1) * tq, :] = (acc / l).astype(o_ref.dtype)


def _attn_prompt(q, k, v, c, batch, seq, n_heads, head_dim, tq):
    head_spec = pl.BlockSpec((seq, head_dim), lambda b, h: (b, h))
    return pl.pallas_call(
        functools.partial(_attn_prompt_kernel, tq=tq, scale=head_dim ** -0.5, n_heads=n_heads),
        grid=(batch, n_heads),
        in_specs=[head_spec, head_spec, head_spec,
                  pl.BlockSpec((seq, LANES), lambda b, h: (b, 0))],
        out_specs=head_spec,
        out_shape=jax.ShapeDtypeStruct(q.shape, BF16),
        scratch_shapes=[pltpu.VMEM((seq, head_dim + LANES), BF16),
                        pltpu.VMEM((seq, head_dim + LANES), BF16),
                        pltpu.VMEM((tq, seq), F32)],
        compiler_params=_params("parallel", "parallel"),
        name="attn_prompt",
    )(q, k, v, c)


def _paged_logf_sums(lf_pages, n_heads):
    n_pages, pw = lf_pages.shape
    within = _prefix_sum(lf_pages, axis=1, start=n_heads)
    lane = lax.broadcasted_iota(jnp.int32, (n_pages, LANES), 1)
    totals = jnp.where(lane >= LANES - n_heads, within[:, pw - LANES:], 0.0)
    shift = n_heads
    while shift < LANES:
        totals = totals + pltpu.roll(totals, LANES - shift, axis=1)
        shift *= 2
    upto = _prefix_sum(totals, axis=0)
    before = upto - totals
    sums = within + jnp.concatenate([before] * (pw // LANES), axis=1)
    return sums, upto[n_pages - 1:, :]


def _paged_attn_step(step, q_ref, kn_ref, vn_ref, lfn_ref, k_refs, v_refs, lf_refs, o_ref,
                     s_s, c_s, cnew_s, m_s, l_s, acc_s, *, steps_per_seq, scale):
    pps = len(k_refs)
    page, n_heads, head_dim = k_refs[0].shape[1:]
    pw = page * n_heads
    part = lax.rem(step, steps_per_seq)

    head_of_row = lax.broadcasted_iota(jnp.int32, (n_heads, pw), 0)
    lane = lax.broadcasted_iota(jnp.int32, (n_heads, pw), 1)
    on_diag = head_of_row == lane % n_heads
    q = q_ref[0]

    @pl.when(part == 0)
    def _():
        c_past, c_total = _paged_logf_sums(
            jnp.concatenate([r[0] for r in lf_refs], axis=0), n_heads)
        c_s[...] = c_past
        own_lane = lane[:, :LANES] == head_of_row[:, :LANES]
        c_last = jnp.sum(jnp.where(own_lane, c_total, 0.0), axis=-1, keepdims=True)
        cnew_s[...] = c_last + lfn_ref[0]
        kn = kn_ref[0].astype(BF16).astype(F32)
        m_s[...] = jnp.sum(q.astype(F32) * kn, axis=-1, keepdims=True) * scale
        l_s[...] = jnp.ones_like(l_s)
        acc_s[...] = vn_ref[0].astype(BF16).astype(F32)

    c_new = cnew_s[...]
    for p in range(pps):
        kf = k_refs[p][0].reshape(pw, head_dim).astype(BF16)
        c_row = c_s[pl.ds(part * pps + p, 1), :]
        s = _dot_nt(q, kf) * scale + (c_new - c_row)
        s_s[:, p * pw:(p + 1) * pw] = jnp.where(on_diag, s, NEG_BIG)

    s_all = s_s[...]
    m_old = m_s[...]
    m_new = jnp.maximum(m_old, jnp.max(s_all, axis=-1, keepdims=True))
    p_all = jnp.exp(s_all - m_new)
    alpha = jnp.exp(m_old - m_new)
    l_s[...] = alpha * l_s[...] + jnp.sum(p_all, axis=-1, keepdims=True)
    m_s[...] = m_new
    p16 = p_all.astype(BF16)
    acc = alpha * acc_s[...]
    for p in range(pps):
        vf = v_refs[p][0].reshape(pw, head_dim).astype(BF16)
        acc = acc + _dot(p16[:, p * pw:(p + 1) * pw], vf)
    acc_s[...] = acc

    @pl.when(part == steps_per_seq - 1)
    def _():
        o_ref[0] = (acc_s[...] / l_s[...]).astype(o_ref.dtype)


def _rglru_gates(conv, r_pre, i_pre, b_rg, b_ig, neg_c_softplus):
    r = jax.nn.sigmoid(r_pre + b_rg)
    i_g = jax.nn.sigmoid(i_pre + b_ig)
    log_a = r * neg_c_softplus
    a = jnp.exp(log_a)
    b = jnp.sqrt(-jnp.tanh(log_a) * (1.0 + a * a)) * (i_g * conv)
    return a, b


def _rglru_prompt_kernel(xr_ref, yr_ref, cw_ref, cb_ref, wrg_ref, wig_ref, brg_ref, big_ref,
                         lam_ref, o_ref, hl_ref, xc_s, conv_s, rp_s, ip_s, a_s, b_s, h_s):
    c = pl.program_id(0)
    nb, tc, n_blocks, bw = xr_ref.shape
    halo = CONV_WIDTH - 1
    rows = nb * tc

    @pl.when(c == 0)
    def _():
        xc_s[:, 0:halo] = jnp.zeros((nb, halo, n_blocks, bw), F32)
        h_s[...] = jnp.zeros_like(h_s)

    @pl.when(c > 0)
    def _():
        xc_s[:, 0:halo] = xc_s[:, tc:tc + halo]

    xc_s[:, halo:halo + tc] = xr_ref[...]

    conv = cb_ref[...] + xc_s[:, 0:tc] * cw_ref[0]
    for i in range(1, CONV_WIDTH):
        conv = conv + xc_s[:, i:i + tc] * cw_ref[i]

    conv_s[...] = conv.reshape(rows * n_blocks, bw)
    for n in range(n_blocks):
        xn = conv_s[pl.ds(n, rows, stride=n_blocks), :].astype(BF16)
        rp_s[pl.ds(n, rows, stride=n_blocks), :] = _dot(xn, wrg_ref[n])
        ip_s[pl.ds(n, rows, stride=n_blocks), :] = _dot(xn, wig_ref[n])

    neg_c_softplus = -LRU_C * _softplus(-lam_ref[...])
    a, b = _rglru_gates(conv,
                        rp_s[...].reshape(nb, tc, n_blocks, bw),
                        ip_s[...].reshape(nb, tc, n_blocks, bw),
                        brg_ref[...], big_ref[...], neg_c_softplus)
    a_s[...] = a
    b_s[...] = b

    def step(t, hs):
        new = tuple(a_s[j, t] * hs[j] + b_s[j, t] for j in range(nb))
        for j in range(nb):
            b_s[j, t] = new[j]
        return new

    hs = lax.fori_loop(0, tc, step, tuple(h_s[j] for j in range(nb)), unroll=8)
    for j in range(nb):
        h_s[j] = hs[j]

    conv_s[...] = (b_s[...] * _gelu_tanh(yr_ref[...])).reshape(rows * n_blocks, bw)
    for n in range(n_blocks):
        o_ref[:, :, n * bw:(n + 1) * bw] = (
            conv_s[pl.ds(n, rows, stride=n_blocks), :].reshape(nb, tc, bw).astype(o_ref.dtype))
    hl_ref[...] = h_s[...]


def _rglru_prompt(xr, yr, w, batch, seq, tc):
    n_blocks, bw = w["n_rnn_blocks"], w["rnn_block"]
    xr4 = xr.reshape(batch, seq, n_blocks, bw)
    yr4 = yr.reshape(batch, seq, n_blocks, bw)
    rows = batch * tc
    halo = CONV_WIDTH - 1

    def const(shape):
        return pl.BlockSpec(shape, lambda c: (0,) * len(shape))

    seq_spec = pl.BlockSpec((batch, tc, n_blocks, bw), lambda c: (0, c, 0, 0))
    out, h_last = pl.pallas_call(
        _rglru_prompt_kernel,
        grid=(seq // tc,),
        in_specs=[seq_spec, seq_spec,
                  const((CONV_WIDTH, n_blocks, bw)), const((n_blocks, bw)),
                  const((n_blocks, bw, bw)), const((n_blocks, bw, bw)),
                  const((n_blocks, bw)), const((n_blocks, bw)), const((n_blocks, bw))],
        out_specs=[pl.BlockSpec((batch, tc, n_blocks * bw), lambda c: (0, c, 0)),
                   const((batch, n_blocks, bw))],
        out_shape=[jax.ShapeDtypeStruct((batch, seq, n_blocks * bw), BF16),
                   jax.ShapeDtypeStruct((batch, n_blocks, bw), F32)],
        scratch_shapes=[pltpu.VMEM((batch, tc + halo, n_blocks, bw), F32),
                        pltpu.VMEM((rows * n_blocks, bw), F32),
                        pltpu.VMEM((rows * n_blocks, bw), F32),
                        pltpu.VMEM((rows * n_blocks, bw), F32),
                        pltpu.VMEM((batch, tc, n_blocks, bw), F32),
                        pltpu.VMEM((batch, tc, n_blocks, bw), F32),
                        pltpu.VMEM((batch, n_blocks, bw), F32)],
        compiler_params=_params("arbitrary"),
        name="rglru_prompt",
    )(xr4, yr4, w["conv_w3"], w["conv_b2"], w["w_rg"], w["w_ig"], w["b_rg2"], w["b_ig2"],
      w["lam2"])
    return out.reshape(batch * seq, n_blocks * bw), h_last.reshape(batch, n_blocks * bw)


def _rglru_sample_kernel(xr_ref, yr_ref, buf_ref, h0_ref, cw_ref, cb_ref, wrg_ref, wig_ref,
                         brg_ref, big_ref, lam_ref, o_ref, h_ref, nbuf_ref):
    halo = CONV_WIDTH - 1
    n_blocks, bw = wrg_ref.shape[0], wrg_ref.shape[1]
    xr = xr_ref[...]
    conv = cb_ref[...] + xr * cw_ref[halo:halo + 1, :]
    for i in range(halo):
        conv = conv + buf_ref[i] * cw_ref[i:i + 1, :]
    r_parts, i_parts = [], []
    for n in range(n_blocks):
        xn = conv[:, n * bw:(n + 1) * bw].astype(BF16)
        r_parts.append(_dot(xn, wrg_ref[n]))
        i_parts.append(_dot(xn, wig_ref[n]))
    neg_c_softplus = -LRU_C * _softplus(-lam_ref[...])
    a, b = _rglru_gates(conv, jnp.concatenate(r_parts, axis=1), jnp.concatenate(i_parts, axis=1),
                        brg_ref[...], big_ref[...], neg_c_softplus)
    h = a * h0_ref[...] + b
    h_ref[...] = h
    o_ref[...] = h * _gelu_tanh(yr_ref[...])
    for i in range(halo - 1):
        nbuf_ref[i] = buf_ref[i + 1]
    nbuf_ref[halo - 1] = xr


def _rglru_sample(xr, yr, conv_buf_t, h0, w):
    rows, width = xr.shape
    halo = CONV_WIDTH - 1
    return pl.pallas_call(
        _rglru_sample_kernel,
        out_shape=[jax.ShapeDtypeStruct((rows, width), F32),
                   jax.ShapeDtypeStruct((rows, width), F32),
                   jax.ShapeDtypeStruct((halo, rows, width), F32)],
        compiler_params=pltpu.CompilerParams(vmem_limit_bytes=VMEM_LIMIT_BYTES),
        name="rglru_sample",
    )(xr, yr, conv_buf_t, h0, w["conv_w"], w["conv_b"], w["w_rg"], w["w_ig"],
      w["b_rg"], w["b_ig"], w["lam"])


def _out_proj_kernel(attn_ref, rnn_ref, ga_ref, gr_ref, x_ref, woa_ref, wor_ref, wout_ref,
                     gpm_ref, gpf_ref, h_o, hn_o):
    o = (jax.nn.sigmoid(ga_ref[...].astype(F32)) * _dot(attn_ref[...].astype(BF16), woa_ref[...])
         + jax.nn.sigmoid(gr_ref[...].astype(F32)) * _dot(rnn_ref[...].astype(BF16), wor_ref[...]))
    mix = _dot(o.astype(BF16), wout_ref[...])
    h = x_ref[...] + _rms_norm(mix, gpm_ref[...])
    h_o[...] = h
    hn_o[...] = _rms_norm(h, gpf_ref[...]).astype(hn_o.dtype)


def _out_proj(attn, rnn, ga, gr, x, w, tm):
    rows, d_model = x.shape

    def row(width):
        return pl.BlockSpec((tm, width), lambda i: (i, 0))

    def const(shape):
        return pl.BlockSpec(shape, lambda i: (0, 0), pipeline_mode=pl.Buffered(1))

    return pl.pallas_call(
        _out_proj_kernel,
        grid=(rows // tm,),
        in_specs=[row(attn.shape[1]), row(rnn.shape[1]), row(d_model), row(d_model), row(d_model),
                  const(w["o_attn"].shape), const(w["o_rnn"].shape), const(w["out"].shape),
                  const((1, d_model)), const((1, d_model))],
        out_specs=[row(d_model), row(d_model)],
        out_shape=[jax.ShapeDtypeStruct((rows, d_model), F32),
                   jax.ShapeDtypeStruct((rows, d_model), BF16)],
        compiler_params=_params("parallel"),
        name="out_proj",
    )(attn, rnn, ga, gr, x, w["o_attn"], w["o_rnn"], w["out"], w["g_post_mix"], w["g_pre_ffn"])


def _ffn_step(hn_ref, h_ref, wg_ref, wu_ref, wd_ref, g_ref, y_o, acc_s):
    j = pl.program_id(1)

    @pl.when(j == 0)
    def _():
        acc_s[...] = jnp.zeros_like(acc_s)

    hn = hn_ref[...]
    gate = _dot(hn, wg_ref[...])
    up = _dot(hn, wu_ref[...])
    acc_s[...] += _dot((jax.nn.silu(gate) * up).astype(BF16), wd_ref[...])

    @pl.when(j == pl.num_programs(1) - 1)
    def _():
        y_o[...] = h_ref[...] + _rms_norm(acc_s[...], g_ref[...])


def _ffn_specs(rows, d_model, d_ff, tm, tf):
    in_specs = [pl.BlockSpec((tm, d_model), lambda i, j, *_: (i, 0)),
                pl.BlockSpec((tm, d_model), lambda i, j, *_: (i, 0)),
                pl.BlockSpec((d_model, tf), lambda i, j, *_: (0, j)),
                pl.BlockSpec((d_model, tf), lambda i, j, *_: (0, j)),
                pl.BlockSpec((tf, d_model), lambda i, j, *_: (j, 0)),
                pl.BlockSpec((1, d_model), lambda i, j, *_: (0, 0))]
    out_spec = pl.BlockSpec((tm, d_model), lambda i, j, *_: (i, 0))
    out_shape = jax.ShapeDtypeStruct((rows, d_model), F32)
    return (rows // tm, d_ff // tf), in_specs, out_spec, out_shape


def _ffn(hn, h, w, tm, tf):
    rows, d_model = h.shape
    grid, in_specs, out_spec, out_shape = _ffn_specs(rows, d_model, w["gate"].shape[1], tm, tf)
    return pl.pallas_call(
        _ffn_step,
        grid=grid,
        in_specs=in_specs,
        out_specs=out_spec,
        out_shape=out_shape,
        scratch_shapes=[pltpu.VMEM((tm, d_model), F32)],
        compiler_params=_params("parallel", "arbitrary"),
        name="ffn",
    )(hn, h, w["gate"], w["up"], w["down"], w["g_post_ffn"])


def _ffn_attn_kernel(pt_ref, hn_ref, h_ref, wg_ref, wu_ref, wd_ref, g_ref,
                     q_ref, kn_ref, vn_ref, lfn_ref, *refs,
                     n_pages, pages_per_step, n_attn_steps, scale):
    del pt_ref
    k_refs = refs[:pages_per_step]
    v_refs = refs[pages_per_step:2 * pages_per_step]
    lf_refs = refs[2 * pages_per_step:2 * pages_per_step + n_pages]
    y_o, attn_o, acc_s, s_s, c_s, cnew_s, m_s, l_s, oacc_s = refs[2 * pages_per_step + n_pages:]
    step = pl.program_id(0) * pl.num_programs(1) + pl.program_id(1)

    @pl.when(step < n_attn_steps)
    def _():
        _paged_attn_step(step, q_ref, kn_ref, vn_ref, lfn_ref, k_refs, v_refs, lf_refs, attn_o,
                         s_s, c_s, cnew_s, m_s, l_s, oacc_s,
                         steps_per_seq=n_pages // pages_per_step, scale=scale)

    _ffn_step(hn_ref, h_ref, wg_ref, wu_ref, wd_ref, g_ref, y_o, acc_s)


def _ffn_with_paged_attn(hn, h, w, tm, tf, q, k_new, v_new, lf_new, cache_k, cache_v, cache_lf,
                         page_table, pages_per_step):
    rows, d_model = h.shape
    grid, ffn_in, y_spec, y_shape = _ffn_specs(rows, d_model, w["gate"].shape[1], tm, tf)
    batch, n_pages = page_table.shape
    _, page, n_heads, head_dim = cache_k.shape
    pw = page * n_heads
    steps_per_seq = n_pages // pages_per_step
    n_attn_steps = batch * steps_per_seq
    assert n_attn_steps <= grid[0] * grid[1]

    def seq_and_part(i, j):
        step = jnp.minimum(i * grid[1] + j, n_attn_steps - 1)
        return step // steps_per_seq, step % steps_per_seq

    def row_spec(shape):
        return pl.BlockSpec((1,) + shape, lambda i, j, pt: (seq_and_part(i, j)[0], 0, 0))

    def page_spec(p):
        def index(i, j, pt):
            b, part = seq_and_part(i, j)
            return (pt[b, part * pages_per_step + p], 0, 0, 0)
        return pl.BlockSpec((1, page, n_heads, head_dim), index)

    def lf_spec(p):
        return pl.BlockSpec((1, 1, pw), lambda i, j, pt: (pt[seq_and_part(i, j)[0], p], 0, 0))

    return pl.pallas_call(
        functools.partial(_ffn_attn_kernel, n_pages=n_pages, pages_per_step=pages_per_step,
                          n_attn_steps=n_attn_steps, scale=head_dim ** -0.5),
        grid_spec=pltpu.PrefetchScalarGridSpec(
            num_scalar_prefetch=1,
            grid=grid,
            in_specs=ffn_in
                     + [row_spec((n_heads, head_dim))] * 3 + [row_spec((n_heads, 1))]
                     + [page_spec(p) for p in range(pages_per_step)] * 2
                     + [lf_spec(p) for p in range(n_pages)],
            out_specs=[y_spec, row_spec((n_heads, head_dim))],
            scratch_shapes=[pltpu.VMEM((tm, d_model), F32),
                            pltpu.VMEM((n_heads, pages_per_step * pw), F32),
                            pltpu.VMEM((n_pages, pw), F32),
                            pltpu.VMEM((n_heads, 1), F32),
                            pltpu.VMEM((n_heads, 1), F32),
                            pltpu.VMEM((n_heads, 1), F32),
                            pltpu.VMEM((n_heads, head_dim), F32)],
        ),
        out_shape=[y_shape, jax.ShapeDtypeStruct((batch, n_heads, head_dim), BF16)],
        compiler_params=_params("arbitrary", "arbitrary"),
        name="ffn_attn",
    )(page_table, hn, h, w["gate"], w["up"], w["down"], w["g_post_ffn"],
      q, k_new, v_new, lf_new,
      *([cache_k] * pages_per_step), *([cache_v] * pages_per_step), *([cache_lf] * n_pages))


def _layer_weights(l, n_heads, head_dim, g_pre_mix, w_in, b_f, conv_w, conv_b, w_rg, b_rg, w_ig,
                   b_ig, lru_lambda, w_o_attn, w_o_rnn, w_out, g_post_mix, g_pre_ffn, w_gate,
                   w_up, w_down, g_post_ffn):
    d_model = w_in.shape[1]
    n_blocks, bw = w_rg.shape[1], w_rg.shape[2]
    d_rnn = n_blocks * bw
    aw = n_heads * head_dim
    wi = w_in[l]
    f_lo, f_hi = 3 * aw, 3 * aw + n_heads
    cols = {
        "qkv": wi[:, :f_lo].astype(BF16),
        "rest": wi[:, f_hi:].astype(BF16),
        "f": jnp.pad(wi[:, f_lo:f_hi].astype(BF16), ((0, 0), (0, LANES - n_heads))),
        "b_f": jnp.pad(b_f[l].reshape(1, n_heads), ((0, 0), (0, LANES - n_heads))),
        "widths": {"q": aw, "k": aw, "v": aw, "xr": d_rnn, "yr": d_rnn,
                   "ga": d_model, "gr": d_model},
        "n_heads": n_heads,
        "head_dim": head_dim,
    }
    rnn = {
        "n_rnn_blocks": n_blocks, "rnn_block": bw,
        "conv_w": conv_w[l], "conv_b": conv_b[l].reshape(1, d_rnn),
        "b_rg": b_rg[l].reshape(1, d_rnn), "b_ig": b_ig[l].reshape(1, d_rnn),
        "lam": lru_lambda[l].reshape(1, d_rnn),
        "conv_w3": conv_w[l].reshape(CONV_WIDTH, n_blocks, bw),
        "conv_b2": conv_b[l].reshape(n_blocks, bw),
        "b_rg2": b_rg[l].reshape(n_blocks, bw), "b_ig2": b_ig[l].reshape(n_blocks, bw),
        "lam2": lru_lambda[l].reshape(n_blocks, bw),
        "w_rg": w_rg[l].astype(BF16), "w_ig": w_ig[l].astype(BF16),
    }
    mix = {"o_attn": w_o_attn[l].astype(BF16), "o_rnn": w_o_rnn[l].astype(BF16),
           "out": w_out[l].astype(BF16),
           "g_post_mix": g_post_mix[l].reshape(1, d_model),
           "g_pre_ffn": g_pre_ffn[l].reshape(1, d_model)}
    ffn = {"gate": w_gate[l].astype(BF16), "up": w_up[l].astype(BF16),
           "down": w_down[l].astype(BF16), "g_post_ffn": g_post_ffn[l].reshape(1, d_model)}
    return g_pre_mix[l].reshape(1, d_model), cols, rnn, mix, ffn


def kernel(x_prompt, x_sample, cache_k, cache_v, cache_logf, state_h, state_conv, page_table,
           g_pre_mix, w_in, b_f, conv_w, conv_b, w_rg, b_rg, w_ig, b_ig, lru_lambda,
           w_o_attn, w_o_rnn, w_out, g_post_mix, g_pre_ffn, w_gate, w_up, w_down, g_post_ffn):
    batch, seq, d_model = x_prompt.shape
    dec_batch, dec_seq, _ = x_sample.shape
    assert dec_seq == 1, "sample group carries one new token per sequence"
    depth, n_phys, page, n_heads, head_dim = cache_k.shape
    aw = n_heads * head_dim
    d_rnn = state_h.shape[2]
    n_pages = page_table.shape[1]
    assert LANES % n_heads == 0 and (page * n_heads) % LANES == 0

    xp = x_prompt.reshape(batch * seq, d_model)
    xs = x_sample.reshape(dec_batch, d_model)
    outs = {name: [] for name in ("kp", "vp", "lp", "hp", "cp", "ks", "vs", "ls", "hs", "cs")}

    for l in range(depth):
        g_in, w_cols, w_rnn, w_mix, w_ffn = _layer_weights(
            l, n_heads, head_dim, g_pre_mix, w_in, b_f, conv_w, conv_b, w_rg, b_rg, w_ig, b_ig,
            lru_lambda, w_o_attn, w_o_rnn, w_out, g_post_mix, g_pre_ffn, w_gate, w_up, w_down,
            g_post_ffn)

        q, k16, v16, k, v, xr, yr, ga, gr, lf, c = _in_proj(xp, g_in, w_cols, tm=512,
                                                            n_col_steps=4, seq=seq)
        qs, _, _, ks, vs, xrs, yrs, gas, grs, lfs = _in_proj(xs, g_in, w_cols, tm=dec_batch,
                                                             n_col_steps=4)
        ks = ks.reshape(dec_batch, n_heads, head_dim)
        vs = vs.reshape(dec_batch, n_heads, head_dim)

        attn = _attn_prompt(q, k16, v16, c, batch, seq, n_heads, head_dim, tq=256)
        rnn, h_last = _rglru_prompt(xr, yr, w_rnn, batch, seq, tc=128)
        h, hn = _out_proj(attn, rnn, ga, gr, xp, w_mix, tm=256)
        xp_next, attn_s = _ffn_with_paged_attn(
            hn, h, w_ffn, 512, 256,
            qs.reshape(dec_batch, n_heads, head_dim), ks, vs, lfs.reshape(dec_batch, n_heads, 1),
            cache_k[l], cache_v[l], cache_logf[l].reshape(n_phys, 1, page * n_heads),
            page_table, pages_per_step=8)
        outs["kp"].append(k.reshape(batch, seq, n_heads, head_dim))
        outs["vp"].append(v.reshape(batch, seq, n_heads, head_dim))
        outs["lp"].append(lf.reshape(batch, seq, n_heads))
        outs["hp"].append(h_last)
        outs["cp"].append(xr.reshape(batch, seq, d_rnn)[:, seq - (CONV_WIDTH - 1):])

        rnn, h_new, buf_new = _rglru_sample(xrs, yrs, jnp.swapaxes(state_conv[l], 0, 1),
                                            state_h[l], w_rnn)
        h, hn = _out_proj(attn_s.reshape(dec_batch, aw), rnn, gas, grs, xs, w_mix, tm=dec_batch)
        xs_next = _ffn(hn, h, w_ffn, tm=dec_batch, tf=512)
        outs["ks"].append(ks.reshape(dec_batch, 1, n_heads, head_dim))
        outs["vs"].append(vs.reshape(dec_batch, 1, n_heads, head_dim))
        outs["ls"].append(lfs.reshape(dec_batch, 1, n_heads))
        outs["hs"].append(h_new)
        outs["cs"].append(jnp.swapaxes(buf_new, 0, 1))

        xp, xs = xp_next, xs_next

    st = {name: jnp.stack(vals) for name, vals in outs.items()}
    return (xp.reshape(batch, seq, d_model), xs.reshape(dec_batch, 1, d_model),
            st["kp"], st["vp"], st["lp"], st["hp"], st["cp"],
            st["ks"], st["vs"], st["ls"], st["hs"], st["cs"])
```

```python
import functools
import math

import jax
import jax.numpy as jnp
from jax import lax
from jax.experimental import pallas as pl
from jax.experimental.pallas import tpu as pltpu

F32 = jnp.float32
BF16 = jnp.bfloat16

EPS = 1e-6
LRU_C = 8.0
CONV_WIDTH = 4
NEG_BIG = -1e30

SUBLANES = 8
LANES = 128
VMEM_LIMIT_BYTES = 56 * 1024 * 1024


def _params(*semantics):
    return pltpu.CompilerParams(dimension_semantics=semantics,
                                vmem_limit_bytes=VMEM_LIMIT_BYTES)


def _rms_norm(x, g):
    inv = lax.rsqrt(jnp.mean(x * x, axis=-1, keepdims=True) + EPS)
    return (x * inv) * g


def _log_sigmoid(x):
    return jnp.minimum(x, 0.0) - jnp.log1p(jnp.exp(-jnp.abs(x)))


def _softplus(x):
    return jnp.maximum(x, 0.0) + jnp.log1p(jnp.exp(-jnp.abs(x)))


def _gelu_tanh(x):
    c = math.sqrt(2.0 / math.pi)
    return x * (0.5 * (1.0 + jnp.tanh(c * (x + 0.044715 * (x * x * x)))))


def _dot(a, b):
    return jnp.dot(a, b, preferred_element_type=F32)


def _dot_nt(a, b):
    return lax.dot_general(a, b, (((1,), (1,)), ((), ())), preferred_element_type=F32)


def _cast_kernel(x_ref, o_ref):
    o_ref[...] = x_ref[...].astype(o_ref.dtype)


def _cast_shifted_kernel(a_ref, b_ref, o_ref, *, shift):
    groups = a_ref.shape[1] // LANES
    lane = lax.broadcasted_iota(jnp.int32, (a_ref.shape[0], LANES), 1)
    for g in range(groups):
        cur = a_ref[:, g * LANES:(g + 1) * LANES]
        nxt = a_ref[:, (g + 1) * LANES:(g + 2) * LANES] if g + 1 < groups else b_ref[...]
        o_ref[:, g * LANES:(g + 1) * LANES] = jnp.where(
            lane < LANES - shift, pltpu.roll(cur, LANES - shift, axis=1),
            pltpu.roll(nxt, LANES - shift, axis=1)).astype(o_ref.dtype)


def _cast_columns(w, first_col, n_cols, tile):
    rows = w.shape[0]
    assert n_cols % tile == 0 and tile % LANES == 0
    shift = first_col % tile
    base = first_col // tile
    out_spec = pl.BlockSpec((rows, tile), lambda j: (0, j))
    out_shape = jax.ShapeDtypeStruct((rows, n_cols), BF16)
    if shift == 0:
        return pl.pallas_call(
            _cast_kernel, grid=(n_cols // tile,),
            in_specs=[pl.BlockSpec((rows, tile), lambda j: (0, base + j))],
            out_specs=out_spec, out_shape=out_shape,
            compiler_params=_params("parallel"), name="cast",
        )(w)
    assert shift < LANES and first_col + n_cols <= w.shape[1]
    per_tile = tile // LANES
    return pl.pallas_call(
        functools.partial(_cast_shifted_kernel, shift=shift), grid=(n_cols // tile,),
        in_specs=[pl.BlockSpec((rows, tile), lambda j: (0, base + j)),
                  pl.BlockSpec((rows, LANES), lambda j: (0, (base + j + 1) * per_tile))],
        out_specs=out_spec, out_shape=out_shape,
        compiler_params=_params("parallel"), name="cast_shifted",
    )(w, w)


def _prefix_sum(x, axis, start=1):
    idx = lax.broadcasted_iota(jnp.int32, x.shape, axis)
    shift = start
    while shift < x.shape[axis]:
        x = x + jnp.where(idx >= shift, pltpu.roll(x, shift, axis=axis), 0.0)
        shift *= 2
    return x


def _in_proj_kernel(x_ref, g_ref, wq, wk, wv, wxr, wyr, wga, wgr, wf, bf_ref,
                    q_o, k16_o, v16_o, k32_o, v32_o, xr_o, yr_o, ga_o, gr_o, lf_o, *rest,
                    tiles_per_seq, inv_scale):
    i, j = pl.program_id(0), pl.program_id(1)
    xn_s = rest[-1]
    n_heads = lf_o.shape[1]

    @pl.when(j == 0)
    def _():
        xn = _rms_norm(x_ref[...], g_ref[...]).astype(BF16)
        xn_s[...] = xn
        lf = _log_sigmoid(_dot(xn, wf[...]) + bf_ref[...])
        lf_o[...] = lf[:, :n_heads]
        if tiles_per_seq is not None:
            c_o, carry_s = rest[0], rest[1]

            @pl.when(lax.rem(i, tiles_per_seq) == 0)
            def _():
                carry_s[...] = jnp.zeros_like(carry_s)

            c = _prefix_sum(lf, axis=0) + carry_s[...]
            carry_s[...] = c[c.shape[0] - 1:, :]
            hi, mid, lo = _split3_bf16(c * inv_scale)
            lane = lax.broadcasted_iota(jnp.int32, c.shape, 1)
            packed = jnp.where(
                lane < n_heads, hi,
                jnp.where(lane < 2 * n_heads, pltpu.roll(mid, n_heads, axis=1),
                          jnp.where(lane < 3 * n_heads, pltpu.roll(lo, 2 * n_heads, axis=1),
                                    jnp.where(lane == 3 * n_heads, 1.0, 0.0))))
            c_o[...] = packed.astype(c_o.dtype)

    xn = xn_s[...]
    for w, o in ((wq, q_o), (wxr, xr_o), (wyr, yr_o), (wga, ga_o), (wgr, gr_o)):
        o[...] = _dot(xn, w[...]).astype(o.dtype)

    tm = xn.shape[0]
    n_heads = k32_o.shape[0] // tm
    head_dim = k32_o.shape[1]
    heads_per_step = wk.shape[1] // head_dim
    for w, o16, o32 in ((wk, k16_o, k32_o), (wv, v16_o, v32_o)):
        res = _dot(xn, w[...])
        o16[...] = res.astype(o16.dtype)
        for hh in range(heads_per_step):
            head = j * heads_per_step + hh
            o32[pl.ds(head, tm, stride=n_heads), :] = res[:, hh * head_dim:(hh + 1) * head_dim]


def _in_proj(x, g, w, tm, n_col_steps, seq=None):
    rows, d_model = x.shape
    n_heads, head_dim = w["n_heads"], w["head_dim"]
    cols = {}
    for arr, names in (("qkv", ("q", "k", "v")), ("rest", ("xr", "yr", "ga", "gr"))):
        start = 0
        for name in names:
            cols[name] = (arr, start, w["widths"][name])
            start += w["widths"][name]
    order = ("q", "k", "v", "xr", "yr", "ga", "gr")
    tiles = {name: cols[name][2] // n_col_steps for name in order}
    grid = (rows // tm, n_col_steps)

    def w_spec(name):
        first_block = cols[name][1] // tiles[name]
        return pl.BlockSpec((d_model, tiles[name]), lambda i, j: (0, first_block + j))

    def o_spec(name):
        return pl.BlockSpec((tm, tiles[name]), lambda i, j: (i, j))

    def o_shape(name, dtype):
        return jax.ShapeDtypeStruct((rows, cols[name][2]), dtype)

    def per_head():
        return pl.BlockSpec((tm, n_heads), lambda i, j: (i, 0))

    flat_spec = pl.BlockSpec((tm * n_heads, head_dim), lambda i, j: (i, 0))
    flat_shape = jax.ShapeDtypeStruct((rows * n_heads, head_dim), F32)
    out_specs = [o_spec("q"), o_spec("k"), o_spec("v"), flat_spec, flat_spec,
                 o_spec("xr"), o_spec("yr"), o_spec("ga"), o_spec("gr"), per_head()]
    out_shape = [o_shape("q", BF16), o_shape("k", BF16), o_shape("v", BF16),
                 flat_shape, flat_shape,
                 o_shape("xr", F32), o_shape("yr", F32), o_shape("ga", BF16), o_shape("gr", BF16),
                 jax.ShapeDtypeStruct((rows, n_heads), F32)]
    scratch = [pltpu.VMEM((tm, d_model), BF16)]
    if seq is not None:
        out_specs.append(pl.BlockSpec((tm, LANES), lambda i, j: (i, 0)))
        out_shape.append(jax.ShapeDtypeStruct((rows, LANES), BF16))
        scratch.insert(0, pltpu.VMEM((1, LANES), F32))
    return pl.pallas_call(
        functools.partial(_in_proj_kernel,
                          tiles_per_seq=None if seq is None else seq // tm,
                          inv_scale=head_dim ** 0.5),
        grid=grid,
        in_specs=[pl.BlockSpec((tm, d_model), lambda i, j: (i, 0)),
                  pl.BlockSpec((1, d_model), lambda i, j: (0, 0))]
                 + [w_spec(name) for name in order]
                 + [pl.BlockSpec((d_model, LANES), lambda i, j: (0, 0)),
                    pl.BlockSpec((1, LANES), lambda i, j: (0, 0))],
        out_specs=out_specs,
        out_shape=out_shape,
        scratch_shapes=scratch,
        compiler_params=_params("arbitrary", "arbitrary"),
        name="in_proj",
    )(x, g, *[w[cols[name][0]] for name in order], w["f"], w["b_f"])


def _split3_bf16(x):
    hi = x.astype(BF16)
    r = x - hi.astype(F32)
    mid = r.astype(BF16)
    lo = (r - mid.astype(F32)).astype(BF16)
    return hi.astype(F32), mid.astype(F32), lo.astype(F32)


def _attn_prompt_kernel(q_ref, k_ref, v_ref, c_ref, o_ref, qa_s, ka_s, s_s, *,
                        tq, scale, n_heads):
    seq, head_dim = q_ref.shape
    h = pl.program_id(1)
    src = lax.broadcasted_iota(jnp.int32, (LANES, LANES), 0)
    dst = lax.broadcasted_iota(jnp.int32, (LANES, LANES), 1)
    one_src = src == 3 * n_heads
    part_of_dst = jnp.where(dst < 3, dst, dst - 3) * n_heads + h
    sel_q = jnp.where((dst < 3) & (src == part_of_dst), 1.0,
                      jnp.where((dst >= 3) & (dst < 6) & one_src, 1.0, 0.0))
    sel_k = jnp.where((dst >= 3) & (dst < 6) & (src == part_of_dst), -1.0,
                      jnp.where((dst < 3) & one_src, 1.0, 0.0))
    c_rows = c_ref[...]
    qa_s[:, :head_dim] = q_ref[...]
    qa_s[:, head_dim:] = _dot(c_rows, sel_q.astype(BF16)).astype(BF16)
    ka_s[:, :head_dim] = k_ref[...]
    ka_s[:, head_dim:] = _dot(c_rows, sel_k.astype(BF16)).astype(BF16)

    sigma = scale * math.log2(math.e)
    row = lax.broadcasted_iota(jnp.int32, (tq, tq), 0)
    col = lax.broadcasted_iota(jnp.int32, (tq, tq), 1)
    visible = col <= row
    groups = tq // LANES
    for i in range(seq // tq):
        qa = qa_s[i * tq:(i + 1) * tq, :]
        m_part = jnp.full((tq, LANES), NEG_BIG, F32)
        for j in range(i + 1):
            s = _dot_nt(qa, ka_s[j * tq:(j + 1) * tq, :]) * sigma
            if j == i:
                s = jnp.where(visible, s, NEG_BIG)
            s_s[:, j * tq:(j + 1) * tq] = s
            for g in range(groups):
                m_part = jnp.maximum(m_part, s[:, g * LANES:(g + 1) * LANES])
        m = jnp.broadcast_to(jnp.max(m_part, axis=-1, keepdims=True), (tq, LANES))
        l_part = jnp.zeros((tq, LANES), F32)
        acc = jnp.zeros((tq, head_dim), F32)
        for j in range(i + 1):
            parts = []
            for g in range(groups):
                off = j * tq + g * LANES
                p = jnp.exp2(s_s[:, off:off + LANES] - m)
                l_part = l_part + p
                parts.append(p.astype(BF16))
            acc = acc + _dot(jnp.concatenate(parts, axis=1), v_ref[j * tq:(j + 1) * tq, :])
        l = jnp.sum(l_part, axis=-1, keepdims=True)
        o_ref[i * tq:(i + 1) * tq, :] = (acc / l).astype(o_ref.dtype)


def _attn_prompt(q, k, v, c, batch, seq, n_heads, head_dim, tq):
    head_spec = pl.BlockSpec((seq, head_dim), lambda b, h: (b, h))
    return pl.pallas_call(
        functools.partial(_attn_prompt_kernel, tq=tq, scale=head_dim ** -0.5, n_heads=n_heads),
        grid=(batch, n_heads),
        in_specs=[head_spec, head_spec, head_spec,
                  pl.BlockSpec((seq, LANES), lambda b, h: (b, 0))],
        out_specs=head_spec,
        out_shape=jax.ShapeDtypeStruct(q.shape, BF16),
        scratch_shapes=[pltpu.VMEM((seq, head_dim + LANES), BF16),
                        pltpu.VMEM((seq, head_dim + LANES), BF16),
                        pltpu.VMEM((tq, seq), F32)],
        compiler_params=_params("parallel", "parallel"),
        name="attn_prompt",
    )(q, k, v, c)


def _paged_logf_sums(lf_pages, n_heads):
    n_pages, pw = lf_pages.shape
    within = _prefix_sum(lf_pages, axis=1, start=n_heads)
    lane = lax.broadcasted_iota(jnp.int32, (n_pages, LANES), 1)
    totals = jnp.where(lane >= LANES - n_heads, within[:, pw - LANES:], 0.0)
    shift = n_heads
    while shift < LANES:
        totals = totals + pltpu.roll(totals, LANES - shift, axis=1)
        shift *= 2
    upto = _prefix_sum(totals, axis=0)
    before = upto - totals
    sums = within + jnp.concatenate([before] * (pw // LANES), axis=1)
    return sums, upto[n_pages - 1:, :]


def _paged_attn_step(step, q_ref, kn_ref, vn_ref, lfn_ref, k_refs, v_refs, lf_refs, o_ref,
                     s_s, c_s, cnew_s, m_s, l_s, acc_s, *, steps_per_seq, scale):
    pps = len(k_refs)
    page, n_heads, head_dim = k_refs[0].shape[1:]
    pw = page * n_heads
    part = lax.rem(step, steps_per_seq)

    head_of_row = lax.broadcasted_iota(jnp.int32, (n_heads, pw), 0)
    lane = lax.broadcasted_iota(jnp.int32, (n_heads, pw), 1)
    on_diag = head_of_row == lane % n_heads
    q = q_ref[0]

    @pl.when(part == 0)
    def _():
        c_past, c_total = _paged_logf_sums(
            jnp.concatenate([r[0] for r in lf_refs], axis=0), n_heads)
        c_s[...] = c_past
        own_lane = lane[:, :LANES] == head_of_row[:, :LANES]
        c_last = jnp.sum(jnp.where(own_lane, c_total, 0.0), axis=-1, keepdims=True)
        cnew_s[...] = c_last + lfn_ref[0]
        kn = kn_ref[0].astype(BF16).astype(F32)
        m_s[...] = jnp.sum(q.astype(F32) * kn, axis=-1, keepdims=True) * scale
        l_s[...] = jnp.ones_like(l_s)
        acc_s[...] = vn_ref[0].astype(BF16).astype(F32)

    c_new = cnew_s[...]
    for p in range(pps):
        kf = k_refs[p][0].reshape(pw, head_dim).astype(BF16)
        c_row = c_s[pl.ds(part * pps + p, 1), :]
        s = _dot_nt(q, kf) * scale + (c_new - c_row)
        s_s[:, p * pw:(p + 1) * pw] = jnp.where(on_diag, s, NEG_BIG)

    s_all = s_s[...]
    m_old = m_s[...]
    m_new = jnp.maximum(m_old, jnp.max(s_all, axis=-1, keepdims=True))
    p_all = jnp.exp(s_all - m_new)
    alpha = jnp.exp(m_old - m_new)
    l_s[...] = alpha * l_s[...] + jnp.sum(p_all, axis=-1, keepdims=True)
    m_s[...] = m_new
    p16 = p_all.astype(BF16)
    acc = alpha * acc_s[...]
    for p in range(pps):
        vf = v_refs[p][0].reshape(pw, head_dim).astype(BF16)
        acc = acc + _dot(p16[:, p * pw:(p + 1) * pw], vf)
    acc_s[...] = acc

    @pl.when(part == steps_per_seq - 1)
    def _():
        o_ref[0] = (acc_s[...] / l_s[...]).astype(o_ref.dtype)


def _attn_sample_kernel(pt_ref, q_ref, kn_ref, vn_ref, lfn_ref, *refs, pages_per_step, n_pages):
    del pt_ref
    k_refs = refs[:pages_per_step]
    v_refs = refs[pages_per_step:2 * pages_per_step]
    lf_refs = refs[2 * pages_per_step:2 * pages_per_step + n_pages]
    o_ref = refs[2 * pages_per_step + n_pages]
    scratch = refs[2 * pages_per_step + n_pages + 1:]
    _paged_attn_step(pl.program_id(0) * pl.num_programs(1) + pl.program_id(1),
                     q_ref, kn_ref, vn_ref, lfn_ref, k_refs, v_refs, lf_refs, o_ref, *scratch,
                     steps_per_seq=n_pages // pages_per_step, scale=q_ref.shape[2] ** -0.5)


def _attn_sample(q, k_new, v_new, lf_new, cache_k, cache_v, cache_lf, page_table, pages_per_step):
    batch, n_pages = page_table.shape
    _, page, n_heads, head_dim = cache_k.shape
    pw = page * n_heads
    steps_per_seq = n_pages // pages_per_step

    def row_spec(shape):
        return pl.BlockSpec((1,) + shape, lambda b, s, pt: (b, 0, 0))

    def page_spec(p):
        return pl.BlockSpec((1, page, n_heads, head_dim),
                            lambda b, s, pt: (pt[b, s * pages_per_step + p], 0, 0, 0))

    def lf_spec(p):
        return pl.BlockSpec((1, 1, pw), lambda b, s, pt: (pt[b, p], 0, 0))

    return pl.pallas_call(
        functools.partial(_attn_sample_kernel, pages_per_step=pages_per_step, n_pages=n_pages),
        grid_spec=pltpu.PrefetchScalarGridSpec(
            num_scalar_prefetch=1,
            grid=(batch, steps_per_seq),
            in_specs=[row_spec((n_heads, head_dim))] * 3 + [row_spec((n_heads, 1))]
                     + [page_spec(p) for p in range(pages_per_step)] * 2
                     + [lf_spec(p) for p in range(n_pages)],
            out_specs=row_spec((n_heads, head_dim)),
            scratch_shapes=[pltpu.VMEM((n_heads, pages_per_step * pw), F32),
                            pltpu.VMEM((n_pages, pw), F32),
                            pltpu.VMEM((n_heads, 1), F32),
                            pltpu.VMEM((n_heads, 1), F32),
                            pltpu.VMEM((n_heads, 1), F32),
                            pltpu.VMEM((n_heads, head_dim), F32)],
        ),
        out_shape=jax.ShapeDtypeStruct((batch, n_heads, head_dim), BF16),
        compiler_params=_params("arbitrary", "arbitrary"),
        name="attn_sample",
    )(page_table, q, k_new, v_new, lf_new,
      *([cache_k] * pages_per_step), *([cache_v] * pages_per_step), *([cache_lf] * n_pages))


def _rglru_gates(conv, r_pre, i_pre, b_rg, b_ig, neg_c_softplus):
    r = jax.nn.sigmoid(r_pre + b_rg)
    i_g = jax.nn.sigmoid(i_pre + b_ig)
    log_a = r * neg_c_softplus
    a = jnp.exp(log_a)
    b = jnp.sqrt(-jnp.tanh(log_a) * (1.0 + a * a)) * (i_g * conv)
    return a, b


def _rglru_prompt_kernel(xr_ref, yr_ref, cw_ref, cb_ref, wrg_ref, wig_ref, brg_ref, big_ref,
                         lam_ref, o_ref, hl_ref, xc_s, conv_s, rp_s, ip_s, a_s, b_s, h_s):
    c = pl.program_id(0)
    nb, tc, n_blocks, bw = xr_ref.shape
    halo = CONV_WIDTH - 1
    rows = nb * tc

    @pl.when(c == 0)
    def _():
        xc_s[:, 0:halo] = jnp.zeros((nb, halo, n_blocks, bw), F32)
        h_s[...] = jnp.zeros_like(h_s)

    @pl.when(c > 0)
    def _():
        xc_s[:, 0:halo] = xc_s[:, tc:tc + halo]

    xc_s[:, halo:halo + tc] = xr_ref[...]

    conv = cb_ref[...] + xc_s[:, 0:tc] * cw_ref[0]
    for i in range(1, CONV_WIDTH):
        conv = conv + xc_s[:, i:i + tc] * cw_ref[i]

    conv_s[...] = conv.reshape(rows * n_blocks, bw)
    for n in range(n_blocks):
        xn = conv_s[pl.ds(n, rows, stride=n_blocks), :].astype(BF16)
        rp_s[pl.ds(n, rows, stride=n_blocks), :] = _dot(xn, wrg_ref[n])
        ip_s[pl.ds(n, rows, stride=n_blocks), :] = _dot(xn, wig_ref[n])

    neg_c_softplus = -LRU_C * _softplus(-lam_ref[...])
    a, b = _rglru_gates(conv,
                        rp_s[...].reshape(nb, tc, n_blocks, bw),
                        ip_s[...].reshape(nb, tc, n_blocks, bw),
                        brg_ref[...], big_ref[...], neg_c_softplus)
    a_s[...] = a
    b_s[...] = b

    def step(t, hs):
        new = tuple(a_s[j, t] * hs[j] + b_s[j, t] for j in range(nb))
        for j in range(nb):
            b_s[j, t] = new[j]
        return new

    hs = lax.fori_loop(0, tc, step, tuple(h_s[j] for j in range(nb)), unroll=8)
    for j in range(nb):
        h_s[j] = hs[j]

    conv_s[...] = (b_s[...] * _gelu_tanh(yr_ref[...])).reshape(rows * n_blocks, bw)
    for n in range(n_blocks):
        o_ref[:, :, n * bw:(n + 1) * bw] = (
            conv_s[pl.ds(n, rows, stride=n_blocks), :].reshape(nb, tc, bw).astype(o_ref.dtype))
    hl_ref[...] = h_s[...]


def _rglru_prompt(xr, yr, w, batch, seq, tc):
    n_blocks, bw = w["n_rnn_blocks"], w["rnn_block"]
    xr4 = xr.reshape(batch, seq, n_blocks, bw)
    yr4 = yr.reshape(batch, seq, n_blocks, bw)
    rows = batch * tc
    halo = CONV_WIDTH - 1

    def const(shape):
        return pl.BlockSpec(shape, lambda c: (0,) * len(shape))

    seq_spec = pl.BlockSpec((batch, tc, n_blocks, bw), lambda c: (0, c, 0, 0))
    out, h_last = pl.pallas_call(
        _rglru_prompt_kernel,
        grid=(seq // tc,),
        in_specs=[seq_spec, seq_spec,
                  const((CONV_WIDTH, n_blocks, bw)), const((n_blocks, bw)),
                  const((n_blocks, bw, bw)), const((n_blocks, bw, bw)),
                  const((n_blocks, bw)), const((n_blocks, bw)), const((n_blocks, bw))],
        out_specs=[pl.BlockSpec((batch, tc, n_blocks * bw), lambda c: (0, c, 0)),
                   const((batch, n_blocks, bw))],
        out_shape=[jax.ShapeDtypeStruct((batch, seq, n_blocks * bw), BF16),
                   jax.ShapeDtypeStruct((batch, n_blocks, bw), F32)],
        scratch_shapes=[pltpu.VMEM((batch, tc + halo, n_blocks, bw), F32),
                        pltpu.VMEM((rows * n_blocks, bw), F32),
                        pltpu.VMEM((rows * n_blocks, bw), F32),
                        pltpu.VMEM((rows * n_blocks, bw), F32),
                        pltpu.VMEM((batch, tc, n_blocks, bw), F32),
                        pltpu.VMEM((batch, tc, n_blocks, bw), F32),
                        pltpu.VMEM((batch, n_blocks, bw), F32)],
        compiler_params=_params("arbitrary"),
        name="rglru_prompt",
    )(xr4, yr4, w["conv_w3"], w["conv_b2"], w["w_rg"], w["w_ig"], w["b_rg2"], w["b_ig2"],
      w["lam2"])
    return out.reshape(batch * seq, n_blocks * bw), h_last.reshape(batch, n_blocks * bw)


def _rglru_sample_kernel(xr_ref, yr_ref, buf_ref, h0_ref, cw_ref, cb_ref, wrg_ref, wig_ref,
                         brg_ref, big_ref, lam_ref, o_ref, h_ref, nbuf_ref):
    halo = CONV_WIDTH - 1
    n_blocks, bw = wrg_ref.shape[0], wrg_ref.shape[1]
    xr = xr_ref[...]
    conv = cb_ref[...] + xr * cw_ref[halo:halo + 1, :]
    for i in range(halo):
        conv = conv + buf_ref[i] * cw_ref[i:i + 1, :]
    r_parts, i_parts = [], []
    for n in range(n_blocks):
        xn = conv[:, n * bw:(n + 1) * bw].astype(BF16)
        r_parts.append(_dot(xn, wrg_ref[n]))
        i_parts.append(_dot(xn, wig_ref[n]))
    neg_c_softplus = -LRU_C * _softplus(-lam_ref[...])
    a, b = _rglru_gates(conv, jnp.concatenate(r_parts, axis=1), jnp.concatenate(i_parts, axis=1),
                        brg_ref[...], big_ref[...], neg_c_softplus)
    h = a * h0_ref[...] + b
    h_ref[...] = h
    o_ref[...] = h * _gelu_tanh(yr_ref[...])
    for i in range(halo - 1):
        nbuf_ref[i] = buf_ref[i + 1]
    nbuf_ref[halo - 1] = xr


def _rglru_sample(xr, yr, conv_buf_t, h0, w):
    rows, width = xr.shape
    halo = CONV_WIDTH - 1
    return pl.pallas_call(
        _rglru_sample_kernel,
        out_shape=[jax.ShapeDtypeStruct((rows, width), F32),
                   jax.ShapeDtypeStruct((rows, width), F32),
                   jax.ShapeDtypeStruct((halo, rows, width), F32)],
        compiler_params=pltpu.CompilerParams(vmem_limit_bytes=VMEM_LIMIT_BYTES),
        name="rglru_sample",
    )(xr, yr, conv_buf_t, h0, w["conv_w"], w["conv_b"], w["w_rg"], w["w_ig"],
      w["b_rg"], w["b_ig"], w["lam"])


def _out_proj_kernel(attn_ref, rnn_ref, ga_ref, gr_ref, x_ref, woa_ref, wor_ref, wout_ref,
                     gpm_ref, gpf_ref, h_o, hn_o, *, n_parts):
    part_rows = x_ref.shape[0] // n_parts
    for part in range(n_parts):
        r = slice(part * part_rows, (part + 1) * part_rows)
        o = (jax.nn.sigmoid(ga_ref[r, :].astype(F32))
             * _dot(attn_ref[r, :].astype(BF16), woa_ref[...])
             + jax.nn.sigmoid(gr_ref[r, :].astype(F32))
             * _dot(rnn_ref[r, :].astype(BF16), wor_ref[...]))
        mix = _dot(o.astype(BF16), wout_ref[...])
        h = x_ref[r, :] + _rms_norm(mix, gpm_ref[...])
        h_o[r, :] = h
        hn_o[r, :] = _rms_norm(h, gpf_ref[...]).astype(hn_o.dtype)


def _out_proj(attn, rnn, ga, gr, x, w, tm, n_parts=1):
    rows, d_model = x.shape

    def row(width):
        return pl.BlockSpec((tm, width), lambda i: (i, 0))

    def const(shape):
        return pl.BlockSpec(shape, lambda i: (0, 0), pipeline_mode=pl.Buffered(1))

    return pl.pallas_call(
        functools.partial(_out_proj_kernel, n_parts=n_parts),
        grid=(rows // tm,),
        in_specs=[row(attn.shape[1]), row(rnn.shape[1]), row(d_model), row(d_model), row(d_model),
                  const(w["o_attn"].shape), const(w["o_rnn"].shape), const(w["out"].shape),
                  const((1, d_model)), const((1, d_model))],
        out_specs=[row(d_model), row(d_model)],
        out_shape=[jax.ShapeDtypeStruct((rows, d_model), F32),
                   jax.ShapeDtypeStruct((rows, d_model), BF16)],
        compiler_params=_params("parallel"),
        name="out_proj",
    )(attn, rnn, ga, gr, x, w["o_attn"], w["o_rnn"], w["out"], w["g_post_mix"], w["g_pre_ffn"])


def _ffn_step(hn_ref, h_ref, wg_ref, wu_ref, wd_ref, g_ref, y_o, acc_s):
    j = pl.program_id(1)

    @pl.when(j == 0)
    def _():
        acc_s[...] = jnp.zeros_like(acc_s)

    hn = hn_ref[...]
    gate = _dot(hn, wg_ref[...])
    up = _dot(hn, wu_ref[...])
    acc_s[...] += _dot((jax.nn.silu(gate) * up).astype(BF16), wd_ref[...])

    @pl.when(j == pl.num_programs(1) - 1)
    def _():
        y_o[...] = h_ref[...] + _rms_norm(acc_s[...], g_ref[...])


def _ffn_specs(rows, d_model, d_ff, tm, tf):
    in_specs = [pl.BlockSpec((tm, d_model), lambda i, j, *_: (i, 0)),
                pl.BlockSpec((tm, d_model), lambda i, j, *_: (i, 0)),
                pl.BlockSpec((d_model, tf), lambda i, j, *_: (0, j)),
                pl.BlockSpec((d_model, tf), lambda i, j, *_: (0, j)),
                pl.BlockSpec((tf, d_model), lambda i, j, *_: (j, 0)),
                pl.BlockSpec((1, d_model), lambda i, j, *_: (0, 0))]
    out_spec = pl.BlockSpec((tm, d_model), lambda i, j, *_: (i, 0))
    out_shape = jax.ShapeDtypeStruct((rows, d_model), F32)
    return (rows // tm, d_ff // tf), in_specs, out_spec, out_shape


def _ffn(hn, h, w, tm, tf):
    rows, d_model = h.shape
    grid, in_specs, out_spec, out_shape = _ffn_specs(rows, d_model, w["gate"].shape[1], tm, tf)
    return pl.pallas_call(
        _ffn_step,
        grid=grid,
        in_specs=in_specs,
        out_specs=out_spec,
        out_shape=out_shape,
        scratch_shapes=[pltpu.VMEM((tm, d_model), F32)],
        compiler_params=_params("parallel", "arbitrary"),
        name="ffn",
    )(hn, h, w["gate"], w["up"], w["down"], w["g_post_ffn"])


def _layer_weights(l, n_heads, head_dim, g_pre_mix, w_in, b_f, conv_w, conv_b, w_rg, b_rg, w_ig,
                   b_ig, lru_lambda, w_o_attn, w_o_rnn, w_out, g_post_mix, g_pre_ffn, w_gate,
                   w_up, w_down, g_post_ffn):
    d_model = w_in.shape[1]
    n_blocks, bw = w_rg.shape[1], w_rg.shape[2]
    d_rnn = n_blocks * bw
    aw = n_heads * head_dim
    wi = w_in[l]
    f_lo, f_hi = 3 * aw, 3 * aw + n_heads
    cols = {
        "qkv": _cast_columns(wi, 0, f_lo, 512),
        "rest": _cast_columns(wi, f_hi, wi.shape[1] - f_hi, 512),
        "f": jnp.pad(wi[:, f_lo:f_hi].astype(BF16), ((0, 0), (0, LANES - n_heads))),
        "b_f": jnp.pad(b_f[l].reshape(1, n_heads), ((0, 0), (0, LANES - n_heads))),
        "widths": {"q": aw, "k": aw, "v": aw, "xr": d_rnn, "yr": d_rnn,
                   "ga": d_model, "gr": d_model},
        "n_heads": n_heads,
        "head_dim": head_dim,
    }
    rnn = {
        "n_rnn_blocks": n_blocks, "rnn_block": bw,
        "conv_w": conv_w[l], "conv_b": conv_b[l].reshape(1, d_rnn),
        "b_rg": b_rg[l].reshape(1, d_rnn), "b_ig": b_ig[l].reshape(1, d_rnn),
        "lam": lru_lambda[l].reshape(1, d_rnn),
        "conv_w3": conv_w[l].reshape(CONV_WIDTH, n_blocks, bw),
        "conv_b2": conv_b[l].reshape(n_blocks, bw),
        "b_rg2": b_rg[l].reshape(n_blocks, bw), "b_ig2": b_ig[l].reshape(n_blocks, bw),
        "lam2": lru_lambda[l].reshape(n_blocks, bw),
        "w_rg": w_rg[l].astype(BF16), "w_ig": w_ig[l].astype(BF16),
    }
    mix = {"o_attn": w_o_attn[l].astype(BF16), "o_rnn": w_o_rnn[l].astype(BF16),
           "out": w_out[l].astype(BF16),
           "g_post_mix": g_post_mix[l].reshape(1, d_model),
           "g_pre_ffn": g_pre_ffn[l].reshape(1, d_model)}
    ffn = {"gate": w_gate[l].astype(BF16), "up": w_up[l].astype(BF16),
           "down": w_down[l].astype(BF16), "g_post_ffn": g_post_ffn[l].reshape(1, d_model)}
    return g_pre_mix[l].reshape(1, d_model), cols, rnn, mix, ffn


def kernel(x_prompt, x_sample, cache_k, cache_v, cache_logf, state_h, state_conv, page_table,
           g_pre_mix, w_in, b_f, conv_w, conv_b, w_rg, b_rg, w_ig, b_ig, lru_lambda,
           w_o_attn, w_o_rnn, w_out, g_post_mix, g_pre_ffn, w_gate, w_up, w_down, g_post_ffn):
    batch, seq, d_model = x_prompt.shape
    dec_batch, dec_seq, _ = x_sample.shape
    assert dec_seq == 1, "sample group carries one new token per sequence"
    depth, n_phys, page, n_heads, head_dim = cache_k.shape
    aw = n_heads * head_dim
    d_rnn = state_h.shape[2]
    n_pages = page_table.shape[1]
    assert LANES % n_heads == 0 and (page * n_heads) % LANES == 0

    xp = x_prompt.reshape(batch * seq, d_model)
    xs = x_sample.reshape(dec_batch, d_model)
    outs = {name: [] for name in ("kp", "vp", "lp", "hp", "cp", "ks", "vs", "ls", "hs", "cs")}

    for l in range(depth):
        g_in, w_cols, w_rnn, w_mix, w_ffn = _layer_weights(
            l, n_heads, head_dim, g_pre_mix, w_in, b_f, conv_w, conv_b, w_rg, b_rg, w_ig, b_ig,
            lru_lambda, w_o_attn, w_o_rnn, w_out, g_post_mix, g_pre_ffn, w_gate, w_up, w_down,
            g_post_ffn)

        q, k16, v16, k, v, xr, yr, ga, gr, lf, c = _in_proj(xp, g_in, w_cols, tm=512,
                                                            n_col_steps=4, seq=seq)
        qs, _, _, ks, vs, xrs, yrs, gas, grs, lfs = _in_proj(xs, g_in, w_cols, tm=dec_batch,
                                                             n_col_steps=4)
        ks = ks.reshape(dec_batch, n_heads, head_dim)
        vs = vs.reshape(dec_batch, n_heads, head_dim)

        attn = _attn_prompt(q, k16, v16, c, batch, seq, n_heads, head_dim, tq=256)
        rnn, h_last = _rglru_prompt(xr, yr, w_rnn, batch, seq, tc=128)
        h, hn = _out_proj(attn, rnn, ga, gr, xp, w_mix, tm=512, n_parts=2)
        xp_next = _ffn(hn, h, w_ffn, tm=512, tf=512)
        outs["kp"].append(k.reshape(batch, seq, n_heads, head_dim))
        outs["vp"].append(v.reshape(batch, seq, n_heads, head_dim))
        outs["lp"].append(lf.reshape(batch, seq, n_heads))
        outs["hp"].append(h_last)
        outs["cp"].append(xr.reshape(batch, seq, d_rnn)[:, seq - (CONV_WIDTH - 1):])

        attn_s = _attn_sample(
            qs.reshape(dec_batch, n_heads, head_dim), ks, vs, lfs.reshape(dec_batch, n_heads, 1),
            cache_k[l], cache_v[l], cache_logf[l].reshape(n_phys, 1, page * n_heads),
            page_table, pages_per_step=n_pages)
        rnn, h_new, buf_new = _rglru_sample(xrs, yrs, jnp.swapaxes(state_conv[l], 0, 1),
                                            state_h[l], w_rnn)
        h, hn = _out_proj(attn_s.reshape(dec_batch, aw), rnn, gas, grs, xs, w_mix, tm=dec_batch)
        xs_next = _ffn(hn, h, w_ffn, tm=dec_batch, tf=512)
        outs["ks"].append(ks.reshape(dec_batch, 1, n_heads, head_dim))
        outs["vs"].append(vs.reshape(dec_batch, 1, n_heads, head_dim))
        outs["ls"].append(lfs.reshape(dec_batch, 1, n_heads))
        outs["hs"].append(h_new)
        outs["cs"].append(jnp.swapaxes(buf_new, 0, 1))

        xp, xs = xp_next, xs_next

    st = {name: jnp.stack(vals) for name, vals in outs.items()}
    return (xp.reshape(batch, seq, d_model), xs.reshape(dec_batch, 1, d_model),
            st["kp"], st["vp"], st["lp"], st["hp"], st["cp"],
            st["ks"], st["vs"], st["ls"], st["hs"], st["cs"])
```

```python
import functools
import math

import jax
import jax.numpy as jnp
from jax import lax
from jax.experimental import pallas as pl
from jax.experimental.pallas import tpu as pltpu

F32 = jnp.float32
BF16 = jnp.bfloat16

EPS = 1e-6
LRU_C = 8.0
CONV_WIDTH = 4
NEG_BIG = -1e30

SUBLANES = 8
LANES = 128
VMEM_LIMIT_BYTES = 56 * 1024 * 1024


def _params(*semantics):
    return pltpu.CompilerParams(dimension_semantics=semantics,
                                vmem_limit_bytes=VMEM_LIMIT_BYTES)


def _rms_norm(x, g):
    inv = lax.rsqrt(jnp.mean(x * x, axis=-1, keepdims=True) + EPS)
    return (x * inv) * g


def _log_sigmoid(x):
    return jnp.minimum(x, 0.0) - jnp.log1p(jnp.exp(-jnp.abs(x)))


def _softplus(x):
    return jnp.maximum(x, 0.0) + jnp.log1p(jnp.exp(-jnp.abs(x)))


def _gelu_tanh(x):
    c = math.sqrt(2.0 / math.pi)
    return x * (0.5 * (1.0 + jnp.tanh(c * (x + 0.044715 * (x * x * x)))))


def _dot(a, b):
    return jnp.dot(a, b, preferred_element_type=F32)


def _dot_nt(a, b):
    return lax.dot_general(a, b, (((1,), (1,)), ((), ())), preferred_element_type=F32)


def _cast_kernel(x_ref, o_ref):
    o_ref[...] = x_ref[...].astype(o_ref.dtype)


def _cast_shifted_kernel(a_ref, b_ref, o_ref, *, shift):
    o_ref[...] = jnp.concatenate([a_ref[shift:, :], b_ref[:shift, :]], axis=0).astype(o_ref.dtype)


def _cast_rows(w_t, first_row, n_rows, tile):
    width = w_t.shape[1]
    assert n_rows % tile == 0 and tile % SUBLANES == 0
    shift = first_row % tile
    base = first_row // tile
    out_spec = pl.BlockSpec((tile, width), lambda j: (j, 0))
    out_shape = jax.ShapeDtypeStruct((n_rows, width), BF16)
    if shift == 0:
        return pl.pallas_call(
            _cast_kernel, grid=(n_rows // tile,),
            in_specs=[pl.BlockSpec((tile, width), lambda j: (base + j, 0))],
            out_specs=out_spec, out_shape=out_shape,
            compiler_params=_params("parallel"), name="cast",
        )(w_t)
    assert shift == SUBLANES and first_row + n_rows <= w_t.shape[0]
    per_tile = tile // SUBLANES
    return pl.pallas_call(
        functools.partial(_cast_shifted_kernel, shift=shift), grid=(n_rows // tile,),
        in_specs=[pl.BlockSpec((tile, width), lambda j: (base + j, 0)),
                  pl.BlockSpec((SUBLANES, width), lambda j: ((base + j + 1) * per_tile, 0))],
        out_specs=out_spec, out_shape=out_shape,
        compiler_params=_params("parallel"), name="cast_shifted",
    )(w_t, w_t)


def _prefix_sum(x, axis, start=1):
    idx = lax.broadcasted_iota(jnp.int32, x.shape, axis)
    shift = start
    while shift < x.shape[axis]:
        x = x + jnp.where(idx >= shift, pltpu.roll(x, shift, axis=axis), 0.0)
        shift *= 2
    return x


def _in_proj_kernel(x_ref, g_ref, wq, wk, wv, wxr, wyr, wga, wgr, wf, bf_ref,
                    q_o, k16_o, v16_o, k32_o, v32_o, xr_o, yr_o, ga_o, gr_o, lf_o, *rest,
                    tiles_per_seq, inv_scale):
    i, j = pl.program_id(0), pl.program_id(1)
    xn_s = rest[-1]
    n_heads = lf_o.shape[1]

    @pl.when(j == 0)
    def _():
        xn = _rms_norm(x_ref[...], g_ref[...]).astype(BF16)
        xn_s[...] = xn
        lf = _log_sigmoid(_dot_nt(xn, wf[...]) + bf_ref[...])
        lf_o[...] = lf[:, :n_heads]
        if tiles_per_seq is not None:
            c_o, carry_s = rest[0], rest[1]

            @pl.when(lax.rem(i, tiles_per_seq) == 0)
            def _():
                carry_s[...] = jnp.zeros_like(carry_s)

            c = _prefix_sum(lf, axis=0) + carry_s[...]
            carry_s[...] = c[c.shape[0] - 1:, :]
            hi, mid, lo = _split3_bf16(c * inv_scale)
            lane = lax.broadcasted_iota(jnp.int32, c.shape, 1)
            packed = jnp.where(
                lane < n_heads, hi,
                jnp.where(lane < 2 * n_heads, pltpu.roll(mid, n_heads, axis=1),
                          jnp.where(lane < 3 * n_heads, pltpu.roll(lo, 2 * n_heads, axis=1),
                                    jnp.where(lane == 3 * n_heads, 1.0, 0.0))))
            c_o[...] = packed.astype(c_o.dtype)

    xn = xn_s[...]
    for w, o in ((wq, q_o), (wxr, xr_o), (wyr, yr_o), (wga, ga_o), (wgr, gr_o)):
        o[...] = _dot_nt(xn, w[...]).astype(o.dtype)

    tm = xn.shape[0]
    n_heads = k32_o.shape[0] // tm
    head_dim = k32_o.shape[1]
    heads_per_step = wk.shape[0] // head_dim
    for w, o16, o32 in ((wk, k16_o, k32_o), (wv, v16_o, v32_o)):
        res = _dot_nt(xn, w[...])
        o16[...] = res.astype(o16.dtype)
        for hh in range(heads_per_step):
            head = j * heads_per_step + hh
            o32[pl.ds(head, tm, stride=n_heads), :] = res[:, hh * head_dim:(hh + 1) * head_dim]


def _in_proj(x, g, w, tm, n_col_steps, seq=None):
    rows, d_model = x.shape
    n_heads, head_dim = w["n_heads"], w["head_dim"]
    cols = {}
    for arr, names in (("qkv", ("q", "k", "v")), ("rest", ("xr", "yr", "ga", "gr"))):
        start = 0
        for name in names:
            cols[name] = (arr, start, w["widths"][name])
            start += w["widths"][name]
    order = ("q", "k", "v", "xr", "yr", "ga", "gr")
    tiles = {name: cols[name][2] // n_col_steps for name in order}
    grid = (rows // tm, n_col_steps)

    def w_spec(name):
        first_block = cols[name][1] // tiles[name]
        return pl.BlockSpec((tiles[name], d_model), lambda i, j: (first_block + j, 0))

    def o_spec(name):
        return pl.BlockSpec((tm, tiles[name]), lambda i, j: (i, j))

    def o_shape(name, dtype):
        return jax.ShapeDtypeStruct((rows, cols[name][2]), dtype)

    def per_head():
        return pl.BlockSpec((tm, n_heads), lambda i, j: (i, 0))

    flat_spec = pl.BlockSpec((tm * n_heads, head_dim), lambda i, j: (i, 0))
    flat_shape = jax.ShapeDtypeStruct((rows * n_heads, head_dim), F32)
    out_specs = [o_spec("q"), o_spec("k"), o_spec("v"), flat_spec, flat_spec,
                 o_spec("xr"), o_spec("yr"), o_spec("ga"), o_spec("gr"), per_head()]
    out_shape = [o_shape("q", BF16), o_shape("k", BF16), o_shape("v", BF16),
                 flat_shape, flat_shape,
                 o_shape("xr", F32), o_shape("yr", F32), o_shape("ga", BF16), o_shape("gr", BF16),
                 jax.ShapeDtypeStruct((rows, n_heads), F32)]
    scratch = [pltpu.VMEM((tm, d_model), BF16)]
    if seq is not None:
        out_specs.append(pl.BlockSpec((tm, LANES), lambda i, j: (i, 0)))
        out_shape.append(jax.ShapeDtypeStruct((rows, LANES), BF16))
        scratch.insert(0, pltpu.VMEM((1, LANES), F32))
    return pl.pallas_call(
        functools.partial(_in_proj_kernel,
                          tiles_per_seq=None if seq is None else seq // tm,
                          inv_scale=head_dim ** 0.5),
        grid=grid,
        in_specs=[pl.BlockSpec((tm, d_model), lambda i, j: (i, 0)),
                  pl.BlockSpec((1, d_model), lambda i, j: (0, 0))]
                 + [w_spec(name) for name in order]
                 + [pl.BlockSpec((LANES, d_model), lambda i, j: (0, 0)),
                    pl.BlockSpec((1, LANES), lambda i, j: (0, 0))],
        out_specs=out_specs,
        out_shape=out_shape,
        scratch_shapes=scratch,
        compiler_params=_params("arbitrary", "arbitrary"),
        name="in_proj",
    )(x, g, *[w[cols[name][0]] for name in order], w["f"], w["b_f"])


def _split3_bf16(x):
    hi = x.astype(BF16)
    r = x - hi.astype(F32)
    mid = r.astype(BF16)
    lo = (r - mid.astype(F32)).astype(BF16)
    return hi.astype(F32), mid.astype(F32), lo.astype(F32)


def _attn_prompt_kernel(q_ref, k_ref, v_ref, c_ref, o_ref, qa_s, ka_s, s_s, *,
                        tq, scale, n_heads):
    seq, head_dim = q_ref.shape
    h = pl.program_id(1)
    src = lax.broadcasted_iota(jnp.int32, (LANES, LANES), 0)
    dst = lax.broadcasted_iota(jnp.int32, (LANES, LANES), 1)
    one_src = src == 3 * n_heads
    part_of_dst = jnp.where(dst < 3, dst, dst - 3) * n_heads + h
    sel_q = jnp.where((dst < 3) & (src == part_of_dst), 1.0,
                      jnp.where((dst >= 3) & (dst < 6) & one_src, 1.0, 0.0))
    sel_k = jnp.where((dst >= 3) & (dst < 6) & (src == part_of_dst), -1.0,
                      jnp.where((dst < 3) & one_src, 1.0, 0.0))
    c_rows = c_ref[...]
    qa_s[:, :head_dim] = q_ref[...]
    qa_s[:, head_dim:] = _dot(c_rows, sel_q.astype(BF16)).astype(BF16)
    ka_s[:, :head_dim] = k_ref[...]
    ka_s[:, head_dim:] = _dot(c_rows, sel_k.astype(BF16)).astype(BF16)

    sigma = scale * math.log2(math.e)
    row = lax.broadcasted_iota(jnp.int32, (tq, tq), 0)
    col = lax.broadcasted_iota(jnp.int32, (tq, tq), 1)
    visible = col <= row
    groups = tq // LANES
    for i in range(seq // tq):
        qa = qa_s[i * tq:(i + 1) * tq, :]
        m_part = jnp.full((tq, LANES), NEG_BIG, F32)
        for j in range(i + 1):
            s = _dot_nt(qa, ka_s[j * tq:(j + 1) * tq, :]) * sigma
            if j == i:
                s = jnp.where(visible, s, NEG_BIG)
            s_s[:, j * tq:(j + 1) * tq] = s
            for g in range(groups):
                m_part = jnp.maximum(m_part, s[:, g * LANES:(g + 1) * LANES])
        m = jnp.broadcast_to(jnp.max(m_part, axis=-1, keepdims=True), (tq, LANES))
        l_part = jnp.zeros((tq, LANES), F32)
        acc = jnp.zeros((tq, head_dim), F32)
        for j in range(i + 1):
            parts = []
            for g in range(groups):
                off = j * tq + g * LANES
                p = jnp.exp2(s_s[:, off:off + LANES] - m)
                l_part = l_part + p
                parts.append(p.astype(BF16))
            acc = acc + _dot(jnp.concatenate(parts, axis=1), v_ref[j * tq:(j + 1) * tq, :])
        l = jnp.sum(l_part, axis=-1, keepdims=True)
        o_ref[i * tq:(i + 1) * tq, :] = (acc / l).astype(o_ref.dtype)


def _attn_prompt(q, k, v, c, batch, seq, n_heads, head_dim, tq):
    head_spec = pl.BlockSpec((seq, head_dim), lambda b, h: (b, h))
    return pl.pallas_call(
        functools.partial(_attn_prompt_kernel, tq=tq, scale=head_dim ** -0.5, n_heads=n_heads),
        grid=(batch, n_heads),
        in_specs=[head_spec, head_spec, head_spec,
                  pl.BlockSpec((seq, LANES), lambda b, h: (b, 0))],
        out_specs=head_spec,
        out_shape=jax.ShapeDtypeStruct(q.shape, BF16),
        scratch_shapes=[pltpu.VMEM((seq, head_dim + LANES), BF16),
                        pltpu.VMEM((seq, head_dim + LANES), BF16),
                        pltpu.VMEM((tq, seq), F32)],
        compiler_params=_params("parallel", "parallel"),
        name="attn_prompt",
    )(q, k, v, c)


def _paged_logf_sums(lf_pages, n_heads):
    n_pages, pw = lf_pages.shape
    within = _prefix_sum(lf_pages, axis=1, start=n_heads)
    lane = lax.broadcasted_iota(jnp.int32, (n_pages, LANES), 1)
    totals = jnp.where(lane >= LANES - n_heads, within[:, pw - LANES:], 0.0)
    shift = n_heads
    while shift < LANES:
        totals = totals + pltpu.roll(totals, LANES - shift, axis=1)
        shift *= 2
    upto = _prefix_sum(totals, axis=0)
    before = upto - totals
    sums = within + jnp.concatenate([before] * (pw // LANES), axis=1)
    return sums, upto[n_pages - 1:, :]


def _paged_attn_step(step, q_ref, kn_ref, vn_ref, lfn_ref, k_refs, v_refs, lf_refs, o_ref,
                     s_s, c_s, cnew_s, m_s, l_s, acc_s, *, steps_per_seq, scale):
    pps = len(k_refs)
    page, n_heads, head_dim = k_refs[0].shape[1:]
    pw = page * n_heads
    part = lax.rem(step, steps_per_seq)

    head_of_row = lax.broadcasted_iota(jnp.int32, (n_heads, pw), 0)
    lane = lax.broadcasted_iota(jnp.int32, (n_heads, pw), 1)
    on_diag = head_of_row == lane % n_heads
    q = q_ref[0]

    @pl.when(part == 0)
    def _():
        c_past, c_total = _paged_logf_sums(
            jnp.concatenate([r[0] for r in lf_refs], axis=0), n_heads)
        c_s[...] = c_past
        own_lane = lane[:, :LANES] == head_of_row[:, :LANES]
        c_last = jnp.sum(jnp.where(own_lane, c_total, 0.0), axis=-1, keepdims=True)
        cnew_s[...] = c_last + lfn_ref[0]
        kn = kn_ref[0].astype(BF16).astype(F32)
        m_s[...] = jnp.sum(q.astype(F32) * kn, axis=-1, keepdims=True) * scale
        l_s[...] = jnp.ones_like(l_s)
        acc_s[...] = vn_ref[0].astype(BF16).astype(F32)

    c_new = cnew_s[...]
    for p in range(pps):
        kf = k_refs[p][0].reshape(pw, head_dim).astype(BF16)
        c_row = c_s[pl.ds(part * pps + p, 1), :]
        s = _dot_nt(q, kf) * scale + (c_new - c_row)
        s_s[:, p * pw:(p + 1) * pw] = jnp.where(on_diag, s, NEG_BIG)

    s_all = s_s[...]
    m_old = m_s[...]
    m_new = jnp.maximum(m_old, jnp.max(s_all, axis=-1, keepdims=True))
    p_all = jnp.exp(s_all - m_new)
    alpha = jnp.exp(m_old - m_new)
    l_s[...] = alpha * l_s[...] + jnp.sum(p_all, axis=-1, keepdims=True)
    m_s[...] = m_new
    p16 = p_all.astype(BF16)
    acc = alpha * acc_s[...]
    for p in range(pps):
        vf = v_refs[p][0].reshape(pw, head_dim).astype(BF16)
        acc = acc + _dot(p16[:, p * pw:(p + 1) * pw], vf)
    acc_s[...] = acc

    @pl.when(part == steps_per_seq - 1)
    def _():
        o_ref[0] = (acc_s[...] / l_s[...]).astype(o_ref.dtype)


def _attn_sample_kernel(pt_ref, q_ref, kn_ref, vn_ref, lfn_ref, *refs, pages_per_step, n_pages):
    del pt_ref
    k_refs = refs[:pages_per_step]
    v_refs = refs[pages_per_step:2 * pages_per_step]
    lf_refs = refs[2 * pages_per_step:2 * pages_per_step + n_pages]
    o_ref = refs[2 * pages_per_step + n_pages]
    scratch = refs[2 * pages_per_step + n_pages + 1:]
    _paged_attn_step(pl.program_id(0) * pl.num_programs(1) + pl.program_id(1),
                     q_ref, kn_ref, vn_ref, lfn_ref, k_refs, v_refs, lf_refs, o_ref, *scratch,
                     steps_per_seq=n_pages // pages_per_step, scale=q_ref.shape[2] ** -0.5)


def _attn_sample(q, k_new, v_new, lf_new, cache_k, cache_v, cache_lf, page_table, pages_per_step):
    batch, n_pages = page_table.shape
    _, page, n_heads, head_dim = cache_k.shape
    pw = page * n_heads
    steps_per_seq = n_pages // pages_per_step

    def row_spec(shape):
        return pl.BlockSpec((1,) + shape, lambda b, s, pt: (b, 0, 0))

    def page_spec(p):
        return pl.BlockSpec((1, page, n_heads, head_dim),
                            lambda b, s, pt: (pt[b, s * pages_per_step + p], 0, 0, 0))

    def lf_spec(p):
        return pl.BlockSpec((1, 1, pw), lambda b, s, pt: (pt[b, p], 0, 0))

    return pl.pallas_call(
        functools.partial(_attn_sample_kernel, pages_per_step=pages_per_step, n_pages=n_pages),
        grid_spec=pltpu.PrefetchScalarGridSpec(
            num_scalar_prefetch=1,
            grid=(batch, steps_per_seq),
            in_specs=[row_spec((n_heads, head_dim))] * 3 + [row_spec((n_heads, 1))]
                     + [page_spec(p) for p in range(pages_per_step)] * 2
                     + [lf_spec(p) for p in range(n_pages)],
            out_specs=row_spec((n_heads, head_dim)),
            scratch_shapes=[pltpu.VMEM((n_heads, pages_per_step * pw), F32),
                            pltpu.VMEM((n_pages, pw), F32),
                            pltpu.VMEM((n_heads, 1), F32),
                            pltpu.VMEM((n_heads, 1), F32),
                            pltpu.VMEM((n_heads, 1), F32),
                            pltpu.VMEM((n_heads, head_dim), F32)],
        ),
        out_shape=jax.ShapeDtypeStruct((batch, n_heads, head_dim), BF16),
        compiler_params=_params("arbitrary", "arbitrary"),
        name="attn_sample",
    )(page_table, q, k_new, v_new, lf_new,
      *([cache_k] * pages_per_step), *([cache_v] * pages_per_step), *([cache_lf] * n_pages))


def _rglru_gates(conv, r_pre, i_pre, b_rg, b_ig, neg_c_softplus):
    r = jax.nn.sigmoid(r_pre + b_rg)
    i_g = jax.nn.sigmoid(i_pre + b_ig)
    log_a = r * neg_c_softplus
    a = jnp.exp(log_a)
    b = jnp.sqrt(-jnp.tanh(log_a) * (1.0 + a * a)) * (i_g * conv)
    return a, b


def _rglru_prompt_kernel(xr_ref, yr_ref, cw_ref, cb_ref, wrg_ref, wig_ref, brg_ref, big_ref,
                         lam_ref, o_ref, hl_ref, xc_s, conv_s, rp_s, ip_s, a_s, b_s, h_s):
    c = pl.program_id(0)
    nb, tc, n_blocks, bw = xr_ref.shape
    halo = CONV_WIDTH - 1
    rows = nb * tc

    @pl.when(c == 0)
    def _():
        xc_s[:, 0:halo] = jnp.zeros((nb, halo, n_blocks, bw), F32)
        h_s[...] = jnp.zeros_like(h_s)

    @pl.when(c > 0)
    def _():
        xc_s[:, 0:halo] = xc_s[:, tc:tc + halo]

    xc_s[:, halo:halo + tc] = xr_ref[...]

    conv = cb_ref[...] + xc_s[:, 0:tc] * cw_ref[0]
    for i in range(1, CONV_WIDTH):
        conv = conv + xc_s[:, i:i + tc] * cw_ref[i]

    conv_s[...] = conv.reshape(rows * n_blocks, bw)
    for n in range(n_blocks):
        xn = conv_s[pl.ds(n, rows, stride=n_blocks), :].astype(BF16)
        rp_s[pl.ds(n, rows, stride=n_blocks), :] = _dot(xn, wrg_ref[n])
        ip_s[pl.ds(n, rows, stride=n_blocks), :] = _dot(xn, wig_ref[n])

    neg_c_softplus = -LRU_C * _softplus(-lam_ref[...])
    a, b = _rglru_gates(conv,
                        rp_s[...].reshape(nb, tc, n_blocks, bw),
                        ip_s[...].reshape(nb, tc, n_blocks, bw),
                        brg_ref[...], big_ref[...], neg_c_softplus)
    a_s[...] = a
    b_s[...] = b

    def step(t, hs):
        new = tuple(a_s[j, t] * hs[j] + b_s[j, t] for j in range(nb))
        for j in range(nb):
            b_s[j, t] = new[j]
        return new

    hs = lax.fori_loop(0, tc, step, tuple(h_s[j] for j in range(nb)), unroll=8)
    for j in range(nb):
        h_s[j] = hs[j]

    conv_s[...] = (b_s[...] * _gelu_tanh(yr_ref[...])).reshape(rows * n_blocks, bw)
    for n in range(n_blocks):
        o_ref[:, :, n * bw:(n + 1) * bw] = (
            conv_s[pl.ds(n, rows, stride=n_blocks), :].reshape(nb, tc, bw).astype(o_ref.dtype))
    hl_ref[...] = h_s[...]


def _rglru_prompt(xr, yr, w, batch, seq, tc):
    n_blocks, bw = w["n_rnn_blocks"], w["rnn_block"]
    xr4 = xr.reshape(batch, seq, n_blocks, bw)
    yr4 = yr.reshape(batch, seq, n_blocks, bw)
    rows = batch * tc
    halo = CONV_WIDTH - 1

    def const(shape):
        return pl.BlockSpec(shape, lambda c: (0,) * len(shape))

    seq_spec = pl.BlockSpec((batch, tc, n_blocks, bw), lambda c: (0, c, 0, 0))
    out, h_last = pl.pallas_call(
        _rglru_prompt_kernel,
        grid=(seq // tc,),
        in_specs=[seq_spec, seq_spec,
                  const((CONV_WIDTH, n_blocks, bw)), const((n_blocks, bw)),
                  const((n_blocks, bw, bw)), const((n_blocks, bw, bw)),
                  const((n_blocks, bw)), const((n_blocks, bw)), const((n_blocks, bw))],
        out_specs=[pl.BlockSpec((batch, tc, n_blocks * bw), lambda c: (0, c, 0)),
                   const((batch, n_blocks, bw))],
        out_shape=[jax.ShapeDtypeStruct((batch, seq, n_blocks * bw), BF16),
                   jax.ShapeDtypeStruct((batch, n_blocks, bw), F32)],
        scratch_shapes=[pltpu.VMEM((batch, tc + halo, n_blocks, bw), F32),
                        pltpu.VMEM((rows * n_blocks, bw), F32),
                        pltpu.VMEM((rows * n_blocks, bw), F32),
                        pltpu.VMEM((rows * n_blocks, bw), F32),
                        pltpu.VMEM((batch, tc, n_blocks, bw), F32),
                        pltpu.VMEM((batch, tc, n_blocks, bw), F32),
                        pltpu.VMEM((batch, n_blocks, bw), F32)],
        compiler_params=_params("arbitrary"),
        name="rglru_prompt",
    )(xr4, yr4, w["conv_w3"], w["conv_b2"], w["w_rg"], w["w_ig"], w["b_rg2"], w["b_ig2"],
      w["lam2"])
    return out.reshape(batch * seq, n_blocks * bw), h_last.reshape(batch, n_blocks * bw)


def _rglru_sample_kernel(xr_ref, yr_ref, buf_ref, h0_ref, cw_ref, cb_ref, wrg_ref, wig_ref,
                         brg_ref, big_ref, lam_ref, o_ref, h_ref, nbuf_ref):
    halo = CONV_WIDTH - 1
    n_blocks, bw = wrg_ref.shape[0], wrg_ref.shape[1]
    xr = xr_ref[...]
    conv = cb_ref[...] + xr * cw_ref[halo:halo + 1, :]
    for i in range(halo):
        conv = conv + buf_ref[i] * cw_ref[i:i + 1, :]
    r_parts, i_parts = [], []
    for n in range(n_blocks):
        xn = conv[:, n * bw:(n + 1) * bw].astype(BF16)
        r_parts.append(_dot(xn, wrg_ref[n]))
        i_parts.append(_dot(xn, wig_ref[n]))
    neg_c_softplus = -LRU_C * _softplus(-lam_ref[...])
    a, b = _rglru_gates(conv, jnp.concatenate(r_parts, axis=1), jnp.concatenate(i_parts, axis=1),
                        brg_ref[...], big_ref[...], neg_c_softplus)
    h = a * h0_ref[...] + b
    h_ref[...] = h
    o_ref[...] = h * _gelu_tanh(yr_ref[...])
    for i in range(halo - 1):
        nbuf_ref[i] = buf_ref[i + 1]
    nbuf_ref[halo - 1] = xr


def _rglru_sample(xr, yr, conv_buf_t, h0, w):
    rows, width = xr.shape
    halo = CONV_WIDTH - 1
    return pl.pallas_call(
        _rglru_sample_kernel,
        out_shape=[jax.ShapeDtypeStruct((rows, width), F32),
                   jax.ShapeDtypeStruct((rows, width), F32),
                   jax.ShapeDtypeStruct((halo, rows, width), F32)],
        compiler_params=pltpu.CompilerParams(vmem_limit_bytes=VMEM_LIMIT_BYTES),
        name="rglru_sample",
    )(xr, yr, conv_buf_t, h0, w["conv_w"], w["conv_b"], w["w_rg"], w["w_ig"],
      w["b_rg"], w["b_ig"], w["lam"])


def _out_proj_kernel(attn_ref, rnn_ref, ga_ref, gr_ref, x_ref, woa_ref, wor_ref, wout_ref,
                     gpm_ref, gpf_ref, h_o, hn_o, *, n_parts):
    part_rows = x_ref.shape[0] // n_parts
    for part in range(n_parts):
        r = slice(part * part_rows, (part + 1) * part_rows)
        o = (jax.nn.sigmoid(ga_ref[r, :].astype(F32))
             * _dot(attn_ref[r, :].astype(BF16), woa_ref[...])
             + jax.nn.sigmoid(gr_ref[r, :].astype(F32))
             * _dot(rnn_ref[r, :].astype(BF16), wor_ref[...]))
        mix = _dot(o.astype(BF16), wout_ref[...])
        h = x_ref[r, :] + _rms_norm(mix, gpm_ref[...])
        h_o[r, :] = h
        hn_o[r, :] = _rms_norm(h, gpf_ref[...]).astype(hn_o.dtype)


def _out_proj(attn, rnn, ga, gr, x, w, tm, n_parts=1):
    rows, d_model = x.shape

    def row(width):
        return pl.BlockSpec((tm, width), lambda i: (i, 0))

    def const(shape):
        return pl.BlockSpec(shape, lambda i: (0, 0), pipeline_mode=pl.Buffered(1))

    return pl.pallas_call(
        functools.partial(_out_proj_kernel, n_parts=n_parts),
        grid=(rows // tm,),
        in_specs=[row(attn.shape[1]), row(rnn.shape[1]), row(d_model), row(d_model), row(d_model),
                  const(w["o_attn"].shape), const(w["o_rnn"].shape), const(w["out"].shape),
                  const((1, d_model)), const((1, d_model))],
        out_specs=[row(d_model), row(d_model)],
        out_shape=[jax.ShapeDtypeStruct((rows, d_model), F32),
                   jax.ShapeDtypeStruct((rows, d_model), BF16)],
        compiler_params=_params("parallel"),
        name="out_proj",
    )(attn, rnn, ga, gr, x, w["o_attn"], w["o_rnn"], w["out"], w["g_post_mix"], w["g_pre_ffn"])


def _ffn_step(hn_ref, h_ref, wg_ref, wu_ref, wd_ref, g_ref, y_o, acc_s):
    j = pl.program_id(1)

    @pl.when(j == 0)
    def _():
        acc_s[...] = jnp.zeros_like(acc_s)

    hn = hn_ref[...]
    gate = _dot(hn, wg_ref[...])
    up = _dot(hn, wu_ref[...])
    acc_s[...] += _dot((jax.nn.silu(gate) * up).astype(BF16), wd_ref[...])

    @pl.when(j == pl.num_programs(1) - 1)
    def _():
        y_o[...] = h_ref[...] + _rms_norm(acc_s[...], g_ref[...])


def _ffn_specs(rows, d_model, d_ff, tm, tf):
    in_specs = [pl.BlockSpec((tm, d_model), lambda i, j, *_: (i, 0)),
                pl.BlockSpec((tm, d_model), lambda i, j, *_: (i, 0)),
                pl.BlockSpec((d_model, tf), lambda i, j, *_: (0, j)),
                pl.BlockSpec((d_model, tf), lambda i, j, *_: (0, j)),
                pl.BlockSpec((tf, d_model), lambda i, j, *_: (j, 0)),
                pl.BlockSpec((1, d_model), lambda i, j, *_: (0, 0))]
    out_spec = pl.BlockSpec((tm, d_model), lambda i, j, *_: (i, 0))
    out_shape = jax.ShapeDtypeStruct((rows, d_model), F32)
    return (rows // tm, d_ff // tf), in_specs, out_spec, out_shape


def _ffn(hn, h, w, tm, tf):
    rows, d_model = h.shape
    grid, in_specs, out_spec, out_shape = _ffn_specs(rows, d_model, w["gate"].shape[1], tm, tf)
    return pl.pallas_call(
        _ffn_step,
        grid=grid,
        in_specs=in_specs,
        out_specs=out_spec,
        out_shape=out_shape,
        scratch_shapes=[pltpu.VMEM((tm, d_model), F32)],
        compiler_params=_params("parallel", "arbitrary"),
        name="ffn",
    )(hn, h, w["gate"], w["up"], w["down"], w["g_post_ffn"])


def _layer_weights(l, n_heads, head_dim, g_pre_mix, w_in, b_f, conv_w, conv_b, w_rg, b_rg, w_ig,
                   b_ig, lru_lambda, w_o_attn, w_o_rnn, w_out, g_post_mix, g_pre_ffn, w_gate,
                   w_up, w_down, g_post_ffn):
    d_model = w_in.shape[1]
    n_blocks, bw = w_rg.shape[1], w_rg.shape[2]
    d_rnn = n_blocks * bw
    aw = n_heads * head_dim
    wi_t = jnp.swapaxes(w_in[l], 0, 1)
    f_lo, f_hi = 3 * aw, 3 * aw + n_heads
    cols = {
        "qkv": _cast_rows(wi_t, 0, f_lo, 512),
        "rest": _cast_rows(wi_t, f_hi, wi_t.shape[0] - f_hi, 512),
        "f": jnp.pad(wi_t[f_lo:f_hi].astype(BF16), ((0, LANES - n_heads), (0, 0))),
        "b_f": jnp.pad(b_f[l].reshape(1, n_heads), ((0, 0), (0, LANES - n_heads))),
        "widths": {"q": aw, "k": aw, "v": aw, "xr": d_rnn, "yr": d_rnn,
                   "ga": d_model, "gr": d_model},
        "n_heads": n_heads,
        "head_dim": head_dim,
    }
    rnn = {
        "n_rnn_blocks": n_blocks, "rnn_block": bw,
        "conv_w": conv_w[l], "conv_b": conv_b[l].reshape(1, d_rnn),
        "b_rg": b_rg[l].reshape(1, d_rnn), "b_ig": b_ig[l].reshape(1, d_rnn),
        "lam": lru_lambda[l].reshape(1, d_rnn),
        "conv_w3": conv_w[l].reshape(CONV_WIDTH, n_blocks, bw),
        "conv_b2": conv_b[l].reshape(n_blocks, bw),
        "b_rg2": b_rg[l].reshape(n_blocks, bw), "b_ig2": b_ig[l].reshape(n_blocks, bw),
        "lam2": lru_lambda[l].reshape(n_blocks, bw),
        "w_rg": w_rg[l].astype(BF16), "w_ig": w_ig[l].astype(BF16),
    }
    mix = {"o_attn": w_o_attn[l].astype(BF16), "o_rnn": w_o_rnn[l].astype(BF16),
           "out": w_out[l].astype(BF16),
           "g_post_mix": g_post_mix[l].reshape(1, d_model),
           "g_pre_ffn": g_pre_ffn[l].reshape(1, d_model)}
    ffn = {"gate": w_gate[l].astype(BF16), "up": w_up[l].astype(BF16),
           "down": w_down[l].astype(BF16), "g_post_ffn": g_post_ffn[l].reshape(1, d_model)}
    return g_pre_mix[l].reshape(1, d_model), cols, rnn, mix, ffn


def kernel(x_prompt, x_sample, cache_k, cache_v, cache_logf, state_h, state_conv, page_table,
           g_pre_mix, w_in, b_f, conv_w, conv_b, w_rg, b_rg, w_ig, b_ig, lru_lambda,
           w_o_attn, w_o_rnn, w_out, g_post_mix, g_pre_ffn, w_gate, w_up, w_down, g_post_ffn):
    batch, seq, d_model = x_prompt.shape
    dec_batch, dec_seq, _ = x_sample.shape
    assert dec_seq == 1, "sample group carries one new token per sequence"
    depth, n_phys, page, n_heads, head_dim = cache_k.shape
    aw = n_heads * head_dim
    d_rnn = state_h.shape[2]
    n_pages = page_table.shape[1]
    assert LANES % n_heads == 0 and (page * n_heads) % LANES == 0

    xp = x_prompt.reshape(batch * seq, d_model)
    xs = x_sample.reshape(dec_batch, d_model)
    outs = {name: [] for name in ("kp", "vp", "lp", "hp", "cp", "ks", "vs", "ls", "hs", "cs")}

    for l in range(depth):
        g_in, w_cols, w_rnn, w_mix, w_ffn = _layer_weights(
            l, n_heads, head_dim, g_pre_mix, w_in, b_f, conv_w, conv_b, w_rg, b_rg, w_ig, b_ig,
            lru_lambda, w_o_attn, w_o_rnn, w_out, g_post_mix, g_pre_ffn, w_gate, w_up, w_down,
            g_post_ffn)

        q, k16, v16, k, v, xr, yr, ga, gr, lf, c = _in_proj(xp, g_in, w_cols, tm=512,
                                                            n_col_steps=4, seq=seq)
        qs, _, _, ks, vs, xrs, yrs, gas, grs, lfs = _in_proj(xs, g_in, w_cols, tm=dec_batch,
                                                             n_col_steps=4)
        ks = ks.reshape(dec_batch, n_heads, head_dim)
        vs = vs.reshape(dec_batch, n_heads, head_dim)

        attn = _attn_prompt(q, k16, v16, c, batch, seq, n_heads, head_dim, tq=256)
        rnn, h_last = _rglru_prompt(xr, yr, w_rnn, batch, seq, tc=128)
        h, hn = _out_proj(attn, rnn, ga, gr, xp, w_mix, tm=512, n_parts=2)
        xp_next = _ffn(hn, h, w_ffn, tm=512, tf=512)
        outs["kp"].append(k.reshape(batch, seq, n_heads, head_dim))
        outs["vp"].append(v.reshape(batch, seq, n_heads, head_dim))
        outs["lp"].append(lf.reshape(batch, seq, n_heads))
        outs["hp"].append(h_last)
        outs["cp"].append(xr.reshape(batch, seq, d_rnn)[:, seq - (CONV_WIDTH - 1):])

        attn_s = _attn_sample(
            qs.reshape(dec_batch, n_heads, head_dim), ks, vs, lfs.reshape(dec_batch, n_heads, 1),
            cache_k[l], cache_v[l], cache_logf[l].reshape(n_phys, 1, page * n_heads),
            page_table, pages_per_step=n_pages)
        rnn, h_new, buf_new = _rglru_sample(xrs, yrs, jnp.swapaxes(state_conv[l], 0, 1),
                                            state_h[l], w_rnn)
        h, hn = _out_proj(attn_s.reshape(dec_batch, aw), rnn, gas, grs, xs, w_mix, tm=dec_batch)
        xs_next = _ffn(hn, h, w_ffn, tm=dec_batch, tf=512)
        outs["ks"].append(ks.reshape(dec_batch, 1, n_heads, head_dim))
        outs["vs"].append(vs.reshape(dec_batch, 1, n_heads, head_dim))
        outs["ls"].append(lfs.reshape(dec_batch, 1, n_heads))
        outs["hs"].append(h_new)
        outs["cs"].append(jnp.swapaxes(buf_new, 0, 1))

        xp, xs = xp_next, xs_next

    st = {name: jnp.stack(vals) for name, vals in outs.items()}
    return (xp.reshape(batch, seq, d_model), xs.reshape(dec_batch, 1, d_model),
            st["kp"], st["vp"], st["lp"], st["hp"], st["cp"],
            st["ks"], st["vs"], st["ls"], st["hs"], st["cs"])
```

```python
import functools
import math

import jax
import jax.numpy as jnp
from jax import lax
from jax.experimental import pallas as pl
from jax.experimental.pallas import tpu as pltpu

F32 = jnp.float32
BF16 = jnp.bfloat16

EPS = 1e-6
LRU_C = 8.0
CONV_WIDTH = 4
NEG_BIG = -1e30

SUBLANES = 8
LANES = 128
VMEM_LIMIT_BYTES = 56 * 1024 * 1024


def _params(*semantics):
    return pltpu.CompilerParams(dimension_semantics=semantics,
                                vmem_limit_bytes=VMEM_LIMIT_BYTES)


def _rms_norm(x, g):
    inv = lax.rsqrt(jnp.mean(x * x, axis=-1, keepdims=True) + EPS)
    return (x * inv) * g


def _log_sigmoid(x):
    return jnp.minimum(x, 0.0) - jnp.log1p(jnp.exp(-jnp.abs(x)))


def _softplus(x):
    return jnp.maximum(x, 0.0) + jnp.log1p(jnp.exp(-jnp.abs(x)))


def _gelu_tanh(x):
    c = math.sqrt(2.0 / math.pi)
    return x * (0.5 * (1.0 + jnp.tanh(c * (x + 0.044715 * (x * x * x)))))


def _dot(a, b):
    return jnp.dot(a, b, preferred_element_type=F32)


def _dot_nt(a, b):
    return lax.dot_general(a, b, (((1,), (1,)), ((), ())), preferred_element_type=F32)


def _cast_kernel(x_ref, o_ref):
    o_ref[...] = x_ref[...].astype(o_ref.dtype)


def _cast_shifted_kernel(a_ref, b_ref, o_ref, *, shift):
    o_ref[...] = jnp.concatenate([a_ref[shift:, :], b_ref[:shift, :]], axis=0).astype(o_ref.dtype)


def _cast_rows(w_t, first_row, n_rows, tile):
    width = w_t.shape[1]
    assert n_rows % tile == 0 and tile % SUBLANES == 0
    shift = first_row % tile
    base = first_row // tile
    out_spec = pl.BlockSpec((tile, width), lambda j: (j, 0))
    out_shape = jax.ShapeDtypeStruct((n_rows, width), BF16)
    if shift == 0:
        return pl.pallas_call(
            _cast_kernel, grid=(n_rows // tile,),
            in_specs=[pl.BlockSpec((tile, width), lambda j: (base + j, 0))],
            out_specs=out_spec, out_shape=out_shape,
            compiler_params=_params("parallel"), name="cast",
        )(w_t)
    assert shift == SUBLANES and first_row + n_rows <= w_t.shape[0]
    per_tile = tile // SUBLANES
    return pl.pallas_call(
        functools.partial(_cast_shifted_kernel, shift=shift), grid=(n_rows // tile,),
        in_specs=[pl.BlockSpec((tile, width), lambda j: (base + j, 0)),
                  pl.BlockSpec((SUBLANES, width), lambda j: ((base + j + 1) * per_tile, 0))],
        out_specs=out_spec, out_shape=out_shape,
        compiler_params=_params("parallel"), name="cast_shifted",
    )(w_t, w_t)


def _prefix_sum(x, axis, start=1):
    idx = lax.broadcasted_iota(jnp.int32, x.shape, axis)
    shift = start
    while shift < x.shape[axis]:
        x = x + jnp.where(idx >= shift, pltpu.roll(x, shift, axis=axis), 0.0)
        shift *= 2
    return x


def _in_proj_kernel(x_ref, g_ref, wq, wk, wv, wxr, wyr, wga, wgr, wf, bf_ref,
                    q_o, k16_o, v16_o, k32_o, v32_o, xr_o, yr_o, ga_o, gr_o, lf_o, *rest,
                    tiles_per_seq, inv_scale):
    i, j = pl.program_id(0), pl.program_id(1)
    xn_s = rest[-1]
    n_heads = lf_o.shape[1]

    @pl.when(j == 0)
    def _():
        xn = _rms_norm(x_ref[...], g_ref[...]).astype(BF16)
        xn_s[...] = xn
        lf = _log_sigmoid(_dot_nt(xn, wf[...]) + bf_ref[...])
        lf_o[...] = lf[:, :n_heads]
        if tiles_per_seq is not None:
            c_o, carry_s = rest[0], rest[1]

            @pl.when(lax.rem(i, tiles_per_seq) == 0)
            def _():
                carry_s[...] = jnp.zeros_like(carry_s)

            c = _prefix_sum(lf, axis=0) + carry_s[...]
            carry_s[...] = c[c.shape[0] - 1:, :]
            hi, mid, lo = _split3_bf16(c * inv_scale)
            lane = lax.broadcasted_iota(jnp.int32, c.shape, 1)
            packed = jnp.where(
                lane < n_heads, hi,
                jnp.where(lane < 2 * n_heads, pltpu.roll(mid, n_heads, axis=1),
                          jnp.where(lane < 3 * n_heads, pltpu.roll(lo, 2 * n_heads, axis=1),
                                    jnp.where(lane == 3 * n_heads, 1.0, 0.0))))
            c_o[...] = packed.astype(c_o.dtype)

    xn = xn_s[...]
    for w, o in ((wq, q_o), (wxr, xr_o), (wyr, yr_o), (wga, ga_o), (wgr, gr_o)):
        o[...] = _dot_nt(xn, w[...]).astype(o.dtype)

    tm = xn.shape[0]
    n_heads = k32_o.shape[0] // tm
    head_dim = k32_o.shape[1]
    heads_per_step = wk.shape[0] // head_dim
    for w, o16, o32 in ((wk, k16_o, k32_o), (wv, v16_o, v32_o)):
        res = _dot_nt(xn, w[...])
        o16[...] = res.astype(o16.dtype)
        for hh in range(heads_per_step):
            head = j * heads_per_step + hh
            o32[pl.ds(head, tm, stride=n_heads), :] = res[:, hh * head_dim:(hh + 1) * head_dim]


def _in_proj(x, g, w, tm, n_col_steps, seq=None):
    rows, d_model = x.shape
    n_heads, head_dim = w["n_heads"], w["head_dim"]
    cols = {}
    for arr, names in (("qkv", ("q", "k", "v")), ("rest", ("xr", "yr", "ga", "gr"))):
        start = 0
        for name in names:
            cols[name] = (arr, start, w["widths"][name])
            start += w["widths"][name]
    order = ("q", "k", "v", "xr", "yr", "ga", "gr")
    tiles = {name: cols[name][2] // n_col_steps for name in order}
    grid = (rows // tm, n_col_steps)

    def w_spec(name):
        first_block = cols[name][1] // tiles[name]
        return pl.BlockSpec((tiles[name], d_model), lambda i, j: (first_block + j, 0))

    def o_spec(name):
        return pl.BlockSpec((tm, tiles[name]), lambda i, j: (i, j))

    def o_shape(name, dtype):
        return jax.ShapeDtypeStruct((rows, cols[name][2]), dtype)

    def per_head():
        return pl.BlockSpec((tm, n_heads), lambda i, j: (i, 0))

    flat_spec = pl.BlockSpec((tm * n_heads, head_dim), lambda i, j: (i, 0))
    flat_shape = jax.ShapeDtypeStruct((rows * n_heads, head_dim), F32)
    out_specs = [o_spec("q"), o_spec("k"), o_spec("v"), flat_spec, flat_spec,
                 o_spec("xr"), o_spec("yr"), o_spec("ga"), o_spec("gr"), per_head()]
    out_shape = [o_shape("q", BF16), o_shape("k", BF16), o_shape("v", BF16),
                 flat_shape, flat_shape,
                 o_shape("xr", F32), o_shape("yr", F32), o_shape("ga", BF16), o_shape("gr", BF16),
                 jax.ShapeDtypeStruct((rows, n_heads), F32)]
    scratch = [pltpu.VMEM((tm, d_model), BF16)]
    if seq is not None:
        out_specs.append(pl.BlockSpec((tm, LANES), lambda i, j: (i, 0)))
        out_shape.append(jax.ShapeDtypeStruct((rows, LANES), BF16))
        scratch.insert(0, pltpu.VMEM((1, LANES), F32))
    return pl.pallas_call(
        functools.partial(_in_proj_kernel,
                          tiles_per_seq=None if seq is None else seq // tm,
                          inv_scale=head_dim ** 0.5),
        grid=grid,
        in_specs=[pl.BlockSpec((tm, d_model), lambda i, j: (i, 0)),
                  pl.BlockSpec((1, d_model), lambda i, j: (0, 0))]
                 + [w_spec(name) for name in order]
                 + [pl.BlockSpec((LANES, d_model), lambda i, j: (0, 0)),
                    pl.BlockSpec((1, LANES), lambda i, j: (0, 0))],
        out_specs=out_specs,
        out_shape=out_shape,
        scratch_shapes=scratch,
        compiler_params=_params("arbitrary", "arbitrary"),
        name="in_proj",
    )(x, g, *[w[cols[name][0]] for name in order], w["f"], w["b_f"])


def _split3_bf16(x):
    hi = x.astype(BF16)
    r = x - hi.astype(F32)
    mid = r.astype(BF16)
    lo = (r - mid.astype(F32)).astype(BF16)
    return hi.astype(F32), mid.astype(F32), lo.astype(F32)


def _attn_prompt_kernel(q_ref, k_ref, v_ref, c_ref, o_ref, qa_s, ka_s, m_s, l_s, acc_s, *,
                        tq, scale, n_heads):
    seq, head_dim = q_ref.shape
    h = pl.program_id(1)
    src = lax.broadcasted_iota(jnp.int32, (LANES, LANES), 0)
    dst = lax.broadcasted_iota(jnp.int32, (LANES, LANES), 1)
    one_src = src == 3 * n_heads
    part_of_dst = jnp.where(dst < 3, dst, dst - 3) * n_heads + h
    sel_q = jnp.where((dst < 3) & (src == part_of_dst), 1.0,
                      jnp.where((dst >= 3) & (dst < 6) & one_src, 1.0, 0.0))
    sel_k = jnp.where((dst >= 3) & (dst < 6) & (src == part_of_dst), -1.0,
                      jnp.where((dst < 3) & one_src, 1.0, 0.0))
    c_rows = c_ref[...]
    qa_s[:, :head_dim] = q_ref[...]
    qa_s[:, head_dim:] = _dot(c_rows, sel_q.astype(BF16)).astype(BF16)
    ka_s[:, :head_dim] = k_ref[...]
    ka_s[:, head_dim:] = _dot(c_rows, sel_k.astype(BF16)).astype(BF16)

    sigma = scale * math.log2(math.e)
    row = lax.broadcasted_iota(jnp.int32, (tq, tq), 0)
    col = lax.broadcasted_iota(jnp.int32, (tq, tq), 1)
    visible = col <= row
    groups = tq // LANES
    n_tiles = seq // tq
    for j in range(n_tiles):
        s_all = _dot_nt(qa_s[j * tq:, :], ka_s[j * tq:(j + 1) * tq, :]) * sigma
        v_j = v_ref[j * tq:(j + 1) * tq, :]
        for i in range(j, n_tiles):
            rows = slice(i * tq, (i + 1) * tq)
            s = s_all[(i - j) * tq:(i - j + 1) * tq, :]
            if i == j:
                s = jnp.where(visible, s, NEG_BIG)
            blk_max = jnp.broadcast_to(jnp.max(s, axis=-1, keepdims=True), (tq, LANES))
            m_new = blk_max if j == 0 else jnp.maximum(m_s[rows, :], blk_max)
            parts = [jnp.exp2(s[:, g * LANES:(g + 1) * LANES] - m_new) for g in range(groups)]
            l_new = parts[0]
            for p in parts[1:]:
                l_new = l_new + p
            pv = _dot(jnp.concatenate([p.astype(BF16) for p in parts], axis=1), v_j)
            if j > 0:
                alpha = jnp.exp2(m_s[rows, :] - m_new)
                l_new = alpha * l_s[rows, :] + l_new
                pv = alpha * acc_s[rows, :] + pv
            if i == j:
                l = jnp.sum(l_new, axis=-1, keepdims=True)
                o_ref[rows, :] = (pv / l).astype(o_ref.dtype)
            else:
                m_s[rows, :] = m_new
                l_s[rows, :] = l_new
                acc_s[rows, :] = pv


def _attn_prompt(q, k, v, c, batch, seq, n_heads, head_dim, tq):
    assert head_dim == LANES, "per-row softmax state is kept one vreg lane group wide"
    head_spec = pl.BlockSpec((seq, head_dim), lambda b, h: (b, h))
    return pl.pallas_call(
        functools.partial(_attn_prompt_kernel, tq=tq, scale=head_dim ** -0.5, n_heads=n_heads),
        grid=(batch, n_heads),
        in_specs=[head_spec, head_spec, head_spec,
                  pl.BlockSpec((seq, LANES), lambda b, h: (b, 0))],
        out_specs=head_spec,
        out_shape=jax.ShapeDtypeStruct(q.shape, BF16),
        scratch_shapes=[pltpu.VMEM((seq, head_dim + LANES), BF16),
                        pltpu.VMEM((seq, head_dim + LANES), BF16),
                        pltpu.VMEM((seq, LANES), F32),
                        pltpu.VMEM((seq, LANES), F32),
                        pltpu.VMEM((seq, head_dim), F32)],
        compiler_params=_params("parallel", "parallel"),
        name="attn_prompt",
    )(q, k, v, c)


def _paged_forget_bias(lf_pages, lf_new, n_heads):
    rows, page = lf_pages.shape
    n_pages = rows // n_heads
    within = _prefix_sum(lf_pages, axis=1)
    totals = jnp.broadcast_to(within[:, page - 1:], (rows, page))
    upto = _prefix_sum(totals, axis=0, start=n_heads)
    c_past = within + (upto - totals)
    c_new = upto[rows - n_heads:, 0:1] + lf_new
    bias = jnp.concatenate([c_new] * n_pages, axis=0) - c_past
    src = lax.broadcasted_iota(jnp.int32, (page, page * n_heads), 0)
    dst = lax.broadcasted_iota(jnp.int32, (page, page * n_heads), 1)
    expand = jnp.where(dst // n_heads == src, 1.0, 0.0).astype(BF16)
    parts = jnp.concatenate(_split3_bf16(bias), axis=0).astype(BF16)
    wide = _dot(parts, expand)
    return wide[:rows] + wide[rows:2 * rows] + wide[2 * rows:]


def _paged_attn_step(step, q_ref, kn_ref, vn_ref, lfn_ref, k_refs, v_refs, lf_refs, o_ref,
                     s_s, bias_s, m_s, l_s, acc_s, *, steps_per_seq, scale):
    pps = len(k_refs)
    page, n_heads, head_dim = k_refs[0].shape[1:]
    pw = page * n_heads
    part = lax.rem(step, steps_per_seq)

    head_of_row = lax.broadcasted_iota(jnp.int32, (n_heads, pw), 0)
    lane = lax.broadcasted_iota(jnp.int32, (n_heads, pw), 1)
    on_diag = head_of_row == lane % n_heads
    q = q_ref[0]

    @pl.when(part == 0)
    def _():
        bias_s[...] = _paged_forget_bias(
            jnp.concatenate([r[0] for r in lf_refs], axis=0), lfn_ref[0], n_heads)
        kn = kn_ref[0].astype(BF16).astype(F32)
        m_s[...] = jnp.sum(q.astype(F32) * kn, axis=-1, keepdims=True) * scale
        l_s[...] = jnp.ones_like(l_s)
        acc_s[...] = vn_ref[0].astype(BF16).astype(F32)

    for p in range(pps):
        kf = k_refs[p][0].reshape(pw, head_dim).astype(BF16)
        first = pl.multiple_of((part * pps + p) * n_heads, n_heads)
        s = _dot_nt(q, kf) * scale + bias_s[pl.ds(first, n_heads), :]
        s_s[:, p * pw:(p + 1) * pw] = jnp.where(on_diag, s, NEG_BIG)

    s_all = s_s[...]
    m_old = m_s[...]
    m_new = jnp.maximum(m_old, jnp.max(s_all, axis=-1, keepdims=True))
    p_all = jnp.exp(s_all - m_new)
    alpha = jnp.exp(m_old - m_new)
    l_s[...] = alpha * l_s[...] + jnp.sum(p_all, axis=-1, keepdims=True)
    m_s[...] = m_new
    p16 = p_all.astype(BF16)
    acc = alpha * acc_s[...]
    for p in range(pps):
        vf = v_refs[p][0].reshape(pw, head_dim).astype(BF16)
        acc = acc + _dot(p16[:, p * pw:(p + 1) * pw], vf)
    acc_s[...] = acc

    @pl.when(part == steps_per_seq - 1)
    def _():
        o_ref[0] = (acc_s[...] / l_s[...]).astype(o_ref.dtype)


def _attn_sample_kernel(pt_ref, q_ref, kn_ref, vn_ref, lfn_ref, *refs, pages_per_step, n_pages):
    del pt_ref
    k_refs = refs[:pages_per_step]
    v_refs = refs[pages_per_step:2 * pages_per_step]
    lf_refs = refs[2 * pages_per_step:2 * pages_per_step + n_pages]
    o_ref = refs[2 * pages_per_step + n_pages]
    scratch = refs[2 * pages_per_step + n_pages + 1:]
    _paged_attn_step(pl.program_id(0) * pl.num_programs(1) + pl.program_id(1),
                     q_ref, kn_ref, vn_ref, lfn_ref, k_refs, v_refs, lf_refs, o_ref, *scratch,
                     steps_per_seq=n_pages // pages_per_step, scale=q_ref.shape[2] ** -0.5)


def _attn_sample(q, k_new, v_new, lf_new, cache_k, cache_v, cache_lf, page_table, pages_per_step):
    batch, n_pages = page_table.shape
    _, page, n_heads, head_dim = cache_k.shape
    pw = page * n_heads
    steps_per_seq = n_pages // pages_per_step
    assert page == LANES, "one page of log-forget values per head is one lane row"

    def row_spec(shape):
        return pl.BlockSpec((1,) + shape, lambda b, s, pt: (b, 0, 0))

    def page_spec(p):
        return pl.BlockSpec((1, page, n_heads, head_dim),
                            lambda b, s, pt: (pt[b, s * pages_per_step + p], 0, 0, 0))

    def lf_spec(p):
        return pl.BlockSpec((1, n_heads, page), lambda b, s, pt: (pt[b, p], 0, 0))

    return pl.pallas_call(
        functools.partial(_attn_sample_kernel, pages_per_step=pages_per_step, n_pages=n_pages),
        grid_spec=pltpu.PrefetchScalarGridSpec(
            num_scalar_prefetch=1,
            grid=(batch, steps_per_seq),
            in_specs=[row_spec((n_heads, head_dim))] * 3 + [row_spec((n_heads, 1))]
                     + [page_spec(p) for p in range(pages_per_step)] * 2
                     + [lf_spec(p) for p in range(n_pages)],
            out_specs=row_spec((n_heads, head_dim)),
            scratch_shapes=[pltpu.VMEM((n_heads, pages_per_step * pw), F32),
                            pltpu.VMEM((n_pages * n_heads, pw), F32),
                            pltpu.VMEM((n_heads, 1), F32),
                            pltpu.VMEM((n_heads, 1), F32),
                            pltpu.VMEM((n_heads, head_dim), F32)],
        ),
        out_shape=jax.ShapeDtypeStruct((batch, n_heads, head_dim), BF16),
        compiler_params=_params("arbitrary", "arbitrary"),
        name="attn_sample",
    )(page_table, q, k_new, v_new, lf_new,
      *([cache_k] * pages_per_step), *([cache_v] * pages_per_step), *([cache_lf] * n_pages))


def _rglru_gates(conv, r_pre, i_pre, b_rg, b_ig, neg_c_softplus):
    r = jax.nn.sigmoid(r_pre + b_rg)
    i_g = jax.nn.sigmoid(i_pre + b_ig)
    log_a = r * neg_c_softplus
    a = jnp.exp(log_a)
    b = jnp.sqrt(-jnp.tanh(log_a) * (1.0 + a * a)) * (i_g * conv)
    return a, b


def _rglru_prompt_kernel(xr_ref, yr_ref, cw_ref, cb_ref, wrg_ref, wig_ref, brg_ref, big_ref,
                         lam_ref, o_ref, hl_ref, xc_s, conv_s, rp_s, ip_s, a_s, b_s, h_s):
    c = pl.program_id(0)
    nb, tc, n_blocks, bw = xr_ref.shape
    halo = CONV_WIDTH - 1
    rows = nb * tc

    @pl.when(c == 0)
    def _():
        xc_s[:, 0:halo] = jnp.zeros((nb, halo, n_blocks, bw), F32)
        h_s[...] = jnp.zeros_like(h_s)

    @pl.when(c > 0)
    def _():
        xc_s[:, 0:halo] = xc_s[:, tc:tc + halo]

    xc_s[:, halo:halo + tc] = xr_ref[...]

    conv = cb_ref[...] + xc_s[:, 0:tc] * cw_ref[0]
    for i in range(1, CONV_WIDTH):
        conv = conv + xc_s[:, i:i + tc] * cw_ref[i]

    conv_s[...] = conv.reshape(rows * n_blocks, bw)
    for n in range(n_blocks):
        xn = conv_s[pl.ds(n, rows, stride=n_blocks), :].astype(BF16)
        rp_s[pl.ds(n, rows, stride=n_blocks), :] = _dot(xn, wrg_ref[n])
        ip_s[pl.ds(n, rows, stride=n_blocks), :] = _dot(xn, wig_ref[n])

    neg_c_softplus = -LRU_C * _softplus(-lam_ref[...])
    a, b = _rglru_gates(conv,
                        rp_s[...].reshape(nb, tc, n_blocks, bw),
                        ip_s[...].reshape(nb, tc, n_blocks, bw),
                        brg_ref[...], big_ref[...], neg_c_softplus)
    a_s[...] = a
    b_s[...] = b

    def step(t, hs):
        new = tuple(a_s[j, t] * hs[j] + b_s[j, t] for j in range(nb))
        for j in range(nb):
            b_s[j, t] = new[j]
        return new

    hs = lax.fori_loop(0, tc, step, tuple(h_s[j] for j in range(nb)), unroll=8)
    for j in range(nb):
        h_s[j] = hs[j]

    conv_s[...] = (b_s[...] * _gelu_tanh(yr_ref[...])).reshape(rows * n_blocks, bw)
    for n in range(n_blocks):
        o_ref[:, :, n * bw:(n + 1) * bw] = (
            conv_s[pl.ds(n, rows, stride=n_blocks), :].reshape(nb, tc, bw).astype(o_ref.dtype))
    hl_ref[...] = h_s[...]


def _rglru_prompt(xr, yr, w, batch, seq, tc):
    n_blocks, bw = w["n_rnn_blocks"], w["rnn_block"]
    xr4 = xr.reshape(batch, seq, n_blocks, bw)
    yr4 = yr.reshape(batch, seq, n_blocks, bw)
    rows = batch * tc
    halo = CONV_WIDTH - 1

    def const(shape):
        return pl.BlockSpec(shape, lambda c: (0,) * len(shape))

    seq_spec = pl.BlockSpec((batch, tc, n_blocks, bw), lambda c: (0, c, 0, 0))
    out, h_last = pl.pallas_call(
        _rglru_prompt_kernel,
        grid=(seq // tc,),
        in_specs=[seq_spec, seq_spec,
                  const((CONV_WIDTH, n_blocks, bw)), const((n_blocks, bw)),
                  const((n_blocks, bw, bw)), const((n_blocks, bw, bw)),
                  const((n_blocks, bw)), const((n_blocks, bw)), const((n_blocks, bw))],
        out_specs=[pl.BlockSpec((batch, tc, n_blocks * bw), lambda c: (0, c, 0)),
                   const((batch, n_blocks, bw))],
        out_shape=[jax.ShapeDtypeStruct((batch, seq, n_blocks * bw), BF16),
                   jax.ShapeDtypeStruct((batch, n_blocks, bw), F32)],
        scratch_shapes=[pltpu.VMEM((batch, tc + halo, n_blocks, bw), F32),
                        pltpu.VMEM((rows * n_blocks, bw), F32),
                        pltpu.VMEM((rows * n_blocks, bw), F32),
                        pltpu.VMEM((rows * n_blocks, bw), F32),
                        pltpu.VMEM((batch, tc, n_blocks, bw), F32),
                        pltpu.VMEM((batch, tc, n_blocks, bw), F32),
                        pltpu.VMEM((batch, n_blocks, bw), F32)],
        compiler_params=_params("arbitrary"),
        name="rglru_prompt",
    )(xr4, yr4, w["conv_w3"], w["conv_b2"], w["w_rg"], w["w_ig"], w["b_rg2"], w["b_ig2"],
      w["lam2"])
    return out.reshape(batch * seq, n_blocks * bw), h_last.reshape(batch, n_blocks * bw)


def _rglru_sample_kernel(xr_ref, yr_ref, buf_ref, h0_ref, cw_ref, cb_ref, wrg_ref, wig_ref,
                         brg_ref, big_ref, lam_ref, o_ref, h_ref, nbuf_ref):
    halo = CONV_WIDTH - 1
    n_blocks, bw = wrg_ref.shape[0], wrg_ref.shape[1]
    xr = xr_ref[...]
    conv = cb_ref[...] + xr * cw_ref[halo:halo + 1, :]
    for i in range(halo):
        conv = conv + buf_ref[i] * cw_ref[i:i + 1, :]
    r_parts, i_parts = [], []
    for n in range(n_blocks):
        xn = conv[:, n * bw:(n + 1) * bw].astype(BF16)
        r_parts.append(_dot(xn, wrg_ref[n]))
        i_parts.append(_dot(xn, wig_ref[n]))
    neg_c_softplus = -LRU_C * _softplus(-lam_ref[...])
    a, b = _rglru_gates(conv, jnp.concatenate(r_parts, axis=1), jnp.concatenate(i_parts, axis=1),
                        brg_ref[...], big_ref[...], neg_c_softplus)
    h = a * h0_ref[...] + b
    h_ref[...] = h
    o_ref[...] = h * _gelu_tanh(yr_ref[...])
    for i in range(halo - 1):
        nbuf_ref[i] = buf_ref[i + 1]
    nbuf_ref[halo - 1] = xr


def _rglru_sample(xr, yr, conv_buf_t, h0, w):
    rows, width = xr.shape
    halo = CONV_WIDTH - 1
    return pl.pallas_call(
        _rglru_sample_kernel,
        out_shape=[jax.ShapeDtypeStruct((rows, width), F32),
                   jax.ShapeDtypeStruct((rows, width), F32),
                   jax.ShapeDtypeStruct((halo, rows, width), F32)],
        compiler_params=pltpu.CompilerParams(vmem_limit_bytes=VMEM_LIMIT_BYTES),
        name="rglru_sample",
    )(xr, yr, conv_buf_t, h0, w["conv_w"], w["conv_b"], w["w_rg"], w["w_ig"],
      w["b_rg"], w["b_ig"], w["lam"])


def _out_proj_kernel(attn_ref, rnn_ref, ga_ref, gr_ref, x_ref, woa_ref, wor_ref, wout_ref,
                     gpm_ref, gpf_ref, h_o, hn_o, *, n_parts):
    part_rows = x_ref.shape[0] // n_parts
    for part in range(n_parts):
        r = slice(part * part_rows, (part + 1) * part_rows)
        o = (jax.nn.sigmoid(ga_ref[r, :].astype(F32))
             * _dot(attn_ref[r, :].astype(BF16), woa_ref[...])
             + jax.nn.sigmoid(gr_ref[r, :].astype(F32))
             * _dot(rnn_ref[r, :].astype(BF16), wor_ref[...]))
        mix = _dot(o.astype(BF16), wout_ref[...])
        h = x_ref[r, :] + _rms_norm(mix, gpm_ref[...])
        h_o[r, :] = h
        hn_o[r, :] = _rms_norm(h, gpf_ref[...]).astype(hn_o.dtype)


def _out_proj(attn, rnn, ga, gr, x, w, tm, n_parts=1):
    rows, d_model = x.shape

    def row(width):
        return pl.BlockSpec((tm, width), lambda i: (i, 0))

    def const(shape):
        return pl.BlockSpec(shape, lambda i: (0, 0), pipeline_mode=pl.Buffered(1))

    return pl.pallas_call(
        functools.partial(_out_proj_kernel, n_parts=n_parts),
        grid=(rows // tm,),
        in_specs=[row(attn.shape[1]), row(rnn.shape[1]), row(d_model), row(d_model), row(d_model),
                  const(w["o_attn"].shape), const(w["o_rnn"].shape), const(w["out"].shape),
                  const((1, d_model)), const((1, d_model))],
        out_specs=[row(d_model), row(d_model)],
        out_shape=[jax.ShapeDtypeStruct((rows, d_model), F32),
                   jax.ShapeDtypeStruct((rows, d_model), BF16)],
        compiler_params=_params("parallel"),
        name="out_proj",
    )(attn, rnn, ga, gr, x, w["o_attn"], w["o_rnn"], w["out"], w["g_post_mix"], w["g_pre_ffn"])


def _ffn_step(hn_ref, h_ref, wg_ref, wu_ref, wd_ref, g_ref, y_o, acc_s):
    j = pl.program_id(1)

    @pl.when(j == 0)
    def _():
        acc_s[...] = jnp.zeros_like(acc_s)

    hn = hn_ref[...]
    gate = _dot(hn, wg_ref[...])
    up = _dot(hn, wu_ref[...])
    acc_s[...] += _dot((jax.nn.silu(gate) * up).astype(BF16), wd_ref[...])

    @pl.when(j == pl.num_programs(1) - 1)
    def _():
        y_o[...] = h_ref[...] + _rms_norm(acc_s[...], g_ref[...])


def _ffn_specs(rows, d_model, d_ff, tm, tf):
    in_specs = [pl.BlockSpec((tm, d_model), lambda i, j, *_: (i, 0)),
                pl.BlockSpec((tm, d_model), lambda i, j, *_: (i, 0)),
                pl.BlockSpec((d_model, tf), lambda i, j, *_: (0, j)),
                pl.BlockSpec((d_model, tf), lambda i, j, *_: (0, j)),
                pl.BlockSpec((tf, d_model), lambda i, j, *_: (j, 0)),
                pl.BlockSpec((1, d_model), lambda i, j, *_: (0, 0))]
    out_spec = pl.BlockSpec((tm, d_model), lambda i, j, *_: (i, 0))
    out_shape = jax.ShapeDtypeStruct((rows, d_model), F32)
    return (rows // tm, d_ff // tf), in_specs, out_spec, out_shape


def _ffn(hn, h, w, tm, tf):
    rows, d_model = h.shape
    grid, in_specs, out_spec, out_shape = _ffn_specs(rows, d_model, w["gate"].shape[1], tm, tf)
    return pl.pallas_call(
        _ffn_step,
        grid=grid,
        in_specs=in_specs,
        out_specs=out_spec,
        out_shape=out_shape,
        scratch_shapes=[pltpu.VMEM((tm, d_model), F32)],
        compiler_params=_params("parallel", "arbitrary"),
        name="ffn",
    )(hn, h, w["gate"], w["up"], w["down"], w["g_post_ffn"])


def _layer_weights(l, n_heads, head_dim, g_pre_mix, w_in, b_f, conv_w, conv_b, w_rg, b_rg, w_ig,
                   b_ig, lru_lambda, w_o_attn, w_o_rnn, w_out, g_post_mix, g_pre_ffn, w_gate,
                   w_up, w_down, g_post_ffn):
    d_model = w_in.shape[1]
    n_blocks, bw = w_rg.shape[1], w_rg.shape[2]
    d_rnn = n_blocks * bw
    aw = n_heads * head_dim
    wi_t = jnp.swapaxes(w_in[l], 0, 1)
    f_lo, f_hi = 3 * aw, 3 * aw + n_heads
    cols = {
        "qkv": _cast_rows(wi_t, 0, f_lo, 512),
        "rest": _cast_rows(wi_t, f_hi, wi_t.shape[0] - f_hi, 512),
        "f": jnp.pad(wi_t[f_lo:f_hi].astype(BF16), ((0, LANES - n_heads), (0, 0))),
        "b_f": jnp.pad(b_f[l].reshape(1, n_heads), ((0, 0), (0, LANES - n_heads))),
        "widths": {"q": aw, "k": aw, "v": aw, "xr": d_rnn, "yr": d_rnn,
                   "ga": d_model, "gr": d_model},
        "n_heads": n_heads,
        "head_dim": head_dim,
    }
    rnn = {
        "n_rnn_blocks": n_blocks, "rnn_block": bw,
        "conv_w": conv_w[l], "conv_b": conv_b[l].reshape(1, d_rnn),
        "b_rg": b_rg[l].reshape(1, d_rnn), "b_ig": b_ig[l].reshape(1, d_rnn),
        "lam": lru_lambda[l].reshape(1, d_rnn),
        "conv_w3": conv_w[l].reshape(CONV_WIDTH, n_blocks, bw),
        "conv_b2": conv_b[l].reshape(n_blocks, bw),
        "b_rg2": b_rg[l].reshape(n_blocks, bw), "b_ig2": b_ig[l].reshape(n_blocks, bw),
        "lam2": lru_lambda[l].reshape(n_blocks, bw),
        "w_rg": w_rg[l].astype(BF16), "w_ig": w_ig[l].astype(BF16),
    }
    mix = {"o_attn": w_o_attn[l].astype(BF16), "o_rnn": w_o_rnn[l].astype(BF16),
           "out": w_out[l].astype(BF16),
           "g_post_mix": g_post_mix[l].reshape(1, d_model),
           "g_pre_ffn": g_pre_ffn[l].reshape(1, d_model)}
    ffn = {"gate": w_gate[l].astype(BF16), "up": w_up[l].astype(BF16),
           "down": w_down[l].astype(BF16), "g_post_ffn": g_post_ffn[l].reshape(1, d_model)}
    return g_pre_mix[l].reshape(1, d_model), cols, rnn, mix, ffn


def kernel(x_prompt, x_sample, cache_k, cache_v, cache_logf, state_h, state_conv, page_table,
           g_pre_mix, w_in, b_f, conv_w, conv_b, w_rg, b_rg, w_ig, b_ig, lru_lambda,
           w_o_attn, w_o_rnn, w_out, g_post_mix, g_pre_ffn, w_gate, w_up, w_down, g_post_ffn):
    batch, seq, d_model = x_prompt.shape
    dec_batch, dec_seq, _ = x_sample.shape
    assert dec_seq == 1, "sample group carries one new token per sequence"
    depth, n_phys, page, n_heads, head_dim = cache_k.shape
    aw = n_heads * head_dim
    d_rnn = state_h.shape[2]
    n_pages = page_table.shape[1]
    assert LANES % n_heads == 0 and (page * n_heads) % LANES == 0

    xp = x_prompt.reshape(batch * seq, d_model)
    xs = x_sample.reshape(dec_batch, d_model)
    outs = {name: [] for name in ("kp", "vp", "lp", "hp", "cp", "ks", "vs", "ls", "hs", "cs")}

    for l in range(depth):
        g_in, w_cols, w_rnn, w_mix, w_ffn = _layer_weights(
            l, n_heads, head_dim, g_pre_mix, w_in, b_f, conv_w, conv_b, w_rg, b_rg, w_ig, b_ig,
            lru_lambda, w_o_attn, w_o_rnn, w_out, g_post_mix, g_pre_ffn, w_gate, w_up, w_down,
            g_post_ffn)

        q, k16, v16, k, v, xr, yr, ga, gr, lf, c = _in_proj(xp, g_in, w_cols, tm=512,
                                                            n_col_steps=4, seq=seq)
        qs, _, _, ks, vs, xrs, yrs, gas, grs, lfs = _in_proj(xs, g_in, w_cols, tm=dec_batch,
                                                             n_col_steps=4)
        ks = ks.reshape(dec_batch, n_heads, head_dim)
        vs = vs.reshape(dec_batch, n_heads, head_dim)

        attn = _attn_prompt(q, k16, v16, c, batch, seq, n_heads, head_dim, tq=256)
        rnn, h_last = _rglru_prompt(xr, yr, w_rnn, batch, seq, tc=128)
        h, hn = _out_proj(attn, rnn, ga, gr, xp, w_mix, tm=512, n_parts=2)
        xp_next = _ffn(hn, h, w_ffn, tm=512, tf=512)
        outs["kp"].append(k.reshape(batch, seq, n_heads, head_dim))
        outs["vp"].append(v.reshape(batch, seq, n_heads, head_dim))
        outs["lp"].append(lf.reshape(batch, seq, n_heads))
        outs["hp"].append(h_last)
        outs["cp"].append(xr.reshape(batch, seq, d_rnn)[:, seq - (CONV_WIDTH - 1):])

        attn_s = _attn_sample(
            qs.reshape(dec_batch, n_heads, head_dim), ks, vs, lfs.reshape(dec_batch, n_heads, 1),
            cache_k[l], cache_v[l], jnp.swapaxes(cache_logf[l], 1, 2),
            page_table, pages_per_step=n_pages)
        rnn, h_new, buf_new = _rglru_sample(xrs, yrs, jnp.swapaxes(state_conv[l], 0, 1),
                                            state_h[l], w_rnn)
        h, hn = _out_proj(attn_s.reshape(dec_batch, aw), rnn, gas, grs, xs, w_mix, tm=dec_batch)
        xs_next = _ffn(hn, h, w_ffn, tm=dec_batch, tf=512)
        outs["ks"].append(ks.reshape(dec_batch, 1, n_heads, head_dim))
        outs["vs"].append(vs.reshape(dec_batch, 1, n_heads, head_dim))
        outs["ls"].append(lfs.reshape(dec_batch, 1, n_heads))
        outs["hs"].append(h_new)
        outs["cs"].append(jnp.swapaxes(buf_new, 0, 1))

        xp, xs = xp_next, xs_next

    st = {name: jnp.stack(vals) for name, vals in outs.items()}
    return (xp.reshape(batch, seq, d_model), xs.reshape(dec_batch, 1, d_model),
            st["kp"], st["vp"], st["lp"], st["hp"], st["cp"],
            st["ks"], st["vs"], st["ls"], st["hs"], st["cs"])
```

```python
import functools
import math

import jax
import jax.numpy as jnp
from jax import lax
from jax.experimental import pallas as pl
from jax.experimental.pallas import tpu as pltpu

F32 = jnp.float32
BF16 = jnp.bfloat16

EPS = 1e-6
LRU_C = 8.0
CONV_WIDTH = 4
NEG_BIG = -1e30

SUBLANES = 8
LANES = 128
VMEM_LIMIT_BYTES = 56 * 1024 * 1024


def _params(*semantics):
    return pltpu.CompilerParams(dimension_semantics=semantics,
                                vmem_limit_bytes=VMEM_LIMIT_BYTES)


def _rms_norm(x, g):
    inv = lax.rsqrt(jnp.mean(x * x, axis=-1, keepdims=True) + EPS)
    return (x * inv) * g


def _log_sigmoid(x):
    return jnp.minimum(x, 0.0) - jnp.log1p(jnp.exp(-jnp.abs(x)))


def _softplus(x):
    return jnp.maximum(x, 0.0) + jnp.log1p(jnp.exp(-jnp.abs(x)))


def _gelu_tanh(x):
    c = math.sqrt(2.0 / math.pi)
    return x * (0.5 * (1.0 + jnp.tanh(c * (x + 0.044715 * (x * x * x)))))


def _dot(a, b):
    return jnp.dot(a, b, preferred_element_type=F32)


def _dot_nt(a, b):
    return lax.dot_general(a, b, (((1,), (1,)), ((), ())), preferred_element_type=F32)


def _cast_kernel(x_ref, o_ref):
    o_ref[...] = x_ref[...].astype(o_ref.dtype)


def _cast_shifted_kernel(a_ref, b_ref, o_ref, *, shift):
    o_ref[...] = jnp.concatenate([a_ref[shift:, :], b_ref[:shift, :]], axis=0).astype(o_ref.dtype)


def _cast_rows(w_t, first_row, n_rows, tile):
    width = w_t.shape[1]
    assert n_rows % tile == 0 and tile % SUBLANES == 0
    shift = first_row % tile
    base = first_row // tile
    out_spec = pl.BlockSpec((tile, width), lambda j: (j, 0))
    out_shape = jax.ShapeDtypeStruct((n_rows, width), BF16)
    if shift == 0:
        return pl.pallas_call(
            _cast_kernel, grid=(n_rows // tile,),
            in_specs=[pl.BlockSpec((tile, width), lambda j: (base + j, 0))],
            out_specs=out_spec, out_shape=out_shape,
            compiler_params=_params("parallel"), name="cast",
        )(w_t)
    assert shift == SUBLANES and first_row + n_rows <= w_t.shape[0]
    per_tile = tile // SUBLANES
    return pl.pallas_call(
        functools.partial(_cast_shifted_kernel, shift=shift), grid=(n_rows // tile,),
        in_specs=[pl.BlockSpec((tile, width), lambda j: (base + j, 0)),
                  pl.BlockSpec((SUBLANES, width), lambda j: ((base + j + 1) * per_tile, 0))],
        out_specs=out_spec, out_shape=out_shape,
        compiler_params=_params("parallel"), name="cast_shifted",
    )(w_t, w_t)


def _prefix_sum(x, axis, start=1):
    idx = lax.broadcasted_iota(jnp.int32, x.shape, axis)
    shift = start
    while shift < x.shape[axis]:
        x = x + jnp.where(idx >= shift, pltpu.roll(x, shift, axis=axis), 0.0)
        shift *= 2
    return x


def _in_proj_kernel(x_ref, g_ref, wq, wk, wv, wxr, wyr, wga, wgr, wf, bf_ref,
                    q_o, k16_o, v16_o, k32_o, v32_o, xr_o, yr_o, ga_o, gr_o, lf_o, *rest,
                    tiles_per_seq, inv_scale):
    i, j = pl.program_id(0), pl.program_id(1)
    xn_s = rest[-1]
    n_heads = lf_o.shape[1]

    @pl.when(j == 0)
    def _():
        xn = _rms_norm(x_ref[...], g_ref[...]).astype(BF16)
        xn_s[...] = xn
        lf = _log_sigmoid(_dot_nt(xn, wf[...]) + bf_ref[...])
        lf_o[...] = lf[:, :n_heads]
        if tiles_per_seq is not None:
            c_o, carry_s = rest[0], rest[1]

            @pl.when(lax.rem(i, tiles_per_seq) == 0)
            def _():
                carry_s[...] = jnp.zeros_like(carry_s)

            c = _prefix_sum(lf, axis=0) + carry_s[...]
            carry_s[...] = c[c.shape[0] - 1:, :]
            hi, mid, lo = _split3_bf16(c * inv_scale)
            lane = lax.broadcasted_iota(jnp.int32, c.shape, 1)
            packed = jnp.where(
                lane < n_heads, hi,
                jnp.where(lane < 2 * n_heads, pltpu.roll(mid, n_heads, axis=1),
                          jnp.where(lane < 3 * n_heads, pltpu.roll(lo, 2 * n_heads, axis=1),
                                    jnp.where(lane == 3 * n_heads, 1.0, 0.0))))
            c_o[...] = packed.astype(c_o.dtype)

    xn = xn_s[...]
    for w, o in ((wq, q_o), (wxr, xr_o), (wyr, yr_o), (wga, ga_o), (wgr, gr_o)):
        o[...] = _dot_nt(xn, w[...]).astype(o.dtype)

    tm = xn.shape[0]
    n_heads = k32_o.shape[0] // tm
    head_dim = k32_o.shape[1]
    heads_per_step = wk.shape[0] // head_dim
    for w, o16, o32 in ((wk, k16_o, k32_o), (wv, v16_o, v32_o)):
        res = _dot_nt(xn, w[...])
        o16[...] = res.astype(o16.dtype)
        for hh in range(heads_per_step):
            head = j * heads_per_step + hh
            o32[pl.ds(head, tm, stride=n_heads), :] = res[:, hh * head_dim:(hh + 1) * head_dim]


def _in_proj(x, g, w, tm, n_col_steps, seq=None):
    rows, d_model = x.shape
    n_heads, head_dim = w["n_heads"], w["head_dim"]
    cols = {}
    for arr, names in (("qkv", ("q", "k", "v")), ("rest", ("xr", "yr", "ga", "gr"))):
        start = 0
        for name in names:
            cols[name] = (arr, start, w["widths"][name])
            start += w["widths"][name]
    order = ("q", "k", "v", "xr", "yr", "ga", "gr")
    tiles = {name: cols[name][2] // n_col_steps for name in order}
    grid = (rows // tm, n_col_steps)

    def w_spec(name):
        first_block = cols[name][1] // tiles[name]
        return pl.BlockSpec((tiles[name], d_model), lambda i, j: (first_block + j, 0))

    def o_spec(name):
        return pl.BlockSpec((tm, tiles[name]), lambda i, j: (i, j))

    def o_shape(name, dtype):
        return jax.ShapeDtypeStruct((rows, cols[name][2]), dtype)

    def per_head():
        return pl.BlockSpec((tm, n_heads), lambda i, j: (i, 0))

    flat_spec = pl.BlockSpec((tm * n_heads, head_dim), lambda i, j: (i, 0))
    flat_shape = jax.ShapeDtypeStruct((rows * n_heads, head_dim), F32)
    out_specs = [o_spec("q"), o_spec("k"), o_spec("v"), flat_spec, flat_spec,
                 o_spec("xr"), o_spec("yr"), o_spec("ga"), o_spec("gr"), per_head()]
    out_shape = [o_shape("q", BF16), o_shape("k", BF16), o_shape("v", BF16),
                 flat_shape, flat_shape,
                 o_shape("xr", F32), o_shape("yr", F32), o_shape("ga", BF16), o_shape("gr", BF16),
                 jax.ShapeDtypeStruct((rows, n_heads), F32)]
    scratch = [pltpu.VMEM((tm, d_model), BF16)]
    if seq is not None:
        out_specs.append(pl.BlockSpec((tm, LANES), lambda i, j: (i, 0)))
        out_shape.append(jax.ShapeDtypeStruct((rows, LANES), BF16))
        scratch.insert(0, pltpu.VMEM((1, LANES), F32))
    return pl.pallas_call(
        functools.partial(_in_proj_kernel,
                          tiles_per_seq=None if seq is None else seq // tm,
                          inv_scale=head_dim ** 0.5),
        grid=grid,
        in_specs=[pl.BlockSpec((tm, d_model), lambda i, j: (i, 0)),
                  pl.BlockSpec((1, d_model), lambda i, j: (0, 0))]
                 + [w_spec(name) for name in order]
                 + [pl.BlockSpec((LANES, d_model), lambda i, j: (0, 0)),
                    pl.BlockSpec((1, LANES), lambda i, j: (0, 0))],
        out_specs=out_specs,
        out_shape=out_shape,
        scratch_shapes=scratch,
        compiler_params=_params("arbitrary", "arbitrary"),
        name="in_proj",
    )(x, g, *[w[cols[name][0]] for name in order], w["f"], w["b_f"])


def _split3_bf16(x):
    hi = x.astype(BF16)
    r = x - hi.astype(F32)
    mid = r.astype(BF16)
    lo = (r - mid.astype(F32)).astype(BF16)
    return hi.astype(F32), mid.astype(F32), lo.astype(F32)


def _attn_prompt_kernel(q_ref, k_ref, v_ref, c_ref, o_ref, qa_s, ka_s, m_s, l_s, acc_s, *,
                        tq, scale, n_heads):
    seq, head_dim = q_ref.shape
    h = pl.program_id(1)
    src = lax.broadcasted_iota(jnp.int32, (LANES, LANES), 0)
    dst = lax.broadcasted_iota(jnp.int32, (LANES, LANES), 1)
    one_src = src == 3 * n_heads
    part_of_dst = jnp.where(dst < 3, dst, dst - 3) * n_heads + h
    sel_q = jnp.where((dst < 3) & (src == part_of_dst), 1.0,
                      jnp.where((dst >= 3) & (dst < 6) & one_src, 1.0, 0.0))
    sel_k = jnp.where((dst >= 3) & (dst < 6) & (src == part_of_dst), -1.0,
                      jnp.where((dst < 3) & one_src, 1.0, 0.0))
    c_rows = c_ref[...]
    qa_s[:, :head_dim] = q_ref[...]
    qa_s[:, head_dim:] = _dot(c_rows, sel_q.astype(BF16)).astype(BF16)
    ka_s[:, :head_dim] = k_ref[...]
    ka_s[:, head_dim:] = _dot(c_rows, sel_k.astype(BF16)).astype(BF16)

    sigma = scale * math.log2(math.e)
    row = lax.broadcasted_iota(jnp.int32, (tq, tq), 0)
    col = lax.broadcasted_iota(jnp.int32, (tq, tq), 1)
    visible = col <= row
    groups = tq // LANES
    n_tiles = seq // tq
    for j in range(n_tiles):
        s_all = _dot_nt(qa_s[j * tq:, :], ka_s[j * tq:(j + 1) * tq, :]) * sigma
        v_j = v_ref[j * tq:(j + 1) * tq, :]
        for i in range(j, n_tiles):
            rows = slice(i * tq, (i + 1) * tq)
            s = s_all[(i - j) * tq:(i - j + 1) * tq, :]
            if i == j:
                s = jnp.where(visible, s, NEG_BIG)
            blk_max = jnp.broadcast_to(jnp.max(s, axis=-1, keepdims=True), (tq, LANES))
            m_new = blk_max if j == 0 else jnp.maximum(m_s[rows, :], blk_max)
            parts = [jnp.exp2(s[:, g * LANES:(g + 1) * LANES] - m_new) for g in range(groups)]
            l_new = parts[0]
            for p in parts[1:]:
                l_new = l_new + p
            pv = _dot(jnp.concatenate([p.astype(BF16) for p in parts], axis=1), v_j)
            if j > 0:
                alpha = jnp.exp2(m_s[rows, :] - m_new)
                l_new = alpha * l_s[rows, :] + l_new
                pv = alpha * acc_s[rows, :] + pv
            if i == j:
                l = jnp.sum(l_new, axis=-1, keepdims=True)
                o_ref[rows, :] = (pv / l).astype(o_ref.dtype)
            else:
                m_s[rows, :] = m_new
                l_s[rows, :] = l_new
                acc_s[rows, :] = pv


def _attn_prompt(q, k, v, c, batch, seq, n_heads, head_dim, tq):
    assert head_dim == LANES, "per-row softmax state is kept one vreg lane group wide"
    head_spec = pl.BlockSpec((seq, head_dim), lambda b, h: (b, h))
    return pl.pallas_call(
        functools.partial(_attn_prompt_kernel, tq=tq, scale=head_dim ** -0.5, n_heads=n_heads),
        grid=(batch, n_heads),
        in_specs=[head_spec, head_spec, head_spec,
                  pl.BlockSpec((seq, LANES), lambda b, h: (b, 0))],
        out_specs=head_spec,
        out_shape=jax.ShapeDtypeStruct(q.shape, BF16),
        scratch_shapes=[pltpu.VMEM((seq, head_dim + LANES), BF16),
                        pltpu.VMEM((seq, head_dim + LANES), BF16),
                        pltpu.VMEM((seq, LANES), F32),
                        pltpu.VMEM((seq, LANES), F32),
                        pltpu.VMEM((seq, head_dim), F32)],
        compiler_params=_params("parallel", "parallel"),
        name="attn_prompt",
    )(q, k, v, c)


def _paged_forget_bias(lf_pages, lf_new, n_heads):
    rows, page = lf_pages.shape
    n_pages = rows // n_heads
    within = _prefix_sum(lf_pages, axis=1)
    totals = jnp.broadcast_to(within[:, page - 1:], (rows, page))
    upto = _prefix_sum(totals, axis=0, start=n_heads)
    c_past = within + (upto - totals)
    c_new = upto[rows - n_heads:, 0:1] + lf_new
    bias = jnp.concatenate([c_new] * n_pages, axis=0) - c_past
    src = lax.broadcasted_iota(jnp.int32, (page, page * n_heads), 0)
    dst = lax.broadcasted_iota(jnp.int32, (page, page * n_heads), 1)
    expand = jnp.where(dst // n_heads == src, 1.0, 0.0).astype(BF16)
    parts = jnp.concatenate(_split3_bf16(bias), axis=0).astype(BF16)
    wide = _dot(parts, expand)
    return wide[:rows] + wide[rows:2 * rows] + wide[2 * rows:]


def _paged_attn_step(step, q_ref, kn_ref, vn_ref, lfn_ref, k_refs, v_refs, load_lf_pages, o_ref,
                     s_s, bias_s, m_s, l_s, acc_s, *, steps_per_seq, scale):
    pps = len(k_refs)
    page, n_heads, head_dim = k_refs[0].shape[1:]
    pw = page * n_heads
    part = lax.rem(step, steps_per_seq)

    head_of_row = lax.broadcasted_iota(jnp.int32, (n_heads, pw), 0)
    lane = lax.broadcasted_iota(jnp.int32, (n_heads, pw), 1)
    on_diag = head_of_row == lane % n_heads
    q = q_ref[0]

    @pl.when(part == 0)
    def _():
        bias_s[...] = _paged_forget_bias(
            jnp.concatenate(load_lf_pages(), axis=0), lfn_ref[0], n_heads)
        kn = kn_ref[0].astype(BF16).astype(F32)
        m_s[...] = jnp.sum(q.astype(F32) * kn, axis=-1, keepdims=True) * scale
        l_s[...] = jnp.ones_like(l_s)
        acc_s[...] = vn_ref[0].astype(BF16).astype(F32)

    for p in range(pps):
        kf = k_refs[p][0].reshape(pw, head_dim).astype(BF16)
        first = pl.multiple_of((part * pps + p) * n_heads, n_heads)
        s = _dot_nt(q, kf) * scale + bias_s[pl.ds(first, n_heads), :]
        s_s[:, p * pw:(p + 1) * pw] = jnp.where(on_diag, s, NEG_BIG)

    s_all = s_s[...]
    m_old = m_s[...]
    m_new = jnp.maximum(m_old, jnp.max(s_all, axis=-1, keepdims=True))
    p_all = jnp.exp(s_all - m_new)
    alpha = jnp.exp(m_old - m_new)
    l_s[...] = alpha * l_s[...] + jnp.sum(p_all, axis=-1, keepdims=True)
    m_s[...] = m_new
    p16 = p_all.astype(BF16)
    acc = alpha * acc_s[...]
    for p in range(pps):
        vf = v_refs[p][0].reshape(pw, head_dim).astype(BF16)
        acc = acc + _dot(p16[:, p * pw:(p + 1) * pw], vf)
    acc_s[...] = acc

    @pl.when(part == steps_per_seq - 1)
    def _():
        o_ref[0] = (acc_s[...] / l_s[...]).astype(o_ref.dtype)


def _attn_sample_kernel(pt_ref, q_ref, kn_ref, vn_ref, lfn_ref, lf_all_ref, *refs,
                        pages_per_step, n_pages):
    k_refs = refs[:pages_per_step]
    v_refs = refs[pages_per_step:2 * pages_per_step]
    o_ref = refs[2 * pages_per_step]
    scratch = refs[2 * pages_per_step + 1:]
    b = pl.program_id(0)

    def load_lf_pages():
        return [lf_all_ref[pt_ref[b, p]] for p in range(n_pages)]

    _paged_attn_step(b * pl.num_programs(1) + pl.program_id(1),
                     q_ref, kn_ref, vn_ref, lfn_ref, k_refs, v_refs, load_lf_pages, o_ref,
                     *scratch,
                     steps_per_seq=n_pages // pages_per_step, scale=q_ref.shape[2] ** -0.5)


def _attn_sample(q, k_new, v_new, lf_new, cache_k, cache_v, cache_lf, page_table, pages_per_step):
    batch, n_pages = page_table.shape
    _, page, n_heads, head_dim = cache_k.shape
    pw = page * n_heads
    steps_per_seq = n_pages // pages_per_step
    assert page == LANES, "one page of log-forget values per head is one lane row"

    def row_spec(shape):
        return pl.BlockSpec((1,) + shape, lambda b, s, pt: (b, 0, 0))

    def page_spec(p):
        return pl.BlockSpec((1, page, n_heads, head_dim),
                            lambda b, s, pt: (pt[b, s * pages_per_step + p], 0, 0, 0))

    lf_spec = pl.BlockSpec(cache_lf.shape, lambda b, s, pt: (0, 0, 0),
                           pipeline_mode=pl.Buffered(1))

    return pl.pallas_call(
        functools.partial(_attn_sample_kernel, pages_per_step=pages_per_step, n_pages=n_pages),
        grid_spec=pltpu.PrefetchScalarGridSpec(
            num_scalar_prefetch=1,
            grid=(batch, steps_per_seq),
            in_specs=[row_spec((n_heads, head_dim))] * 3 + [row_spec((n_heads, 1)), lf_spec]
                     + [page_spec(p) for p in range(pages_per_step)] * 2,
            out_specs=row_spec((n_heads, head_dim)),
            scratch_shapes=[pltpu.VMEM((n_heads, pages_per_step * pw), F32),
                            pltpu.VMEM((n_pages * n_heads, pw), F32),
                            pltpu.VMEM((n_heads, 1), F32),
                            pltpu.VMEM((n_heads, 1), F32),
                            pltpu.VMEM((n_heads, head_dim), F32)],
        ),
        out_shape=jax.ShapeDtypeStruct((batch, n_heads, head_dim), BF16),
        compiler_params=_params("arbitrary", "arbitrary"),
        name="attn_sample",
    )(page_table, q, k_new, v_new, lf_new, cache_lf,
      *([cache_k] * pages_per_step), *([cache_v] * pages_per_step))


def _rglru_gates(conv, r_pre, i_pre, b_rg, b_ig, neg_c_softplus):
    r = jax.nn.sigmoid(r_pre + b_rg)
    i_g = jax.nn.sigmoid(i_pre + b_ig)
    log_a = r * neg_c_softplus
    a = jnp.exp(log_a)
    b = jnp.sqrt(-jnp.tanh(log_a) * (1.0 + a * a)) * (i_g * conv)
    return a, b


def _rglru_prompt_kernel(xr_ref, yr_ref, cw_ref, cb_ref, wrg_ref, wig_ref, brg_ref, big_ref,
                         lam_ref, o_ref, hl_ref, xc_s, conv_s, rp_s, ip_s, a_s, b_s, h_s):
    c = pl.program_id(0)
    nb, tc, n_blocks, bw = xr_ref.shape
    halo = CONV_WIDTH - 1
    rows = nb * tc

    @pl.when(c == 0)
    def _():
        xc_s[:, 0:halo] = jnp.zeros((nb, halo, n_blocks, bw), F32)
        h_s[...] = jnp.zeros_like(h_s)

    @pl.when(c > 0)
    def _():
        xc_s[:, 0:halo] = xc_s[:, tc:tc + halo]

    xc_s[:, halo:halo + tc] = xr_ref[...]

    conv = cb_ref[...] + xc_s[:, 0:tc] * cw_ref[0]
    for i in range(1, CONV_WIDTH):
        conv = conv + xc_s[:, i:i + tc] * cw_ref[i]

    conv_s[...] = conv.reshape(rows * n_blocks, bw)
    for n in range(n_blocks):
        xn = conv_s[pl.ds(n, rows, stride=n_blocks), :].astype(BF16)
        rp_s[pl.ds(n, rows, stride=n_blocks), :] = _dot(xn, wrg_ref[n])
        ip_s[pl.ds(n, rows, stride=n_blocks), :] = _dot(xn, wig_ref[n])

    neg_c_softplus = -LRU_C * _softplus(-lam_ref[...])
    a, b = _rglru_gates(conv,
                        rp_s[...].reshape(nb, tc, n_blocks, bw),
                        ip_s[...].reshape(nb, tc, n_blocks, bw),
                        brg_ref[...], big_ref[...], neg_c_softplus)
    a_s[...] = a
    b_s[...] = b

    def step(t, hs):
        new = tuple(a_s[j, t] * hs[j] + b_s[j, t] for j in range(nb))
        for j in range(nb):
            b_s[j, t] = new[j]
        return new

    hs = lax.fori_loop(0, tc, step, tuple(h_s[j] for j in range(nb)), unroll=8)
    for j in range(nb):
        h_s[j] = hs[j]

    conv_s[...] = (b_s[...] * _gelu_tanh(yr_ref[...])).reshape(rows * n_blocks, bw)
    for n in range(n_blocks):
        o_ref[:, :, n * bw:(n + 1) * bw] = (
            conv_s[pl.ds(n, rows, stride=n_blocks), :].reshape(nb, tc, bw).astype(o_ref.dtype))
    hl_ref[...] = h_s[...]


def _rglru_prompt(xr, yr, w, batch, seq, tc):
    n_blocks, bw = w["n_rnn_blocks"], w["rnn_block"]
    xr4 = xr.reshape(batch, seq, n_blocks, bw)
    yr4 = yr.reshape(batch, seq, n_blocks, bw)
    rows = batch * tc
    halo = CONV_WIDTH - 1

    def const(shape):
        return pl.BlockSpec(shape, lambda c: (0,) * len(shape))

    seq_spec = pl.BlockSpec((batch, tc, n_blocks, bw), lambda c: (0, c, 0, 0))
    out, h_last = pl.pallas_call(
        _rglru_prompt_kernel,
        grid=(seq // tc,),
        in_specs=[seq_spec, seq_spec,
                  const((CONV_WIDTH, n_blocks, bw)), const((n_blocks, bw)),
                  const((n_blocks, bw, bw)), const((n_blocks, bw, bw)),
                  const((n_blocks, bw)), const((n_blocks, bw)), const((n_blocks, bw))],
        out_specs=[pl.BlockSpec((batch, tc, n_blocks * bw), lambda c: (0, c, 0)),
                   const((batch, n_blocks, bw))],
        out_shape=[jax.ShapeDtypeStruct((batch, seq, n_blocks * bw), BF16),
                   jax.ShapeDtypeStruct((batch, n_blocks, bw), F32)],
        scratch_shapes=[pltpu.VMEM((batch, tc + halo, n_blocks, bw), F32),
                        pltpu.VMEM((rows * n_blocks, bw), F32),
                        pltpu.VMEM((rows * n_blocks, bw), F32),
                        pltpu.VMEM((rows * n_blocks, bw), F32),
                        pltpu.VMEM((batch, tc, n_blocks, bw), F32),
                        pltpu.VMEM((batch, tc, n_blocks, bw), F32),
                        pltpu.VMEM((batch, n_blocks, bw), F32)],
        compiler_params=_params("arbitrary"),
        name="rglru_prompt",
    )(xr4, yr4, w["conv_w3"], w["conv_b2"], w["w_rg"], w["w_ig"], w["b_rg2"], w["b_ig2"],
      w["lam2"])
    return out.reshape(batch * seq, n_blocks * bw), h_last.reshape(batch, n_blocks * bw)


def _rglru_sample_kernel(xr_ref, yr_ref, buf_ref, h0_ref, cw_ref, cb_ref, wrg_ref, wig_ref,
                         brg_ref, big_ref, lam_ref, o_ref, h_ref, nbuf_ref):
    halo = CONV_WIDTH - 1
    n_blocks, bw = wrg_ref.shape[0], wrg_ref.shape[1]
    xr = xr_ref[...]
    conv = cb_ref[...] + xr * cw_ref[halo:halo + 1, :]
    for i in range(halo):
        conv = conv + buf_ref[i] * cw_ref[i:i + 1, :]
    r_parts, i_parts = [], []
    for n in range(n_blocks):
        xn = conv[:, n * bw:(n + 1) * bw].astype(BF16)
        r_parts.append(_dot(xn, wrg_ref[n]))
        i_parts.append(_dot(xn, wig_ref[n]))
    neg_c_softplus = -LRU_C * _softplus(-lam_ref[...])
    a, b = _rglru_gates(conv, jnp.concatenate(r_parts, axis=1), jnp.concatenate(i_parts, axis=1),
                        brg_ref[...], big_ref[...], neg_c_softplus)
    h = a * h0_ref[...] + b
    h_ref[...] = h
    o_ref[...] = h * _gelu_tanh(yr_ref[...])
    for i in range(halo - 1):
        nbuf_ref[i] = buf_ref[i + 1]
    nbuf_ref[halo - 1] = xr


def _rglru_sample(xr, yr, conv_buf_t, h0, w):
    rows, width = xr.shape
    halo = CONV_WIDTH - 1
    return pl.pallas_call(
        _rglru_sample_kernel,
        out_shape=[jax.ShapeDtypeStruct((rows, width), F32),
                   jax.ShapeDtypeStruct((rows, width), F32),
                   jax.ShapeDtypeStruct((halo, rows, width), F32)],
        compiler_params=pltpu.CompilerParams(vmem_limit_bytes=VMEM_LIMIT_BYTES),
        name="rglru_sample",
    )(xr, yr, conv_buf_t, h0, w["conv_w"], w["conv_b"], w["w_rg"], w["w_ig"],
      w["b_rg"], w["b_ig"], w["lam"])


def _out_proj_kernel(attn_ref, rnn_ref, ga_ref, gr_ref, x_ref, woa_ref, wor_ref, wout_ref,
                     gpm_ref, gpf_ref, h_o, hn_o, *, n_parts):
    part_rows = x_ref.shape[0] // n_parts
    for part in range(n_parts):
        r = slice(part * part_rows, (part + 1) * part_rows)
        o = (jax.nn.sigmoid(ga_ref[r, :].astype(F32))
             * _dot(attn_ref[r, :].astype(BF16), woa_ref[...])
             + jax.nn.sigmoid(gr_ref[r, :].astype(F32))
             * _dot(rnn_ref[r, :].astype(BF16), wor_ref[...]))
        mix = _dot(o.astype(BF16), wout_ref[...])
        h = x_ref[r, :] + _rms_norm(mix, gpm_ref[...])
        h_o[r, :] = h
        hn_o[r, :] = _rms_norm(h, gpf_ref[...]).astype(hn_o.dtype)


def _out_proj(attn, rnn, ga, gr, x, w, tm, n_parts=1):
    rows, d_model = x.shape

    def row(width):
        return pl.BlockSpec((tm, width), lambda i: (i, 0))

    def const(shape):
        return pl.BlockSpec(shape, lambda i: (0, 0), pipeline_mode=pl.Buffered(1))

    return pl.pallas_call(
        functools.partial(_out_proj_kernel, n_parts=n_parts),
        grid=(rows // tm,),
        in_specs=[row(attn.shape[1]), row(rnn.shape[1]), row(d_model), row(d_model), row(d_model),
                  const(w["o_attn"].shape), const(w["o_rnn"].shape), const(w["out"].shape),
                  const((1, d_model)), const((1, d_model))],
        out_specs=[row(d_model), row(d_model)],
        out_shape=[jax.ShapeDtypeStruct((rows, d_model), F32),
                   jax.ShapeDtypeStruct((rows, d_model), BF16)],
        compiler_params=_params("parallel"),
        name="out_proj",
    )(attn, rnn, ga, gr, x, w["o_attn"], w["o_rnn"], w["out"], w["g_post_mix"], w["g_pre_ffn"])


def _ffn_kernel(hn_ref, h_ref, wg_ref, wu_ref, wd_ref, g_ref, y_o, acc_s, *, n_ff_steps):
    j = pl.program_id(1)

    @pl.when(j == 0)
    def _():
        acc_s[...] = jnp.zeros_like(acc_s)

    @pl.when(j < n_ff_steps)
    def _():
        hn = hn_ref[...]
        gate = _dot(hn, wg_ref[...])
        up = _dot(hn, wu_ref[...])
        acc_s[...] += _dot((jax.nn.silu(gate) * up).astype(BF16), wd_ref[...])

    @pl.when(j >= n_ff_steps)
    def _():
        part_rows = h_ref.shape[0]
        first = pl.multiple_of((j - n_ff_steps) * part_rows, part_rows)
        y_o[...] = h_ref[...] + _rms_norm(acc_s[pl.ds(first, part_rows), :], g_ref[...])


def _ffn(hn, h, w, tm, tf, n_parts):
    rows, d_model = h.shape
    n_ff_steps = w["gate"].shape[1] // tf
    part_rows = tm // n_parts

    def ff_tile(j):
        return jnp.minimum(j, n_ff_steps - 1)

    def part(i, j):
        return i * n_parts + jnp.maximum(j - n_ff_steps, 0)

    return pl.pallas_call(
        functools.partial(_ffn_kernel, n_ff_steps=n_ff_steps),
        grid=(rows // tm, n_ff_steps + n_parts),
        in_specs=[pl.BlockSpec((tm, d_model), lambda i, j: (i, 0)),
                  pl.BlockSpec((part_rows, d_model), lambda i, j: (part(i, j), 0)),
                  pl.BlockSpec((d_model, tf), lambda i, j: (0, ff_tile(j))),
                  pl.BlockSpec((d_model, tf), lambda i, j: (0, ff_tile(j))),
                  pl.BlockSpec((tf, d_model), lambda i, j: (ff_tile(j), 0)),
                  pl.BlockSpec((1, d_model), lambda i, j: (0, 0))],
        out_specs=pl.BlockSpec((part_rows, d_model), lambda i, j: (part(i, j), 0)),
        out_shape=jax.ShapeDtypeStruct((rows, d_model), F32),
        scratch_shapes=[pltpu.VMEM((tm, d_model), F32)],
        compiler_params=_params("parallel", "arbitrary"),
        name="ffn",
    )(hn, h, w["gate"], w["up"], w["down"], w["g_post_ffn"])


def _layer_weights(l, n_heads, head_dim, g_pre_mix, w_in, b_f, conv_w, conv_b, w_rg, b_rg, w_ig,
                   b_ig, lru_lambda, w_o_attn, w_o_rnn, w_out, g_post_mix, g_pre_ffn, w_gate,
                   w_up, w_down, g_post_ffn):
    d_model = w_in.shape[1]
    n_blocks, bw = w_rg.shape[1], w_rg.shape[2]
    d_rnn = n_blocks * bw
    aw = n_heads * head_dim
    wi_t = jnp.swapaxes(w_in[l], 0, 1)
    f_lo, f_hi = 3 * aw, 3 * aw + n_heads
    cols = {
        "qkv": _cast_rows(wi_t, 0, f_lo, 512),
        "rest": _cast_rows(wi_t, f_hi, wi_t.shape[0] - f_hi, 512),
        "f": jnp.pad(wi_t[f_lo:f_hi].astype(BF16), ((0, LANES - n_heads), (0, 0))),
        "b_f": jnp.pad(b_f[l].reshape(1, n_heads), ((0, 0), (0, LANES - n_heads))),
        "widths": {"q": aw, "k": aw, "v": aw, "xr": d_rnn, "yr": d_rnn,
                   "ga": d_model, "gr": d_model},
        "n_heads": n_heads,
        "head_dim": head_dim,
    }
    rnn = {
        "n_rnn_blocks": n_blocks, "rnn_block": bw,
        "conv_w": conv_w[l], "conv_b": conv_b[l].reshape(1, d_rnn),
        "b_rg": b_rg[l].reshape(1, d_rnn), "b_ig": b_ig[l].reshape(1, d_rnn),
        "lam": lru_lambda[l].reshape(1, d_rnn),
        "conv_w3": conv_w[l].reshape(CONV_WIDTH, n_blocks, bw),
        "conv_b2": conv_b[l].reshape(n_blocks, bw),
        "b_rg2": b_rg[l].reshape(n_blocks, bw), "b_ig2": b_ig[l].reshape(n_blocks, bw),
        "lam2": lru_lambda[l].reshape(n_blocks, bw),
        "w_rg": w_rg[l].astype(BF16), "w_ig": w_ig[l].astype(BF16),
    }
    mix = {"o_attn": w_o_attn[l].astype(BF16), "o_rnn": w_o_rnn[l].astype(BF16),
           "out": w_out[l].astype(BF16),
           "g_post_mix": g_post_mix[l].reshape(1, d_model),
           "g_pre_ffn": g_pre_ffn[l].reshape(1, d_model)}
    ffn = {"gate": w_gate[l].astype(BF16), "up": w_up[l].astype(BF16),
           "down": w_down[l].astype(BF16), "g_post_ffn": g_post_ffn[l].reshape(1, d_model)}
    return g_pre_mix[l].reshape(1, d_model), cols, rnn, mix, ffn


def kernel(x_prompt, x_sample, cache_k, cache_v, cache_logf, state_h, state_conv, page_table,
           g_pre_mix, w_in, b_f, conv_w, conv_b, w_rg, b_rg, w_ig, b_ig, lru_lambda,
           w_o_attn, w_o_rnn, w_out, g_post_mix, g_pre_ffn, w_gate, w_up, w_down, g_post_ffn):
    batch, seq, d_model = x_prompt.shape
    dec_batch, dec_seq, _ = x_sample.shape
    assert dec_seq == 1, "sample group carries one new token per sequence"
    depth, n_phys, page, n_heads, head_dim = cache_k.shape
    aw = n_heads * head_dim
    d_rnn = state_h.shape[2]
    n_pages = page_table.shape[1]
    assert LANES % n_heads == 0 and (page * n_heads) % LANES == 0

    xp = x_prompt.reshape(batch * seq, d_model)
    xs = x_sample.reshape(dec_batch, d_model)
    outs = {name: [] for name in ("kp", "vp", "lp", "hp", "cp", "ks", "vs", "ls", "hs", "cs")}

    for l in range(depth):
        g_in, w_cols, w_rnn, w_mix, w_ffn = _layer_weights(
            l, n_heads, head_dim, g_pre_mix, w_in, b_f, conv_w, conv_b, w_rg, b_rg, w_ig, b_ig,
            lru_lambda, w_o_attn, w_o_rnn, w_out, g_post_mix, g_pre_ffn, w_gate, w_up, w_down,
            g_post_ffn)

        q, k16, v16, k, v, xr, yr, ga, gr, lf, c = _in_proj(xp, g_in, w_cols, tm=512,
                                                            n_col_steps=4, seq=seq)
        qs, _, _, ks, vs, xrs, yrs, gas, grs, lfs = _in_proj(xs, g_in, w_cols, tm=dec_batch,
                                                             n_col_steps=4)
        ks = ks.reshape(dec_batch, n_heads, head_dim)
        vs = vs.reshape(dec_batch, n_heads, head_dim)

        attn = _attn_prompt(q, k16, v16, c, batch, seq, n_heads, head_dim, tq=256)
        rnn, h_last = _rglru_prompt(xr, yr, w_rnn, batch, seq, tc=128)
        h, hn = _out_proj(attn, rnn, ga, gr, xp, w_mix, tm=512, n_parts=2)
        xp_next = _ffn(hn, h, w_ffn, tm=1024, tf=512, n_parts=2)
        outs["kp"].append(k.reshape(batch, seq, n_heads, head_dim))
        outs["vp"].append(v.reshape(batch, seq, n_heads, head_dim))
        outs["lp"].append(lf.reshape(batch, seq, n_heads))
        outs["hp"].append(h_last)
        outs["cp"].append(xr.reshape(batch, seq, d_rnn)[:, seq - (CONV_WIDTH - 1):])

        attn_s = _attn_sample(
            qs.reshape(dec_batch, n_heads, head_dim), ks, vs, lfs.reshape(dec_batch, n_heads, 1),
            cache_k[l], cache_v[l], jnp.swapaxes(cache_logf[l], 1, 2),
            page_table, pages_per_step=n_pages)
        rnn, h_new, buf_new = _rglru_sample(xrs, yrs, jnp.swapaxes(state_conv[l], 0, 1),
                                            state_h[l], w_rnn)
        h, hn = _out_proj(attn_s.reshape(dec_batch, aw), rnn, gas, grs, xs, w_mix, tm=dec_batch)
        xs_next = _ffn(hn, h, w_ffn, tm=dec_batch, tf=512, n_parts=1)
        outs["ks"].append(ks.reshape(dec_batch, 1, n_heads, head_dim))
        outs["vs"].append(vs.reshape(dec_batch, 1, n_heads, head_dim))
        outs["ls"].append(lfs.reshape(dec_batch, 1, n_heads))
        outs["hs"].append(h_new)
        outs["cs"].append(jnp.swapaxes(buf_new, 0, 1))

        xp, xs = xp_next, xs_next

    st = {name: jnp.stack(vals) for name, vals in outs.items()}
    return (xp.reshape(batch, seq, d_model), xs.reshape(dec_batch, 1, d_model),
            st["kp"], st["vp"], st["lp"], st["hp"], st["cp"],
            st["ks"], st["vs"], st["ls"], st["hs"], st["cs"])
```

```python
import functools
import math

import jax
import jax.numpy as jnp
from jax import lax
from jax.experimental import pallas as pl
from jax.experimental.pallas import tpu as pltpu

F32 = jnp.float32
BF16 = jnp.bfloat16

EPS = 1e-6
LRU_C = 8.0
CONV_WIDTH = 4
NEG_BIG = -1e30

SUBLANES = 8
LANES = 128
VMEM_LIMIT_BYTES = 56 * 1024 * 1024


def _params(*semantics):
    return pltpu.CompilerParams(dimension_semantics=semantics,
                                vmem_limit_bytes=VMEM_LIMIT_BYTES)


def _rms_norm(x, g):
    inv = lax.rsqrt(jnp.mean(x * x, axis=-1, keepdims=True) + EPS)
    return (x * inv) * g


def _log_sigmoid(x):
    return jnp.minimum(x, 0.0) - jnp.log1p(jnp.exp(-jnp.abs(x)))


def _softplus(x):
    return jnp.maximum(x, 0.0) + jnp.log1p(jnp.exp(-jnp.abs(x)))


def _gelu_tanh(x):
    c = math.sqrt(2.0 / math.pi)
    return x * (0.5 * (1.0 + jnp.tanh(c * (x + 0.044715 * (x * x * x)))))


def _dot(a, b):
    return jnp.dot(a, b, preferred_element_type=F32)


def _dot_nt(a, b):
    return lax.dot_general(a, b, (((1,), (1,)), ((), ())), preferred_element_type=F32)


def _cast_kernel(x_ref, o_ref):
    o_ref[...] = x_ref[...].astype(o_ref.dtype)


def _cast_shifted_kernel(a_ref, b_ref, o_ref, *, shift):
    o_ref[...] = jnp.concatenate([a_ref[shift:, :], b_ref[:shift, :]], axis=0).astype(o_ref.dtype)


def _cast_rows(w_t, first_row, n_rows, tile):
    width = w_t.shape[1]
    assert n_rows % tile == 0 and tile % SUBLANES == 0
    shift = first_row % tile
    base = first_row // tile
    out_spec = pl.BlockSpec((tile, width), lambda j: (j, 0))
    out_shape = jax.ShapeDtypeStruct((n_rows, width), BF16)
    if shift == 0:
        return pl.pallas_call(
            _cast_kernel, grid=(n_rows // tile,),
            in_specs=[pl.BlockSpec((tile, width), lambda j: (base + j, 0))],
            out_specs=out_spec, out_shape=out_shape,
            compiler_params=_params("parallel"), name="cast",
        )(w_t)
    assert shift == SUBLANES and first_row + n_rows <= w_t.shape[0]
    per_tile = tile // SUBLANES
    return pl.pallas_call(
        functools.partial(_cast_shifted_kernel, shift=shift), grid=(n_rows // tile,),
        in_specs=[pl.BlockSpec((tile, width), lambda j: (base + j, 0)),
                  pl.BlockSpec((SUBLANES, width), lambda j: ((base + j + 1) * per_tile, 0))],
        out_specs=out_spec, out_shape=out_shape,
        compiler_params=_params("parallel"), name="cast_shifted",
    )(w_t, w_t)


def _prefix_sum(x, axis, start=1):
    idx = lax.broadcasted_iota(jnp.int32, x.shape, axis)
    shift = start
    while shift < x.shape[axis]:
        x = x + jnp.where(idx >= shift, pltpu.roll(x, shift, axis=axis), 0.0)
        shift *= 2
    return x


def _in_proj_kernel(x_ref, g_ref, wq, wk, wv, wxr, wyr, wga, wgr, wf, bf_ref,
                    q_o, k16_o, v16_o, k32_o, v32_o, xr_o, yr_o, ga_o, gr_o, lf_o, *rest,
                    tiles_per_seq, inv_scale):
    i, j = pl.program_id(0), pl.program_id(1)
    xn_s = rest[-1]
    n_heads = lf_o.shape[1]

    @pl.when(j == 0)
    def _():
        xn = _rms_norm(x_ref[...], g_ref[...]).astype(BF16)
        xn_s[...] = xn
        lf = _log_sigmoid(_dot_nt(xn, wf[...]) + bf_ref[...])
        lf_o[...] = lf[:, :n_heads]
        if tiles_per_seq is not None:
            c_o, carry_s = rest[0], rest[1]

            @pl.when(lax.rem(i, tiles_per_seq) == 0)
            def _():
                carry_s[...] = jnp.zeros_like(carry_s)

            c = _prefix_sum(lf, axis=0) + carry_s[...]
            carry_s[...] = c[c.shape[0] - 1:, :]
            hi, mid, lo = _split3_bf16(c * inv_scale)
            lane = lax.broadcasted_iota(jnp.int32, c.shape, 1)
            packed = jnp.where(
                lane < n_heads, hi,
                jnp.where(lane < 2 * n_heads, pltpu.roll(mid, n_heads, axis=1),
                          jnp.where(lane < 3 * n_heads, pltpu.roll(lo, 2 * n_heads, axis=1),
                                    jnp.where(lane == 3 * n_heads, 1.0, 0.0))))
            c_o[...] = packed.astype(c_o.dtype)

    xn = xn_s[...]
    for w, o in ((wq, q_o), (wxr, xr_o), (wyr, yr_o), (wga, ga_o), (wgr, gr_o)):
        o[...] = _dot_nt(xn, w[...]).astype(o.dtype)

    tm = xn.shape[0]
    n_heads = k32_o.shape[0] // tm
    head_dim = k32_o.shape[1]
    heads_per_step = wk.shape[0] // head_dim
    for w, o16, o32 in ((wk, k16_o, k32_o), (wv, v16_o, v32_o)):
        res = _dot_nt(xn, w[...])
        o16[...] = res.astype(o16.dtype)
        for hh in range(heads_per_step):
            head = j * heads_per_step + hh
            o32[pl.ds(head, tm, stride=n_heads), :] = res[:, hh * head_dim:(hh + 1) * head_dim]


def _in_proj(x, g, w, tm, n_col_steps, seq=None):
    rows, d_model = x.shape
    n_heads, head_dim = w["n_heads"], w["head_dim"]
    cols = {}
    for arr, names in (("qkv", ("q", "k", "v")), ("rest", ("xr", "yr", "ga", "gr"))):
        start = 0
        for name in names:
            cols[name] = (arr, start, w["widths"][name])
            start += w["widths"][name]
    order = ("q", "k", "v", "xr", "yr", "ga", "gr")
    tiles = {name: cols[name][2] // n_col_steps for name in order}
    grid = (rows // tm, n_col_steps)

    def w_spec(name):
        first_block = cols[name][1] // tiles[name]
        return pl.BlockSpec((tiles[name], d_model), lambda i, j: (first_block + j, 0))

    def o_spec(name):
        return pl.BlockSpec((tm, tiles[name]), lambda i, j: (i, j))

    def o_shape(name, dtype):
        return jax.ShapeDtypeStruct((rows, cols[name][2]), dtype)

    def per_head():
        return pl.BlockSpec((tm, n_heads), lambda i, j: (i, 0))

    flat_spec = pl.BlockSpec((tm * n_heads, head_dim), lambda i, j: (i, 0))
    flat_shape = jax.ShapeDtypeStruct((rows * n_heads, head_dim), F32)
    out_specs = [o_spec("q"), o_spec("k"), o_spec("v"), flat_spec, flat_spec,
                 o_spec("xr"), o_spec("yr"), o_spec("ga"), o_spec("gr"), per_head()]
    out_shape = [o_shape("q", BF16), o_shape("k", BF16), o_shape("v", BF16),
                 flat_shape, flat_shape,
                 o_shape("xr", F32), o_shape("yr", F32), o_shape("ga", BF16), o_shape("gr", BF16),
                 jax.ShapeDtypeStruct((rows, n_heads), F32)]
    scratch = [pltpu.VMEM((tm, d_model), BF16)]
    if seq is not None:
        out_specs.append(pl.BlockSpec((tm, LANES), lambda i, j: (i, 0)))
        out_shape.append(jax.ShapeDtypeStruct((rows, LANES), BF16))
        scratch.insert(0, pltpu.VMEM((1, LANES), F32))
    return pl.pallas_call(
        functools.partial(_in_proj_kernel,
                          tiles_per_seq=None if seq is None else seq // tm,
                          inv_scale=head_dim ** 0.5),
        grid=grid,
        in_specs=[pl.BlockSpec((tm, d_model), lambda i, j: (i, 0)),
                  pl.BlockSpec((1, d_model), lambda i, j: (0, 0))]
                 + [w_spec(name) for name in order]
                 + [pl.BlockSpec((LANES, d_model), lambda i, j: (0, 0)),
                    pl.BlockSpec((1, LANES), lambda i, j: (0, 0))],
        out_specs=out_specs,
        out_shape=out_shape,
        scratch_shapes=scratch,
        compiler_params=_params("arbitrary", "arbitrary"),
        name="in_proj",
    )(x, g, *[w[cols[name][0]] for name in order], w["f"], w["b_f"])


def _split3_bf16(x):
    hi = x.astype(BF16)
    r = x - hi.astype(F32)
    mid = r.astype(BF16)
    lo = (r - mid.astype(F32)).astype(BF16)
    return hi.astype(F32), mid.astype(F32), lo.astype(F32)


def _attn_prompt_kernel(q_ref, k_ref, v_ref, c_ref, *rest, tq, scale, n_heads, n_casts):
    o_ref = rest[n_casts]
    for w_ref, w16_ref in zip(rest[:n_casts], rest[n_casts + 1:2 * n_casts + 1]):
        w16_ref[...] = w_ref[...].astype(w16_ref.dtype)
    qa_s, ka_s, m_s, l_s, acc_s = rest[2 * n_casts + 1:]
    seq, head_dim = q_ref.shape
    h = pl.program_id(1)
    src = lax.broadcasted_iota(jnp.int32, (LANES, LANES), 0)
    dst = lax.broadcasted_iota(jnp.int32, (LANES, LANES), 1)
    one_src = src == 3 * n_heads
    part_of_dst = jnp.where(dst < 3, dst, dst - 3) * n_heads + h
    sel_q = jnp.where((dst < 3) & (src == part_of_dst), 1.0,
                      jnp.where((dst >= 3) & (dst < 6) & one_src, 1.0, 0.0))
    sel_k = jnp.where((dst >= 3) & (dst < 6) & (src == part_of_dst), -1.0,
                      jnp.where((dst < 3) & one_src, 1.0, 0.0))
    c_rows = c_ref[...]
    qa_s[:, :head_dim] = q_ref[...]
    qa_s[:, head_dim:] = _dot(c_rows, sel_q.astype(BF16)).astype(BF16)
    ka_s[:, :head_dim] = k_ref[...]
    ka_s[:, head_dim:] = _dot(c_rows, sel_k.astype(BF16)).astype(BF16)

    sigma = scale * math.log2(math.e)
    row = lax.broadcasted_iota(jnp.int32, (tq, tq), 0)
    col = lax.broadcasted_iota(jnp.int32, (tq, tq), 1)
    visible = col <= row
    groups = tq // LANES
    n_tiles = seq // tq
    for j in range(n_tiles):
        s_all = _dot_nt(qa_s[j * tq:, :], ka_s[j * tq:(j + 1) * tq, :]) * sigma
        v_j = v_ref[j * tq:(j + 1) * tq, :]
        for i in range(j, n_tiles):
            rows = slice(i * tq, (i + 1) * tq)
            s = s_all[(i - j) * tq:(i - j + 1) * tq, :]
            if i == j:
                s = jnp.where(visible, s, NEG_BIG)
            blk_max = jnp.broadcast_to(jnp.max(s, axis=-1, keepdims=True), (tq, LANES))
            m_new = blk_max if j == 0 else jnp.maximum(m_s[rows, :], blk_max)
            parts = [jnp.exp2(s[:, g * LANES:(g + 1) * LANES] - m_new) for g in range(groups)]
            l_new = parts[0]
            for p in parts[1:]:
                l_new = l_new + p
            pv = _dot(jnp.concatenate([p.astype(BF16) for p in parts], axis=1), v_j)
            if j > 0:
                alpha = jnp.exp2(m_s[rows, :] - m_new)
                l_new = alpha * l_s[rows, :] + l_new
                pv = alpha * acc_s[rows, :] + pv
            if i == j:
                l = jnp.sum(l_new, axis=-1, keepdims=True)
                o_ref[rows, :] = (pv / l).astype(o_ref.dtype)
            else:
                m_s[rows, :] = m_new
                l_s[rows, :] = l_new
                acc_s[rows, :] = pv


def _attn_prompt(q, k, v, c, batch, seq, n_heads, head_dim, tq, weights_to_cast):
    assert head_dim == LANES, "per-row softmax state is kept one vreg lane group wide"
    head_spec = pl.BlockSpec((seq, head_dim), lambda b, h: (b, h))
    n_steps = batch * n_heads
    cast_specs = []
    for w in weights_to_cast:
        assert w.shape[0] % (n_steps * 2 * SUBLANES) == 0
        cast_specs.append(pl.BlockSpec((w.shape[0] // n_steps, w.shape[1]),
                                       lambda b, h: (b * n_heads + h, 0)))
    attn, *casts = pl.pallas_call(
        functools.partial(_attn_prompt_kernel, tq=tq, scale=head_dim ** -0.5, n_heads=n_heads,
                          n_casts=len(weights_to_cast)),
        grid=(batch, n_heads),
        in_specs=[head_spec, head_spec, head_spec,
                  pl.BlockSpec((seq, LANES), lambda b, h: (b, 0))] + cast_specs,
        out_specs=[head_spec] + cast_specs,
        out_shape=[jax.ShapeDtypeStruct(q.shape, BF16)]
                  + [jax.ShapeDtypeStruct(w.shape, BF16) for w in weights_to_cast],
        scratch_shapes=[pltpu.VMEM((seq, head_dim + LANES), BF16),
                        pltpu.VMEM((seq, head_dim + LANES), BF16),
                        pltpu.VMEM((seq, LANES), F32),
                        pltpu.VMEM((seq, LANES), F32),
                        pltpu.VMEM((seq, head_dim), F32)],
        compiler_params=_params("parallel", "parallel"),
        name="attn_prompt",
    )(q, k, v, c, *weights_to_cast)
    return attn, casts


def _paged_forget_bias(lf_pages, lf_new, n_heads):
    rows, page = lf_pages.shape
    n_pages = rows // n_heads
    within = _prefix_sum(lf_pages, axis=1)
    totals = jnp.broadcast_to(within[:, page - 1:], (rows, page))
    upto = _prefix_sum(totals, axis=0, start=n_heads)
    c_past = within + (upto - totals)
    c_new = upto[rows - n_heads:, 0:1] + lf_new
    bias = jnp.concatenate([c_new] * n_pages, axis=0) - c_past
    src = lax.broadcasted_iota(jnp.int32, (page, page * n_heads), 0)
    dst = lax.broadcasted_iota(jnp.int32, (page, page * n_heads), 1)
    expand = jnp.where(dst // n_heads == src, 1.0, 0.0).astype(BF16)
    parts = jnp.concatenate(_split3_bf16(bias), axis=0).astype(BF16)
    wide = _dot(parts, expand)
    return wide[:rows] + wide[rows:2 * rows] + wide[2 * rows:]


def _paged_attn_step(step, q_ref, kn_ref, vn_ref, lfn_ref, k_refs, v_refs, load_lf_pages, o_ref,
                     s_s, bias_s, m_s, l_s, acc_s, *, steps_per_seq, scale):
    pps = len(k_refs)
    page, n_heads, head_dim = k_refs[0].shape[1:]
    pw = page * n_heads
    part = lax.rem(step, steps_per_seq)

    head_of_row = lax.broadcasted_iota(jnp.int32, (n_heads, pw), 0)
    lane = lax.broadcasted_iota(jnp.int32, (n_heads, pw), 1)
    on_diag = head_of_row == lane % n_heads
    q = q_ref[0]

    @pl.when(part == 0)
    def _():
        bias_s[...] = _paged_forget_bias(
            jnp.concatenate(load_lf_pages(), axis=0), lfn_ref[0], n_heads)
        kn = kn_ref[0].astype(BF16).astype(F32)
        m_s[...] = jnp.sum(q.astype(F32) * kn, axis=-1, keepdims=True) * scale
        l_s[...] = jnp.ones_like(l_s)
        acc_s[...] = vn_ref[0].astype(BF16).astype(F32)

    for p in range(pps):
        kf = k_refs[p][0].reshape(pw, head_dim).astype(BF16)
        first = pl.multiple_of((part * pps + p) * n_heads, n_heads)
        s = _dot_nt(q, kf) * scale + bias_s[pl.ds(first, n_heads), :]
        s_s[:, p * pw:(p + 1) * pw] = jnp.where(on_diag, s, NEG_BIG)

    s_all = s_s[...]
    m_old = m_s[...]
    m_new = jnp.maximum(m_old, jnp.max(s_all, axis=-1, keepdims=True))
    p_all = jnp.exp(s_all - m_new)
    alpha = jnp.exp(m_old - m_new)
    l_s[...] = alpha * l_s[...] + jnp.sum(p_all, axis=-1, keepdims=True)
    m_s[...] = m_new
    p16 = p_all.astype(BF16)
    acc = alpha * acc_s[...]
    for p in range(pps):
        vf = v_refs[p][0].reshape(pw, head_dim).astype(BF16)
        acc = acc + _dot(p16[:, p * pw:(p + 1) * pw], vf)
    acc_s[...] = acc

    @pl.when(part == steps_per_seq - 1)
    def _():
        o_ref[0] = (acc_s[...] / l_s[...]).astype(o_ref.dtype)


def _attn_sample_kernel(pt_ref, q_ref, kn_ref, vn_ref, lfn_ref, lf_all_ref, *refs,
                        pages_per_step, n_pages):
    k_refs = refs[:pages_per_step]
    v_refs = refs[pages_per_step:2 * pages_per_step]
    o_ref = refs[2 * pages_per_step]
    scratch = refs[2 * pages_per_step + 1:]
    b = pl.program_id(0)

    def load_lf_pages():
        return [lf_all_ref[pt_ref[b, p]] for p in range(n_pages)]

    _paged_attn_step(b * pl.num_programs(1) + pl.program_id(1),
                     q_ref, kn_ref, vn_ref, lfn_ref, k_refs, v_refs, load_lf_pages, o_ref,
                     *scratch,
                     steps_per_seq=n_pages // pages_per_step, scale=q_ref.shape[2] ** -0.5)


def _attn_sample(q, k_new, v_new, lf_new, cache_k, cache_v, cache_lf, page_table, pages_per_step):
    batch, n_pages = page_table.shape
    _, page, n_heads, head_dim = cache_k.shape
    pw = page * n_heads
    steps_per_seq = n_pages // pages_per_step
    assert page == LANES, "one page of log-forget values per head is one lane row"

    def row_spec(shape):
        return pl.BlockSpec((1,) + shape, lambda b, s, pt: (b, 0, 0))

    def page_spec(p):
        return pl.BlockSpec((1, page, n_heads, head_dim),
                            lambda b, s, pt: (pt[b, s * pages_per_step + p], 0, 0, 0))

    lf_spec = pl.BlockSpec(cache_lf.shape, lambda b, s, pt: (0, 0, 0),
                           pipeline_mode=pl.Buffered(1))

    return pl.pallas_call(
        functools.partial(_attn_sample_kernel, pages_per_step=pages_per_step, n_pages=n_pages),
        grid_spec=pltpu.PrefetchScalarGridSpec(
            num_scalar_prefetch=1,
            grid=(batch, steps_per_seq),
            in_specs=[row_spec((n_heads, head_dim))] * 3 + [row_spec((n_heads, 1)), lf_spec]
                     + [page_spec(p) for p in range(pages_per_step)] * 2,
            out_specs=row_spec((n_heads, head_dim)),
            scratch_shapes=[pltpu.VMEM((n_heads, pages_per_step * pw), F32),
                            pltpu.VMEM((n_pages * n_heads, pw), F32),
                            pltpu.VMEM((n_heads, 1), F32),
                            pltpu.VMEM((n_heads, 1), F32),
                            pltpu.VMEM((n_heads, head_dim), F32)],
        ),
        out_shape=jax.ShapeDtypeStruct((batch, n_heads, head_dim), BF16),
        compiler_params=_params("arbitrary", "arbitrary"),
        name="attn_sample",
    )(page_table, q, k_new, v_new, lf_new, cache_lf,
      *([cache_k] * pages_per_step), *([cache_v] * pages_per_step))


def _rglru_gates(conv, r_pre, i_pre, b_rg, b_ig, neg_c_softplus):
    r = jax.nn.sigmoid(r_pre + b_rg)
    i_g = jax.nn.sigmoid(i_pre + b_ig)
    log_a = r * neg_c_softplus
    a = jnp.exp(log_a)
    b = jnp.sqrt(-jnp.tanh(log_a) * (1.0 + a * a)) * (i_g * conv)
    return a, b


def _rglru_prompt_kernel(xr_ref, yr_ref, cw_ref, cb_ref, wrg_ref, wig_ref, brg_ref, big_ref,
                         lam_ref, o_ref, hl_ref, xc_s, conv_s, rp_s, ip_s, a_s, b_s, h_s):
    c = pl.program_id(0)
    nb, tc, n_blocks, bw = xr_ref.shape
    halo = CONV_WIDTH - 1
    rows = nb * tc

    @pl.when(c == 0)
    def _():
        xc_s[:, 0:halo] = jnp.zeros((nb, halo, n_blocks, bw), F32)
        h_s[...] = jnp.zeros_like(h_s)

    @pl.when(c > 0)
    def _():
        xc_s[:, 0:halo] = xc_s[:, tc:tc + halo]

    xc_s[:, halo:halo + tc] = xr_ref[...]

    conv = cb_ref[...] + xc_s[:, 0:tc] * cw_ref[0]
    for i in range(1, CONV_WIDTH):
        conv = conv + xc_s[:, i:i + tc] * cw_ref[i]

    conv_s[...] = conv.reshape(rows * n_blocks, bw)
    for n in range(n_blocks):
        xn = conv_s[pl.ds(n, rows, stride=n_blocks), :].astype(BF16)
        rp_s[pl.ds(n, rows, stride=n_blocks), :] = _dot(xn, wrg_ref[n])
        ip_s[pl.ds(n, rows, stride=n_blocks), :] = _dot(xn, wig_ref[n])

    neg_c_softplus = -LRU_C * _softplus(-lam_ref[...])
    a, b = _rglru_gates(conv,
                        rp_s[...].reshape(nb, tc, n_blocks, bw),
                        ip_s[...].reshape(nb, tc, n_blocks, bw),
                        brg_ref[...], big_ref[...], neg_c_softplus)
    a_s[...] = a
    b_s[...] = b

    def step(t, hs):
        new = tuple(a_s[j, t] * hs[j] + b_s[j, t] for j in range(nb))
        for j in range(nb):
            b_s[j, t] = new[j]
        return new

    hs = lax.fori_loop(0, tc, step, tuple(h_s[j] for j in range(nb)), unroll=8)
    for j in range(nb):
        h_s[j] = hs[j]

    conv_s[...] = (b_s[...] * _gelu_tanh(yr_ref[...])).reshape(rows * n_blocks, bw)
    for n in range(n_blocks):
        o_ref[:, :, n * bw:(n + 1) * bw] = (
            conv_s[pl.ds(n, rows, stride=n_blocks), :].reshape(nb, tc, bw).astype(o_ref.dtype))
    hl_ref[...] = h_s[...]


def _rglru_prompt(xr, yr, w, batch, seq, tc):
    n_blocks, bw = w["n_rnn_blocks"], w["rnn_block"]
    xr4 = xr.reshape(batch, seq, n_blocks, bw)
    yr4 = yr.reshape(batch, seq, n_blocks, bw)
    rows = batch * tc
    halo = CONV_WIDTH - 1

    def const(shape):
        return pl.BlockSpec(shape, lambda c: (0,) * len(shape))

    seq_spec = pl.BlockSpec((batch, tc, n_blocks, bw), lambda c: (0, c, 0, 0))
    out, h_last = pl.pallas_call(
        _rglru_prompt_kernel,
        grid=(seq // tc,),
        in_specs=[seq_spec, seq_spec,
                  const((CONV_WIDTH, n_blocks, bw)), const((n_blocks, bw)),
                  const((n_blocks, bw, bw)), const((n_blocks, bw, bw)),
                  const((n_blocks, bw)), const((n_blocks, bw)), const((n_blocks, bw))],
        out_specs=[pl.BlockSpec((batch, tc, n_blocks * bw), lambda c: (0, c, 0)),
                   const((batch, n_blocks, bw))],
        out_shape=[jax.ShapeDtypeStruct((batch, seq, n_blocks * bw), BF16),
                   jax.ShapeDtypeStruct((batch, n_blocks, bw), F32)],
        scratch_shapes=[pltpu.VMEM((batch, tc + halo, n_blocks, bw), F32),
                        pltpu.VMEM((rows * n_blocks, bw), F32),
                        pltpu.VMEM((rows * n_blocks, bw), F32),
                        pltpu.VMEM((rows * n_blocks, bw), F32),
                        pltpu.VMEM((batch, tc, n_blocks, bw), F32),
                        pltpu.VMEM((batch, tc, n_blocks, bw), F32),
                        pltpu.VMEM((batch, n_blocks, bw), F32)],
        compiler_params=_params("arbitrary"),
        name="rglru_prompt",
    )(xr4, yr4, w["conv_w3"], w["conv_b2"], w["w_rg"], w["w_ig"], w["b_rg2"], w["b_ig2"],
      w["lam2"])
    return out.reshape(batch * seq, n_blocks * bw), h_last.reshape(batch, n_blocks * bw)


def _rglru_sample_kernel(xr_ref, yr_ref, buf_ref, h0_ref, cw_ref, cb_ref, wrg_ref, wig_ref,
                         brg_ref, big_ref, lam_ref, o_ref, h_ref, nbuf_ref):
    halo = CONV_WIDTH - 1
    n_blocks, bw = wrg_ref.shape[0], wrg_ref.shape[1]
    xr = xr_ref[...]
    conv = cb_ref[...] + xr * cw_ref[halo:halo + 1, :]
    for i in range(halo):
        conv = conv + buf_ref[i] * cw_ref[i:i + 1, :]
    r_parts, i_parts = [], []
    for n in range(n_blocks):
        xn = conv[:, n * bw:(n + 1) * bw].astype(BF16)
        r_parts.append(_dot(xn, wrg_ref[n]))
        i_parts.append(_dot(xn, wig_ref[n]))
    neg_c_softplus = -LRU_C * _softplus(-lam_ref[...])
    a, b = _rglru_gates(conv, jnp.concatenate(r_parts, axis=1), jnp.concatenate(i_parts, axis=1),
                        brg_ref[...], big_ref[...], neg_c_softplus)
    h = a * h0_ref[...] + b
    h_ref[...] = h
    o_ref[...] = h * _gelu_tanh(yr_ref[...])
    for i in range(halo - 1):
        nbuf_ref[i] = buf_ref[i + 1]
    nbuf_ref[halo - 1] = xr


def _rglru_sample(xr, yr, conv_buf_t, h0, w):
    rows, width = xr.shape
    halo = CONV_WIDTH - 1
    return pl.pallas_call(
        _rglru_sample_kernel,
        out_shape=[jax.ShapeDtypeStruct((rows, width), F32),
                   jax.ShapeDtypeStruct((rows, width), F32),
                   jax.ShapeDtypeStruct((halo, rows, width), F32)],
        compiler_params=pltpu.CompilerParams(vmem_limit_bytes=VMEM_LIMIT_BYTES),
        name="rglru_sample",
    )(xr, yr, conv_buf_t, h0, w["conv_w"], w["conv_b"], w["w_rg"], w["w_ig"],
      w["b_rg"], w["b_ig"], w["lam"])


def _out_proj_kernel(attn_ref, rnn_ref, ga_ref, gr_ref, x_ref, woa_ref, wor_ref, wout_ref,
                     gpm_ref, gpf_ref, h_o, hn_o, *, n_parts):
    part_rows = x_ref.shape[0] // n_parts
    for part in range(n_parts):
        r = slice(part * part_rows, (part + 1) * part_rows)
        o = (jax.nn.sigmoid(ga_ref[r, :].astype(F32))
             * _dot(attn_ref[r, :].astype(BF16), woa_ref[...])
             + jax.nn.sigmoid(gr_ref[r, :].astype(F32))
             * _dot(rnn_ref[r, :].astype(BF16), wor_ref[...]))
        mix = _dot(o.astype(BF16), wout_ref[...])
        h = x_ref[r, :] + _rms_norm(mix, gpm_ref[...])
        h_o[r, :] = h
        hn_o[r, :] = _rms_norm(h, gpf_ref[...]).astype(hn_o.dtype)


def _out_proj(attn, rnn, ga, gr, x, w, tm, n_parts=1):
    rows, d_model = x.shape

    def row(width):
        return pl.BlockSpec((tm, width), lambda i: (i, 0))

    def const(shape):
        return pl.BlockSpec(shape, lambda i: (0, 0), pipeline_mode=pl.Buffered(1))

    return pl.pallas_call(
        functools.partial(_out_proj_kernel, n_parts=n_parts),
        grid=(rows // tm,),
        in_specs=[row(attn.shape[1]), row(rnn.shape[1]), row(d_model), row(d_model), row(d_model),
                  const(w["o_attn"].shape), const(w["o_rnn"].shape), const(w["out"].shape),
                  const((1, d_model)), const((1, d_model))],
        out_specs=[row(d_model), row(d_model)],
        out_shape=[jax.ShapeDtypeStruct((rows, d_model), F32),
                   jax.ShapeDtypeStruct((rows, d_model), BF16)],
        compiler_params=_params("parallel"),
        name="out_proj",
    )(attn, rnn, ga, gr, x, w["o_attn"], w["o_rnn"], w["out"], w["g_post_mix"], w["g_pre_ffn"])


def _ffn_kernel(hn_ref, h_ref, wg_ref, wu_ref, wd_ref, g_ref, y_o, acc_s, *, n_ff_steps):
    j = pl.program_id(1)

    @pl.when(j == 0)
    def _():
        acc_s[...] = jnp.zeros_like(acc_s)

    @pl.when(j < n_ff_steps)
    def _():
        hn = hn_ref[...]
        gate = _dot(hn, wg_ref[...])
        up = _dot(hn, wu_ref[...])
        acc_s[...] += _dot((jax.nn.silu(gate) * up).astype(BF16), wd_ref[...])

    @pl.when(j >= n_ff_steps)
    def _():
        part_rows = h_ref.shape[0]
        first = pl.multiple_of((j - n_ff_steps) * part_rows, part_rows)
        y_o[...] = h_ref[...] + _rms_norm(acc_s[pl.ds(first, part_rows), :], g_ref[...])


def _ffn(hn, h, w, tm, tf, n_parts):
    rows, d_model = h.shape
    n_ff_steps = w["gate"].shape[1] // tf
    part_rows = tm // n_parts

    def ff_tile(j):
        return jnp.minimum(j, n_ff_steps - 1)

    def part(i, j):
        return i * n_parts + jnp.maximum(j - n_ff_steps, 0)

    return pl.pallas_call(
        functools.partial(_ffn_kernel, n_ff_steps=n_ff_steps),
        grid=(rows // tm, n_ff_steps + n_parts),
        in_specs=[pl.BlockSpec((tm, d_model), lambda i, j: (i, 0)),
                  pl.BlockSpec((part_rows, d_model), lambda i, j: (part(i, j), 0)),
                  pl.BlockSpec((d_model, tf), lambda i, j: (0, ff_tile(j))),
                  pl.BlockSpec((d_model, tf), lambda i, j: (0, ff_tile(j))),
                  pl.BlockSpec((tf, d_model), lambda i, j: (ff_tile(j), 0)),
                  pl.BlockSpec((1, d_model), lambda i, j: (0, 0))],
        out_specs=pl.BlockSpec((part_rows, d_model), lambda i, j: (part(i, j), 0)),
        out_shape=jax.ShapeDtypeStruct((rows, d_model), F32),
        scratch_shapes=[pltpu.VMEM((tm, d_model), F32)],
        compiler_params=_params("parallel", "arbitrary"),
        name="ffn",
    )(hn, h, w["gate"], w["up"], w["down"], w["g_post_ffn"])


def _layer_weights(l, n_heads, head_dim, g_pre_mix, w_in, b_f, conv_w, conv_b, w_rg, b_rg, w_ig,
                   b_ig, lru_lambda, w_o_attn, w_o_rnn, w_out, g_post_mix, g_pre_ffn, w_gate,
                   w_up, w_down, g_post_ffn):
    d_model = w_in.shape[1]
    n_blocks, bw = w_rg.shape[1], w_rg.shape[2]
    d_rnn = n_blocks * bw
    aw = n_heads * head_dim
    wi_t = jnp.swapaxes(w_in[l], 0, 1)
    f_lo, f_hi = 3 * aw, 3 * aw + n_heads
    cols = {
        "qkv": _cast_rows(wi_t, 0, f_lo, 512),
        "rest": _cast_rows(wi_t, f_hi, wi_t.shape[0] - f_hi, 512),
        "f": jnp.pad(wi_t[f_lo:f_hi].astype(BF16), ((0, LANES - n_heads), (0, 0))),
        "b_f": jnp.pad(b_f[l].reshape(1, n_heads), ((0, 0), (0, LANES - n_heads))),
        "widths": {"q": aw, "k": aw, "v": aw, "xr": d_rnn, "yr": d_rnn,
                   "ga": d_model, "gr": d_model},
        "n_heads": n_heads,
        "head_dim": head_dim,
    }
    rnn = {
        "n_rnn_blocks": n_blocks, "rnn_block": bw,
        "conv_w": conv_w[l], "conv_b": conv_b[l].reshape(1, d_rnn),
        "b_rg": b_rg[l].reshape(1, d_rnn), "b_ig": b_ig[l].reshape(1, d_rnn),
        "lam": lru_lambda[l].reshape(1, d_rnn),
        "conv_w3": conv_w[l].reshape(CONV_WIDTH, n_blocks, bw),
        "conv_b2": conv_b[l].reshape(n_blocks, bw),
        "b_rg2": b_rg[l].reshape(n_blocks, bw), "b_ig2": b_ig[l].reshape(n_blocks, bw),
        "lam2": lru_lambda[l].reshape(n_blocks, bw),
        "w_rg": w_rg[l].astype(BF16), "w_ig": w_ig[l].astype(BF16),
    }
    mix = {"o_attn": w_o_attn[l], "o_rnn": w_o_rnn[l], "out": w_out[l],
           "g_post_mix": g_post_mix[l].reshape(1, d_model),
           "g_pre_ffn": g_pre_ffn[l].reshape(1, d_model)}
    ffn = {"gate": w_gate[l], "up": w_up[l], "down": w_down[l],
           "g_post_ffn": g_post_ffn[l].reshape(1, d_model)}
    return g_pre_mix[l].reshape(1, d_model), cols, rnn, mix, ffn


def kernel(x_prompt, x_sample, cache_k, cache_v, cache_logf, state_h, state_conv, page_table,
           g_pre_mix, w_in, b_f, conv_w, conv_b, w_rg, b_rg, w_ig, b_ig, lru_lambda,
           w_o_attn, w_o_rnn, w_out, g_post_mix, g_pre_ffn, w_gate, w_up, w_down, g_post_ffn):
    batch, seq, d_model = x_prompt.shape
    dec_batch, dec_seq, _ = x_sample.shape
    assert dec_seq == 1, "sample group carries one new token per sequence"
    depth, n_phys, page, n_heads, head_dim = cache_k.shape
    aw = n_heads * head_dim
    d_rnn = state_h.shape[2]
    n_pages = page_table.shape[1]
    assert LANES % n_heads == 0 and (page * n_heads) % LANES == 0

    xp = x_prompt.reshape(batch * seq, d_model)
    xs = x_sample.reshape(dec_batch, d_model)
    outs = {name: [] for name in ("kp", "vp", "lp", "hp", "cp", "ks", "vs", "ls", "hs", "cs")}

    for l in range(depth):
        g_in, w_cols, w_rnn, w_mix, w_ffn = _layer_weights(
            l, n_heads, head_dim, g_pre_mix, w_in, b_f, conv_w, conv_b, w_rg, b_rg, w_ig, b_ig,
            lru_lambda, w_o_attn, w_o_rnn, w_out, g_post_mix, g_pre_ffn, w_gate, w_up, w_down,
            g_post_ffn)

        q, k16, v16, k, v, xr, yr, ga, gr, lf, c = _in_proj(xp, g_in, w_cols, tm=512,
                                                            n_col_steps=4, seq=seq)
        qs, _, _, ks, vs, xrs, yrs, gas, grs, lfs = _in_proj(xs, g_in, w_cols, tm=dec_batch,
                                                             n_col_steps=4)
        ks = ks.reshape(dec_batch, n_heads, head_dim)
        vs = vs.reshape(dec_batch, n_heads, head_dim)

        attn, (woa, wor, wout, wg, wu, wd) = _attn_prompt(
            q, k16, v16, c, batch, seq, n_heads, head_dim, tq=256,
            weights_to_cast=[w_mix["o_attn"], w_mix["o_rnn"], w_mix["out"],
                             w_ffn["gate"], w_ffn["up"], w_ffn["down"]])
        w_mix = dict(w_mix, o_attn=woa, o_rnn=wor, out=wout)
        w_ffn = dict(w_ffn, gate=wg, up=wu, down=wd)
        rnn, h_last = _rglru_prompt(xr, yr, w_rnn, batch, seq, tc=128)
        h, hn = _out_proj(attn, rnn, ga, gr, xp, w_mix, tm=512, n_parts=2)
        xp_next = _ffn(hn, h, w_ffn, tm=512, tf=512, n_parts=1)
        outs["kp"].append(k.reshape(batch, seq, n_heads, head_dim))
        outs["vp"].append(v.reshape(batch, seq, n_heads, head_dim))
        outs["lp"].append(lf.reshape(batch, seq, n_heads))
        outs["hp"].append(h_last)
        outs["cp"].append(xr.reshape(batch, seq, d_rnn)[:, seq - (CONV_WIDTH - 1):])

        attn_s = _attn_sample(
            qs.reshape(dec_batch, n_heads, head_dim), ks, vs, lfs.reshape(dec_batch, n_heads, 1),
            cache_k[l], cache_v[l], jnp.swapaxes(cache_logf[l], 1, 2),
            page_table, pages_per_step=n_pages)
        rnn, h_new, buf_new = _rglru_sample(xrs, yrs, jnp.swapaxes(state_conv[l], 0, 1),
                                            state_h[l], w_rnn)
        h, hn = _out_proj(attn_s.reshape(dec_batch, aw), rnn, gas, grs, xs, w_mix, tm=dec_batch)
        xs_next = _ffn(hn, h, w_ffn, tm=dec_batch, tf=512, n_parts=1)
        outs["ks"].append(ks.reshape(dec_batch, 1, n_heads, head_dim))
        outs["vs"].append(vs.reshape(dec_batch, 1, n_heads, head_dim))
        outs["ls"].append(lfs.reshape(dec_batch, 1, n_heads))
        outs["hs"].append(h_new)
        outs["cs"].append(jnp.swapaxes(buf_new, 0, 1))

        xp, xs = xp_next, xs_next

    st = {name: jnp.stack(vals) for name, vals in outs.items()}
    return (xp.reshape(batch, seq, d_model), xs.reshape(dec_batch, 1, d_model),
            st["kp"], st["vp"], st["lp"], st["hp"], st["cp"],
            st["ks"], st["vs"], st["ls"], st["hs"], st["cs"])
```

```python
import functools
import math

import jax
import jax.numpy as jnp
from jax import lax
from jax.experimental import pallas as pl
from jax.experimental.pallas import tpu as pltpu

F32 = jnp.float32
BF16 = jnp.bfloat16

EPS = 1e-6
LRU_C = 8.0
CONV_WIDTH = 4
NEG_BIG = -1e30

SUBLANES = 8
LANES = 128
VMEM_LIMIT_BYTES = 56 * 1024 * 1024


def _params(*semantics):
    return pltpu.CompilerParams(dimension_semantics=semantics,
                                vmem_limit_bytes=VMEM_LIMIT_BYTES)


def _rms_norm(x, g):
    inv = lax.rsqrt(jnp.mean(x * x, axis=-1, keepdims=True) + EPS)
    return (x * inv) * g


def _log_sigmoid(x):
    return jnp.minimum(x, 0.0) - jnp.log1p(jnp.exp(-jnp.abs(x)))


def _softplus(x):
    return jnp.maximum(x, 0.0) + jnp.log1p(jnp.exp(-jnp.abs(x)))


def _gelu_tanh(x):
    c = math.sqrt(2.0 / math.pi)
    return x * (0.5 * (1.0 + jnp.tanh(c * (x + 0.044715 * (x * x * x)))))


def _dot(a, b):
    return jnp.dot(a, b, preferred_element_type=F32)


def _dot_nt(a, b):
    return lax.dot_general(a, b, (((1,), (1,)), ((), ())), preferred_element_type=F32)


def _cast_kernel(x_ref, o_ref):
    o_ref[...] = x_ref[...].astype(o_ref.dtype)


def _cast_shifted_kernel(a_ref, b_ref, o_ref, *, shift):
    o_ref[...] = jnp.concatenate([a_ref[shift:, :], b_ref[:shift, :]], axis=0).astype(o_ref.dtype)


def _cast_rows(w_t, first_row, n_rows, tile):
    width = w_t.shape[1]
    assert n_rows % tile == 0 and tile % SUBLANES == 0
    shift = first_row % tile
    base = first_row // tile
    out_spec = pl.BlockSpec((tile, width), lambda j: (j, 0))
    out_shape = jax.ShapeDtypeStruct((n_rows, width), BF16)
    if shift == 0:
        return pl.pallas_call(
            _cast_kernel, grid=(n_rows // tile,),
            in_specs=[pl.BlockSpec((tile, width), lambda j: (base + j, 0))],
            out_specs=out_spec, out_shape=out_shape,
            compiler_params=_params("parallel"), name="cast",
        )(w_t)
    assert shift == SUBLANES and first_row + n_rows <= w_t.shape[0]
    per_tile = tile // SUBLANES
    return pl.pallas_call(
        functools.partial(_cast_shifted_kernel, shift=shift), grid=(n_rows // tile,),
        in_specs=[pl.BlockSpec((tile, width), lambda j: (base + j, 0)),
                  pl.BlockSpec((SUBLANES, width), lambda j: ((base + j + 1) * per_tile, 0))],
        out_specs=out_spec, out_shape=out_shape,
        compiler_params=_params("parallel"), name="cast_shifted",
    )(w_t, w_t)


def _prefix_sum(x, axis, start=1):
    idx = lax.broadcasted_iota(jnp.int32, x.shape, axis)
    shift = start
    while shift < x.shape[axis]:
        x = x + jnp.where(idx >= shift, pltpu.roll(x, shift, axis=axis), 0.0)
        shift *= 2
    return x


def _in_proj_kernel(x_ref, g_ref, wq, wk, wv, wxr, wyr, wga, wgr, wf, bf_ref,
                    q_o, k16_o, v16_o, k32_o, v32_o, xr_o, yr_o, ga_o, gr_o, lf_o, *rest,
                    tiles_per_seq, inv_scale):
    i, j = pl.program_id(0), pl.program_id(1)
    xn_s = rest[-1]
    n_heads = lf_o.shape[1]

    @pl.when(j == 0)
    def _():
        xn = _rms_norm(x_ref[...], g_ref[...]).astype(BF16)
        xn_s[...] = xn
        lf = _log_sigmoid(_dot_nt(xn, wf[...]) + bf_ref[...])
        lf_o[...] = lf[:, :n_heads]
        if tiles_per_seq is not None:
            c_o, carry_s = rest[0], rest[1]

            @pl.when(lax.rem(i, tiles_per_seq) == 0)
            def _():
                carry_s[...] = jnp.zeros_like(carry_s)

            c = _prefix_sum(lf, axis=0) + carry_s[...]
            carry_s[...] = c[c.shape[0] - 1:, :]
            hi, mid, lo = _split3_bf16(c * inv_scale)
            lane = lax.broadcasted_iota(jnp.int32, c.shape, 1)
            packed = jnp.where(
                lane < n_heads, hi,
                jnp.where(lane < 2 * n_heads, pltpu.roll(mid, n_heads, axis=1),
                          jnp.where(lane < 3 * n_heads, pltpu.roll(lo, 2 * n_heads, axis=1),
                                    jnp.where(lane == 3 * n_heads, 1.0, 0.0))))
            c_o[...] = packed.astype(c_o.dtype)

    xn = xn_s[...]
    for w, o in ((wq, q_o), (wxr, xr_o), (wyr, yr_o), (wga, ga_o), (wgr, gr_o)):
        o[...] = _dot_nt(xn, w[...]).astype(o.dtype)

    tm = xn.shape[0]
    n_heads = k32_o.shape[0] // tm
    head_dim = k32_o.shape[1]
    heads_per_step = wk.shape[0] // head_dim
    for w, o16, o32 in ((wk, k16_o, k32_o), (wv, v16_o, v32_o)):
        res = _dot_nt(xn, w[...])
        o16[...] = res.astype(o16.dtype)
        for hh in range(heads_per_step):
            head = j * heads_per_step + hh
            o32[pl.ds(head, tm, stride=n_heads), :] = res[:, hh * head_dim:(hh + 1) * head_dim]


def _in_proj(x, g, w, tm, n_col_steps, seq=None):
    rows, d_model = x.shape
    n_heads, head_dim = w["n_heads"], w["head_dim"]
    cols = {}
    for arr, names in (("qkv", ("q", "k", "v")), ("rest", ("xr", "yr", "ga", "gr"))):
        start = 0
        for name in names:
            cols[name] = (arr, start, w["widths"][name])
            start += w["widths"][name]
    order = ("q", "k", "v", "xr", "yr", "ga", "gr")
    tiles = {name: cols[name][2] // n_col_steps for name in order}
    grid = (rows // tm, n_col_steps)

    def w_spec(name):
        first_block = cols[name][1] // tiles[name]
        return pl.BlockSpec((tiles[name], d_model), lambda i, j: (first_block + j, 0))

    def o_spec(name):
        return pl.BlockSpec((tm, tiles[name]), lambda i, j: (i, j))

    def o_shape(name, dtype):
        return jax.ShapeDtypeStruct((rows, cols[name][2]), dtype)

    def per_head():
        return pl.BlockSpec((tm, n_heads), lambda i, j: (i, 0))

    flat_spec = pl.BlockSpec((tm * n_heads, head_dim), lambda i, j: (i, 0))
    flat_shape = jax.ShapeDtypeStruct((rows * n_heads, head_dim), F32)
    out_specs = [o_spec("q"), o_spec("k"), o_spec("v"), flat_spec, flat_spec,
                 o_spec("xr"), o_spec("yr"), o_spec("ga"), o_spec("gr"), per_head()]
    out_shape = [o_shape("q", BF16), o_shape("k", BF16), o_shape("v", BF16),
                 flat_shape, flat_shape,
                 o_shape("xr", F32), o_shape("yr", F32), o_shape("ga", BF16), o_shape("gr", BF16),
                 jax.ShapeDtypeStruct((rows, n_heads), F32)]
    scratch = [pltpu.VMEM((tm, d_model), BF16)]
    if seq is not None:
        out_specs.append(pl.BlockSpec((tm, LANES), lambda i, j: (i, 0)))
        out_shape.append(jax.ShapeDtypeStruct((rows, LANES), BF16))
        scratch.insert(0, pltpu.VMEM((1, LANES), F32))
    return pl.pallas_call(
        functools.partial(_in_proj_kernel,
                          tiles_per_seq=None if seq is None else seq // tm,
                          inv_scale=head_dim ** 0.5),
        grid=grid,
        in_specs=[pl.BlockSpec((tm, d_model), lambda i, j: (i, 0)),
                  pl.BlockSpec((1, d_model), lambda i, j: (0, 0))]
                 + [w_spec(name) for name in order]
                 + [pl.BlockSpec((LANES, d_model), lambda i, j: (0, 0)),
                    pl.BlockSpec((1, LANES), lambda i, j: (0, 0))],
        out_specs=out_specs,
        out_shape=out_shape,
        scratch_shapes=scratch,
        compiler_params=_params("arbitrary", "arbitrary"),
        name="in_proj",
    )(x, g, *[w[cols[name][0]] for name in order], w["f"], w["b_f"])


def _split3_bf16(x):
    hi = x.astype(BF16)
    r = x - hi.astype(F32)
    mid = r.astype(BF16)
    lo = (r - mid.astype(F32)).astype(BF16)
    return hi.astype(F32), mid.astype(F32), lo.astype(F32)


def _attn_prompt_kernel(q_ref, k_ref, v_ref, c_ref, *rest, tq, scale, n_heads, n_casts):
    o_ref = rest[n_casts]
    for w_ref, w16_ref in zip(rest[:n_casts], rest[n_casts + 1:2 * n_casts + 1]):
        w16_ref[...] = w_ref[...].astype(w16_ref.dtype)
    qa_s, ka_s, m_s, l_s, acc_s = rest[2 * n_casts + 1:]
    seq, head_dim = q_ref.shape
    h = pl.program_id(1)
    src = lax.broadcasted_iota(jnp.int32, (LANES, LANES), 0)
    dst = lax.broadcasted_iota(jnp.int32, (LANES, LANES), 1)
    one_src = src == 3 * n_heads
    part_of_dst = jnp.where(dst < 3, dst, dst - 3) * n_heads + h
    sel_q = jnp.where((dst < 3) & (src == part_of_dst), 1.0,
                      jnp.where((dst >= 3) & (dst < 6) & one_src, 1.0, 0.0))
    sel_k = jnp.where((dst >= 3) & (dst < 6) & (src == part_of_dst), -1.0,
                      jnp.where((dst < 3) & one_src, 1.0, 0.0))
    c_rows = c_ref[...]
    qa_s[:, :head_dim] = q_ref[...]
    qa_s[:, head_dim:] = _dot(c_rows, sel_q.astype(BF16)).astype(BF16)
    ka_s[:, :head_dim] = k_ref[...]
    ka_s[:, head_dim:] = _dot(c_rows, sel_k.astype(BF16)).astype(BF16)

    sigma = scale * math.log2(math.e)
    row = lax.broadcasted_iota(jnp.int32, (tq, tq), 0)
    col = lax.broadcasted_iota(jnp.int32, (tq, tq), 1)
    visible = col <= row
    groups = tq // LANES
    n_tiles = seq // tq
    for j in range(n_tiles):
        s_all = _dot_nt(qa_s[j * tq:, :], ka_s[j * tq:(j + 1) * tq, :]) * sigma
        v_j = v_ref[j * tq:(j + 1) * tq, :]
        for i in range(j, n_tiles):
            rows = slice(i * tq, (i + 1) * tq)
            s = s_all[(i - j) * tq:(i - j + 1) * tq, :]
            if i == j:
                s = jnp.where(visible, s, NEG_BIG)
            blk_max = jnp.broadcast_to(jnp.max(s, axis=-1, keepdims=True), (tq, LANES))
            m_new = blk_max if j == 0 else jnp.maximum(m_s[rows, :], blk_max)
            parts = [jnp.exp2(s[:, g * LANES:(g + 1) * LANES] - m_new) for g in range(groups)]
            l_new = parts[0]
            for p in parts[1:]:
                l_new = l_new + p
            pv = _dot(jnp.concatenate([p.astype(BF16) for p in parts], axis=1), v_j)
            if j > 0:
                alpha = jnp.exp2(m_s[rows, :] - m_new)
                l_new = alpha * l_s[rows, :] + l_new
                pv = alpha * acc_s[rows, :] + pv
            if i == j:
                l = jnp.sum(l_new, axis=-1, keepdims=True)
                o_ref[rows, :] = (pv / l).astype(o_ref.dtype)
            else:
                m_s[rows, :] = m_new
                l_s[rows, :] = l_new
                acc_s[rows, :] = pv


def _attn_prompt(q, k, v, c, batch, seq, n_heads, head_dim, tq, weights_to_cast):
    assert head_dim == LANES, "per-row softmax state is kept one vreg lane group wide"
    head_spec = pl.BlockSpec((seq, head_dim), lambda b, h: (b, h))
    n_steps = batch * n_heads
    cast_specs = []
    for w in weights_to_cast:
        assert w.shape[0] % (n_steps * 2 * SUBLANES) == 0
        cast_specs.append(pl.BlockSpec((w.shape[0] // n_steps, w.shape[1]),
                                       lambda b, h: (b * n_heads + h, 0)))
    attn, *casts = pl.pallas_call(
        functools.partial(_attn_prompt_kernel, tq=tq, scale=head_dim ** -0.5, n_heads=n_heads,
                          n_casts=len(weights_to_cast)),
        grid=(batch, n_heads),
        in_specs=[head_spec, head_spec, head_spec,
                  pl.BlockSpec((seq, LANES), lambda b, h: (b, 0))] + cast_specs,
        out_specs=[head_spec] + cast_specs,
        out_shape=[jax.ShapeDtypeStruct(q.shape, BF16)]
                  + [jax.ShapeDtypeStruct(w.shape, BF16) for w in weights_to_cast],
        scratch_shapes=[pltpu.VMEM((seq, head_dim + LANES), BF16),
                        pltpu.VMEM((seq, head_dim + LANES), BF16),
                        pltpu.VMEM((seq, LANES), F32),
                        pltpu.VMEM((seq, LANES), F32),
                        pltpu.VMEM((seq, head_dim), F32)],
        compiler_params=_params("parallel", "parallel"),
        name="attn_prompt",
    )(q, k, v, c, *weights_to_cast)
    return attn, casts


def _paged_forget_bias(lf_pages, lf_new, n_heads):
    rows, page = lf_pages.shape
    n_pages = rows // n_heads
    within = _prefix_sum(lf_pages, axis=1)
    totals = jnp.broadcast_to(within[:, page - 1:], (rows, page))
    upto = _prefix_sum(totals, axis=0, start=n_heads)
    c_past = within + (upto - totals)
    c_new = upto[rows - n_heads:, 0:1] + lf_new
    bias = jnp.concatenate([c_new] * n_pages, axis=0) - c_past
    src = lax.broadcasted_iota(jnp.int32, (page, page * n_heads), 0)
    dst = lax.broadcasted_iota(jnp.int32, (page, page * n_heads), 1)
    expand = jnp.where(dst // n_heads == src, 1.0, 0.0).astype(BF16)
    parts = jnp.concatenate(_split3_bf16(bias), axis=0).astype(BF16)
    wide = _dot(parts, expand)
    return wide[:rows] + wide[rows:2 * rows] + wide[2 * rows:]


def _paged_attn_step(step, q_ref, kn_ref, vn_ref, lfn_ref, k_refs, v_refs, load_lf_pages, o_ref,
                     s_s, bias_s, m_s, l_s, acc_s, *, steps_per_seq, scale):
    pps = len(k_refs)
    page, n_heads, head_dim = k_refs[0].shape[1:]
    pw = page * n_heads
    part = lax.rem(step, steps_per_seq)

    head_of_row = lax.broadcasted_iota(jnp.int32, (n_heads, pw), 0)
    lane = lax.broadcasted_iota(jnp.int32, (n_heads, pw), 1)
    on_diag = head_of_row == lane % n_heads
    q = q_ref[0]

    @pl.when(part == 0)
    def _():
        bias_s[...] = _paged_forget_bias(
            jnp.concatenate(load_lf_pages(), axis=0), lfn_ref[0], n_heads)
        kn = kn_ref[0].astype(BF16).astype(F32)
        m_s[...] = jnp.sum(q.astype(F32) * kn, axis=-1, keepdims=True) * scale
        l_s[...] = jnp.ones_like(l_s)
        acc_s[...] = vn_ref[0].astype(BF16).astype(F32)

    for p in range(pps):
        kf = k_refs[p][0].reshape(pw, head_dim).astype(BF16)
        first = pl.multiple_of((part * pps + p) * n_heads, n_heads)
        s = _dot_nt(q, kf) * scale + bias_s[pl.ds(first, n_heads), :]
        s_s[:, p * pw:(p + 1) * pw] = jnp.where(on_diag, s, NEG_BIG)

    s_all = s_s[...]
    m_old = m_s[...]
    m_new = jnp.maximum(m_old, jnp.max(s_all, axis=-1, keepdims=True))
    p_all = jnp.exp(s_all - m_new)
    alpha = jnp.exp(m_old - m_new)
    l_s[...] = alpha * l_s[...] + jnp.sum(p_all, axis=-1, keepdims=True)
    m_s[...] = m_new
    p16 = p_all.astype(BF16)
    acc = alpha * acc_s[...]
    for p in range(pps):
        vf = v_refs[p][0].reshape(pw, head_dim).astype(BF16)
        acc = acc + _dot(p16[:, p * pw:(p + 1) * pw], vf)
    acc_s[...] = acc

    @pl.when(part == steps_per_seq - 1)
    def _():
        o_ref[0] = (acc_s[...] / l_s[...]).astype(o_ref.dtype)


def _attn_sample_kernel(pt_ref, q_ref, kn_ref, vn_ref, lfn_ref, lf_all_ref, *refs,
                        pages_per_step, n_pages):
    k_refs = refs[:pages_per_step]
    v_refs = refs[pages_per_step:2 * pages_per_step]
    o_ref = refs[2 * pages_per_step]
    scratch = refs[2 * pages_per_step + 1:]
    b = pl.program_id(0)

    def load_lf_pages():
        return [lf_all_ref[pt_ref[b, p]] for p in range(n_pages)]

    _paged_attn_step(b * pl.num_programs(1) + pl.program_id(1),
                     q_ref, kn_ref, vn_ref, lfn_ref, k_refs, v_refs, load_lf_pages, o_ref,
                     *scratch,
                     steps_per_seq=n_pages // pages_per_step, scale=q_ref.shape[2] ** -0.5)


def _attn_sample(q, k_new, v_new, lf_new, cache_k, cache_v, cache_lf, page_table, pages_per_step):
    batch, n_pages = page_table.shape
    _, page, n_heads, head_dim = cache_k.shape
    pw = page * n_heads
    steps_per_seq = n_pages // pages_per_step
    assert page == LANES, "one page of log-forget values per head is one lane row"

    def row_spec(shape):
        return pl.BlockSpec((1,) + shape, lambda b, s, pt: (b, 0, 0))

    def page_spec(p):
        return pl.BlockSpec((1, page, n_heads, head_dim),
                            lambda b, s, pt: (pt[b, s * pages_per_step + p], 0, 0, 0))

    lf_spec = pl.BlockSpec(cache_lf.shape, lambda b, s, pt: (0, 0, 0),
                           pipeline_mode=pl.Buffered(1))

    return pl.pallas_call(
        functools.partial(_attn_sample_kernel, pages_per_step=pages_per_step, n_pages=n_pages),
        grid_spec=pltpu.PrefetchScalarGridSpec(
            num_scalar_prefetch=1,
            grid=(batch, steps_per_seq),
            in_specs=[row_spec((n_heads, head_dim))] * 3 + [row_spec((n_heads, 1)), lf_spec]
                     + [page_spec(p) for p in range(pages_per_step)] * 2,
            out_specs=row_spec((n_heads, head_dim)),
            scratch_shapes=[pltpu.VMEM((n_heads, pages_per_step * pw), F32),
                            pltpu.VMEM((n_pages * n_heads, pw), F32),
                            pltpu.VMEM((n_heads, 1), F32),
                            pltpu.VMEM((n_heads, 1), F32),
                            pltpu.VMEM((n_heads, head_dim), F32)],
        ),
        out_shape=jax.ShapeDtypeStruct((batch, n_heads, head_dim), BF16),
        compiler_params=_params("arbitrary", "arbitrary"),
        name="attn_sample",
    )(page_table, q, k_new, v_new, lf_new, cache_lf,
      *([cache_k] * pages_per_step), *([cache_v] * pages_per_step))


def _rglru_gates(conv, r_pre, i_pre, b_rg, b_ig, neg_c_softplus):
    r = jax.nn.sigmoid(r_pre + b_rg)
    i_g = jax.nn.sigmoid(i_pre + b_ig)
    log_a = r * neg_c_softplus
    a = jnp.exp(log_a)
    b = jnp.sqrt(-jnp.tanh(log_a) * (1.0 + a * a)) * (i_g * conv)
    return a, b


def _rglru_prompt_kernel(xr_ref, yr_ref, cw_ref, cb_ref, wrg_ref, wig_ref, brg_ref, big_ref,
                         lam_ref, o_ref, hl_ref, xc_s, conv_s, rp_s, ip_s, a_s, b_s, h_s):
    c = pl.program_id(0)
    nb, tc, n_blocks, bw = xr_ref.shape
    halo = CONV_WIDTH - 1
    rows = nb * tc

    @pl.when(c == 0)
    def _():
        xc_s[:, 0:halo] = jnp.zeros((nb, halo, n_blocks, bw), F32)
        h_s[...] = jnp.zeros_like(h_s)

    @pl.when(c > 0)
    def _():
        xc_s[:, 0:halo] = xc_s[:, tc:tc + halo]

    xc_s[:, halo:halo + tc] = xr_ref[...]

    conv = cb_ref[...] + xc_s[:, 0:tc] * cw_ref[0]
    for i in range(1, CONV_WIDTH):
        conv = conv + xc_s[:, i:i + tc] * cw_ref[i]

    conv_s[...] = conv.reshape(rows * n_blocks, bw)
    for n in range(n_blocks):
        xn = conv_s[pl.ds(n, rows, stride=n_blocks), :].astype(BF16)
        rp_s[pl.ds(n, rows, stride=n_blocks), :] = _dot(xn, wrg_ref[n])
        ip_s[pl.ds(n, rows, stride=n_blocks), :] = _dot(xn, wig_ref[n])

    neg_c_softplus = -LRU_C * _softplus(-lam_ref[...])
    a, b = _rglru_gates(conv,
                        rp_s[...].reshape(nb, tc, n_blocks, bw),
                        ip_s[...].reshape(nb, tc, n_blocks, bw),
                        brg_ref[...], big_ref[...], neg_c_softplus)
    a_s[...] = a
    b_s[...] = b

    def step(t, hs):
        new = tuple(a_s[j, t] * hs[j] + b_s[j, t] for j in range(nb))
        for j in range(nb):
            b_s[j, t] = new[j]
        return new

    hs = lax.fori_loop(0, tc, step, tuple(h_s[j] for j in range(nb)), unroll=8)
    for j in range(nb):
        h_s[j] = hs[j]

    conv_s[...] = (b_s[...] * _gelu_tanh(yr_ref[...])).reshape(rows * n_blocks, bw)
    for n in range(n_blocks):
        o_ref[:, :, n * bw:(n + 1) * bw] = (
            conv_s[pl.ds(n, rows, stride=n_blocks), :].reshape(nb, tc, bw).astype(o_ref.dtype))
    hl_ref[...] = h_s[...]


def _rglru_prompt(xr, yr, w, batch, seq, tc):
    n_blocks, bw = w["n_rnn_blocks"], w["rnn_block"]
    xr4 = xr.reshape(batch, seq, n_blocks, bw)
    yr4 = yr.reshape(batch, seq, n_blocks, bw)
    rows = batch * tc
    halo = CONV_WIDTH - 1

    def const(shape):
        return pl.BlockSpec(shape, lambda c: (0,) * len(shape))

    seq_spec = pl.BlockSpec((batch, tc, n_blocks, bw), lambda c: (0, c, 0, 0))
    out, h_last = pl.pallas_call(
        _rglru_prompt_kernel,
        grid=(seq // tc,),
        in_specs=[seq_spec, seq_spec,
                  const((CONV_WIDTH, n_blocks, bw)), const((n_blocks, bw)),
                  const((n_blocks, bw, bw)), const((n_blocks, bw, bw)),
                  const((n_blocks, bw)), const((n_blocks, bw)), const((n_blocks, bw))],
        out_specs=[pl.BlockSpec((batch, tc, n_blocks * bw), lambda c: (0, c, 0)),
                   const((batch, n_blocks, bw))],
        out_shape=[jax.ShapeDtypeStruct((batch, seq, n_blocks * bw), BF16),
                   jax.ShapeDtypeStruct((batch, n_blocks, bw), F32)],
        scratch_shapes=[pltpu.VMEM((batch, tc + halo, n_blocks, bw), F32),
                        pltpu.VMEM((rows * n_blocks, bw), F32),
                        pltpu.VMEM((rows * n_blocks, bw), F32),
                        pltpu.VMEM((rows * n_blocks, bw), F32),
                        pltpu.VMEM((batch, tc, n_blocks, bw), F32),
                        pltpu.VMEM((batch, tc, n_blocks, bw), F32),
                        pltpu.VMEM((batch, n_blocks, bw), F32)],
        compiler_params=_params("arbitrary"),
        name="rglru_prompt",
    )(xr4, yr4, w["conv_w3"], w["conv_b2"], w["w_rg"], w["w_ig"], w["b_rg2"], w["b_ig2"],
      w["lam2"])
    return out.reshape(batch * seq, n_blocks * bw), h_last.reshape(batch, n_blocks * bw)


def _rglru_sample_kernel(xr_ref, yr_ref, buf_ref, h0_ref, cw_ref, cb_ref, wrg_ref, wig_ref,
                         brg_ref, big_ref, lam_ref, o_ref, h_ref, nbuf_ref):
    halo = CONV_WIDTH - 1
    n_blocks, bw = wrg_ref.shape[0], wrg_ref.shape[1]
    xr = xr_ref[...]
    conv = cb_ref[...] + xr * cw_ref[halo:halo + 1, :]
    for i in range(halo):
        conv = conv + buf_ref[i] * cw_ref[i:i + 1, :]
    r_parts, i_parts = [], []
    for n in range(n_blocks):
        xn = conv[:, n * bw:(n + 1) * bw].astype(BF16)
        r_parts.append(_dot(xn, wrg_ref[n]))
        i_parts.append(_dot(xn, wig_ref[n]))
    neg_c_softplus = -LRU_C * _softplus(-lam_ref[...])
    a, b = _rglru_gates(conv, jnp.concatenate(r_parts, axis=1), jnp.concatenate(i_parts, axis=1),
                        brg_ref[...], big_ref[...], neg_c_softplus)
    h = a * h0_ref[...] + b
    h_ref[...] = h
    o_ref[...] = h * _gelu_tanh(yr_ref[...])
    for i in range(halo - 1):
        nbuf_ref[i] = buf_ref[i + 1]
    nbuf_ref[halo - 1] = xr


def _rglru_sample(xr, yr, conv_buf_t, h0, w):
    rows, width = xr.shape
    halo = CONV_WIDTH - 1
    return pl.pallas_call(
        _rglru_sample_kernel,
        out_shape=[jax.ShapeDtypeStruct((rows, width), F32),
                   jax.ShapeDtypeStruct((rows, width), F32),
                   jax.ShapeDtypeStruct((halo, rows, width), F32)],
        compiler_params=pltpu.CompilerParams(vmem_limit_bytes=VMEM_LIMIT_BYTES),
        name="rglru_sample",
    )(xr, yr, conv_buf_t, h0, w["conv_w"], w["conv_b"], w["w_rg"], w["w_ig"],
      w["b_rg"], w["b_ig"], w["lam"])


def _out_proj_kernel(attn_ref, rnn_ref, ga_ref, gr_ref, x_ref, woa_ref, wor_ref, wout_ref,
                     gpm_ref, gpf_ref, h_o, hn_o, *, n_parts):
    part_rows = x_ref.shape[0] // n_parts
    for part in range(n_parts):
        r = slice(part * part_rows, (part + 1) * part_rows)
        o = (jax.nn.sigmoid(ga_ref[r, :].astype(F32))
             * _dot(attn_ref[r, :].astype(BF16), woa_ref[...])
             + jax.nn.sigmoid(gr_ref[r, :].astype(F32))
             * _dot(rnn_ref[r, :].astype(BF16), wor_ref[...]))
        mix = _dot(o.astype(BF16), wout_ref[...])
        h = x_ref[r, :] + _rms_norm(mix, gpm_ref[...])
        h_o[r, :] = h
        hn_o[r, :] = _rms_norm(h, gpf_ref[...]).astype(hn_o.dtype)


def _out_proj(attn, rnn, ga, gr, x, w, tm, n_parts=1):
    rows, d_model = x.shape

    def row(width):
        return pl.BlockSpec((tm, width), lambda i: (i, 0))

    def const(shape):
        return pl.BlockSpec(shape, lambda i: (0, 0), pipeline_mode=pl.Buffered(1))

    return pl.pallas_call(
        functools.partial(_out_proj_kernel, n_parts=n_parts),
        grid=(rows // tm,),
        in_specs=[row(attn.shape[1]), row(rnn.shape[1]), row(d_model), row(d_model), row(d_model),
                  const(w["o_attn"].shape), const(w["o_rnn"].shape), const(w["out"].shape),
                  const((1, d_model)), const((1, d_model))],
        out_specs=[row(d_model), row(d_model)],
        out_shape=[jax.ShapeDtypeStruct((rows, d_model), F32),
                   jax.ShapeDtypeStruct((rows, d_model), BF16)],
        compiler_params=_params("parallel"),
        name="out_proj",
    )(attn, rnn, ga, gr, x, w["o_attn"], w["o_rnn"], w["out"], w["g_post_mix"], w["g_pre_ffn"])


def _ffn_kernel(hn_ref, h_ref, wg_ref, wu_ref, wd_ref, g_ref, y_o, acc_s):
    j = pl.program_id(1)

    @pl.when(j == 0)
    def _():
        acc_s[...] = jnp.zeros_like(acc_s)

    hn = hn_ref[...]
    gate = _dot(hn, wg_ref[...])
    up = _dot(hn, wu_ref[...])
    acc_s[...] += _dot((jax.nn.silu(gate) * up).astype(BF16), wd_ref[...])

    @pl.when(j == pl.num_programs(1) - 1)
    def _():
        y_o[...] = h_ref[...] + _rms_norm(acc_s[...], g_ref[...])


def _ffn(hn, h, w, tm, tf):
    rows, d_model = h.shape
    return pl.pallas_call(
        _ffn_kernel,
        grid=(rows // tm, w["gate"].shape[1] // tf),
        in_specs=[pl.BlockSpec((tm, d_model), lambda i, j: (i, 0)),
                  pl.BlockSpec((tm, d_model), lambda i, j: (i, 0)),
                  pl.BlockSpec((d_model, tf), lambda i, j: (0, j)),
                  pl.BlockSpec((d_model, tf), lambda i, j: (0, j)),
                  pl.BlockSpec((tf, d_model), lambda i, j: (j, 0)),
                  pl.BlockSpec((1, d_model), lambda i, j: (0, 0))],
        out_specs=pl.BlockSpec((tm, d_model), lambda i, j: (i, 0)),
        out_shape=jax.ShapeDtypeStruct((rows, d_model), F32),
        scratch_shapes=[pltpu.VMEM((tm, d_model), F32)],
        compiler_params=_params("parallel", "arbitrary"),
        name="ffn",
    )(hn, h, w["gate"], w["up"], w["down"], w["g_post_ffn"])


def _layer_weights(l, n_heads, head_dim, g_pre_mix, w_in, b_f, conv_w, conv_b, w_rg, b_rg, w_ig,
                   b_ig, lru_lambda, w_o_attn, w_o_rnn, w_out, g_post_mix, g_pre_ffn, w_gate,
                   w_up, w_down, g_post_ffn):
    d_model = w_in.shape[1]
    n_blocks, bw = w_rg.shape[1], w_rg.shape[2]
    d_rnn = n_blocks * bw
    aw = n_heads * head_dim
    wi_t = jnp.swapaxes(w_in[l], 0, 1)
    f_lo, f_hi = 3 * aw, 3 * aw + n_heads
    cols = {
        "qkv": _cast_rows(wi_t, 0, f_lo, 512),
        "rest": _cast_rows(wi_t, f_hi, wi_t.shape[0] - f_hi, 512),
        "f": jnp.pad(wi_t[f_lo:f_hi].astype(BF16), ((0, LANES - n_heads), (0, 0))),
        "b_f": jnp.pad(b_f[l].reshape(1, n_heads), ((0, 0), (0, LANES - n_heads))),
        "widths": {"q": aw, "k": aw, "v": aw, "xr": d_rnn, "yr": d_rnn,
                   "ga": d_model, "gr": d_model},
        "n_heads": n_heads,
        "head_dim": head_dim,
    }
    rnn = {
        "n_rnn_blocks": n_blocks, "rnn_block": bw,
        "conv_w": conv_w[l], "conv_b": conv_b[l].reshape(1, d_rnn),
        "b_rg": b_rg[l].reshape(1, d_rnn), "b_ig": b_ig[l].reshape(1, d_rnn),
        "lam": lru_lambda[l].reshape(1, d_rnn),
        "conv_w3": conv_w[l].reshape(CONV_WIDTH, n_blocks, bw),
        "conv_b2": conv_b[l].reshape(n_blocks, bw),
        "b_rg2": b_rg[l].reshape(n_blocks, bw), "b_ig2": b_ig[l].reshape(n_blocks, bw),
        "lam2": lru_lambda[l].reshape(n_blocks, bw),
        "w_rg": w_rg[l].astype(BF16), "w_ig": w_ig[l].astype(BF16),
    }
    mix = {"o_attn": w_o_attn[l], "o_rnn": w_o_rnn[l], "out": w_out[l],
           "g_post_mix": g_post_mix[l].reshape(1, d_model),
           "g_pre_ffn": g_pre_ffn[l].reshape(1, d_model)}
    ffn = {"gate": w_gate[l], "up": w_up[l], "down": w_down[l],
           "g_post_ffn": g_post_ffn[l].reshape(1, d_model)}
    return g_pre_mix[l].reshape(1, d_model), cols, rnn, mix, ffn


def kernel(x_prompt, x_sample, cache_k, cache_v, cache_logf, state_h, state_conv, page_table,
           g_pre_mix, w_in, b_f, conv_w, conv_b, w_rg, b_rg, w_ig, b_ig, lru_lambda,
           w_o_attn, w_o_rnn, w_out, g_post_mix, g_pre_ffn, w_gate, w_up, w_down, g_post_ffn):
    batch, seq, d_model = x_prompt.shape
    dec_batch, dec_seq, _ = x_sample.shape
    assert dec_seq == 1, "sample group carries one new token per sequence"
    depth, n_phys, page, n_heads, head_dim = cache_k.shape
    aw = n_heads * head_dim
    d_rnn = state_h.shape[2]
    n_pages = page_table.shape[1]
    assert LANES % n_heads == 0 and (page * n_heads) % LANES == 0

    xp = x_prompt.reshape(batch * seq, d_model)
    xs = x_sample.reshape(dec_batch, d_model)
    outs = {name: [] for name in ("kp", "vp", "lp", "hp", "cp", "ks", "vs", "ls", "hs", "cs")}

    for l in range(depth):
        g_in, w_cols, w_rnn, w_mix, w_ffn = _layer_weights(
            l, n_heads, head_dim, g_pre_mix, w_in, b_f, conv_w, conv_b, w_rg, b_rg, w_ig, b_ig,
            lru_lambda, w_o_attn, w_o_rnn, w_out, g_post_mix, g_pre_ffn, w_gate, w_up, w_down,
            g_post_ffn)

        q, k16, v16, k, v, xr, yr, ga, gr, lf, c = _in_proj(xp, g_in, w_cols, tm=512,
                                                            n_col_steps=4, seq=seq)
        qs, _, _, ks, vs, xrs, yrs, gas, grs, lfs = _in_proj(xs, g_in, w_cols, tm=dec_batch,
                                                             n_col_steps=4)
        ks = ks.reshape(dec_batch, n_heads, head_dim)
        vs = vs.reshape(dec_batch, n_heads, head_dim)

        attn, (woa, wor, wout, wg, wu, wd) = _attn_prompt(
            q, k16, v16, c, batch, seq, n_heads, head_dim, tq=256,
            weights_to_cast=[w_mix["o_attn"], w_mix["o_rnn"], w_mix["out"],
                             w_ffn["gate"], w_ffn["up"], w_ffn["down"]])
        w_mix = dict(w_mix, o_attn=woa, o_rnn=wor, out=wout)
        w_ffn = dict(w_ffn, gate=wg, up=wu, down=wd)
        rnn, h_last = _rglru_prompt(xr, yr, w_rnn, batch, seq, tc=128)
        h, hn = _out_proj(attn, rnn, ga, gr, xp, w_mix, tm=512, n_parts=2)
        xp_next = _ffn(hn, h, w_ffn, tm=512, tf=512)
        outs["kp"].append(k.reshape(batch, seq, n_heads, head_dim))
        outs["vp"].append(v.reshape(batch, seq, n_heads, head_dim))
        outs["lp"].append(lf.reshape(batch, seq, n_heads))
        outs["hp"].append(h_last)
        outs["cp"].append(xr.reshape(batch, seq, d_rnn)[:, seq - (CONV_WIDTH - 1):])

        attn_s = _attn_sample(
            qs.reshape(dec_batch, n_heads, head_dim), ks, vs, lfs.reshape(dec_batch, n_heads, 1),
            cache_k[l], cache_v[l], jnp.swapaxes(cache_logf[l], 1, 2),
            page_table, pages_per_step=n_pages)
        rnn, h_new, buf_new = _rglru_sample(xrs, yrs, jnp.swapaxes(state_conv[l], 0, 1),
                                            state_h[l], w_rnn)
        h, hn = _out_proj(attn_s.reshape(dec_batch, aw), rnn, gas, grs, xs, w_mix, tm=dec_batch)
        xs_next = _ffn(hn, h, w_ffn, tm=dec_batch, tf=w_gate.shape[2] // 4)
        outs["ks"].append(ks.reshape(dec_batch, 1, n_heads, head_dim))
        outs["vs"].append(vs.reshape(dec_batch, 1, n_heads, head_dim))
        outs["ls"].append(lfs.reshape(dec_batch, 1, n_heads))
        outs["hs"].append(h_new)
        outs["cs"].append(jnp.swapaxes(buf_new, 0, 1))

        xp, xs = xp_next, xs_next

    st = {name: jnp.stack(vals) for name, vals in outs.items()}
    return (xp.reshape(batch, seq, d_model), xs.reshape(dec_batch, 1, d_model),
            st["kp"], st["vp"], st["lp"], st["hp"], st["cp"],
            st["ks"], st["vs"], st["ls"], st["hs"], st["cs"])
```

```python
import functools
import math

import jax
import jax.numpy as jnp
from jax import lax
from jax.experimental import pallas as pl
from jax.experimental.pallas import tpu as pltpu

F32 = jnp.float32
BF16 = jnp.bfloat16

EPS = 1e-6
LRU_C = 8.0
CONV_WIDTH = 4
NEG_BIG = -1e30

SUBLANES = 8
LANES = 128
VMEM_LIMIT_BYTES = 56 * 1024 * 1024


def _params(*semantics):
    return pltpu.CompilerParams(dimension_semantics=semantics,
                                vmem_limit_bytes=VMEM_LIMIT_BYTES)


def _rms_norm(x, g):
    inv = lax.rsqrt(jnp.mean(x * x, axis=-1, keepdims=True) + EPS)
    return (x * inv) * g


def _log_sigmoid(x):
    return jnp.minimum(x, 0.0) - jnp.log1p(jnp.exp(-jnp.abs(x)))


def _softplus(x):
    return jnp.maximum(x, 0.0) + jnp.log1p(jnp.exp(-jnp.abs(x)))


def _gelu_tanh(x):
    c = math.sqrt(2.0 / math.pi)
    return x * (0.5 * (1.0 + jnp.tanh(c * (x + 0.044715 * (x * x * x)))))


def _dot(a, b):
    return jnp.dot(a, b, preferred_element_type=F32)


def _dot_nt(a, b):
    return lax.dot_general(a, b, (((1,), (1,)), ((), ())), preferred_element_type=F32)


def _cast_kernel(x_ref, o_ref):
    o_ref[...] = x_ref[...].astype(o_ref.dtype)


def _cast_shifted_kernel(a_ref, b_ref, o_ref, *, shift):
    o_ref[...] = jnp.concatenate([a_ref[shift:, :], b_ref[:shift, :]], axis=0).astype(o_ref.dtype)


def _cast_rows(w_t, first_row, n_rows, tile):
    width = w_t.shape[1]
    assert n_rows % tile == 0 and tile % SUBLANES == 0
    shift = first_row % tile
    base = first_row // tile
    out_spec = pl.BlockSpec((tile, width), lambda j: (j, 0))
    out_shape = jax.ShapeDtypeStruct((n_rows, width), BF16)
    if shift == 0:
        return pl.pallas_call(
            _cast_kernel, grid=(n_rows // tile,),
            in_specs=[pl.BlockSpec((tile, width), lambda j: (base + j, 0))],
            out_specs=out_spec, out_shape=out_shape,
            compiler_params=_params("parallel"), name="cast",
        )(w_t)
    assert shift == SUBLANES and first_row + n_rows <= w_t.shape[0]
    per_tile = tile // SUBLANES
    return pl.pallas_call(
        functools.partial(_cast_shifted_kernel, shift=shift), grid=(n_rows // tile,),
        in_specs=[pl.BlockSpec((tile, width), lambda j: (base + j, 0)),
                  pl.BlockSpec((SUBLANES, width), lambda j: ((base + j + 1) * per_tile, 0))],
        out_specs=out_spec, out_shape=out_shape,
        compiler_params=_params("parallel"), name="cast_shifted",
    )(w_t, w_t)


def _prefix_sum(x, axis, start=1):
    idx = lax.broadcasted_iota(jnp.int32, x.shape, axis)
    shift = start
    while shift < x.shape[axis]:
        x = x + jnp.where(idx >= shift, pltpu.roll(x, shift, axis=axis), 0.0)
        shift *= 2
    return x


def _in_proj_kernel(x_ref, g_ref, wq, wk, wv, wxr, wyr, wga, wgr, wf, bf_ref,
                    q_o, k16_o, v16_o, k32_o, v32_o, xr_o, yr_o, ga_o, gr_o, lf_o, *rest,
                    tiles_per_seq, inv_scale):
    i, j = pl.program_id(0), pl.program_id(1)
    xn_s = rest[-1]
    n_heads = lf_o.shape[1]

    @pl.when(j == 0)
    def _():
        xn = _rms_norm(x_ref[...], g_ref[...]).astype(BF16)
        xn_s[...] = xn
        lf = _log_sigmoid(_dot_nt(xn, wf[...]) + bf_ref[...])
        lf_o[...] = lf[:, :n_heads]
        if tiles_per_seq is not None:
            c_o, carry_s = rest[0], rest[1]

            @pl.when(lax.rem(i, tiles_per_seq) == 0)
            def _():
                carry_s[...] = jnp.zeros_like(carry_s)

            c = _prefix_sum(lf, axis=0) + carry_s[...]
            carry_s[...] = c[c.shape[0] - 1:, :]
            hi, mid, lo = _split3_bf16(c * inv_scale)
            lane = lax.broadcasted_iota(jnp.int32, c.shape, 1)
            packed = jnp.where(
                lane < n_heads, hi,
                jnp.where(lane < 2 * n_heads, pltpu.roll(mid, n_heads, axis=1),
                          jnp.where(lane < 3 * n_heads, pltpu.roll(lo, 2 * n_heads, axis=1),
                                    jnp.where(lane == 3 * n_heads, 1.0, 0.0))))
            c_o[...] = packed.astype(c_o.dtype)

    xn = xn_s[...]
    for w, o in ((wq, q_o), (wga, ga_o), (wgr, gr_o)):
        o[...] = _dot_nt(xn, w[...]).astype(o.dtype)

    tm = xn.shape[0]
    for w, o16, o32 in ((wk, k16_o, k32_o), (wv, v16_o, v32_o), (wxr, None, xr_o),
                        (wyr, None, yr_o)):
        res = _dot_nt(xn, w[...])
        if o16 is not None:
            o16[...] = res.astype(o16.dtype)
        width = o32.shape[1]
        n_blocks = o32.shape[0] // tm
        blocks_per_step = w.shape[0] // width
        for bb in range(blocks_per_step):
            block = j * blocks_per_step + bb
            o32[pl.ds(block, tm, stride=n_blocks), :] = res[:, bb * width:(bb + 1) * width]


def _in_proj(x, g, w, tm, n_col_steps, seq=None):
    rows, d_model = x.shape
    n_heads, head_dim = w["n_heads"], w["head_dim"]
    cols = {}
    for arr, names in (("qkv", ("q", "k", "v")), ("rest", ("xr", "yr", "ga", "gr"))):
        start = 0
        for name in names:
            cols[name] = (arr, start, w["widths"][name])
            start += w["widths"][name]
    order = ("q", "k", "v", "xr", "yr", "ga", "gr")
    tiles = {name: cols[name][2] // n_col_steps for name in order}
    grid = (rows // tm, n_col_steps)

    def w_spec(name):
        first_block = cols[name][1] // tiles[name]
        return pl.BlockSpec((tiles[name], d_model), lambda i, j: (first_block + j, 0))

    def o_spec(name):
        return pl.BlockSpec((tm, tiles[name]), lambda i, j: (i, j))

    def o_shape(name, dtype):
        return jax.ShapeDtypeStruct((rows, cols[name][2]), dtype)

    def per_head():
        return pl.BlockSpec((tm, n_heads), lambda i, j: (i, 0))

    def flat(width, block_width):
        n_blocks = width // block_width
        return (pl.BlockSpec((tm * n_blocks, block_width), lambda i, j: (i, 0)),
                jax.ShapeDtypeStruct((rows * n_blocks, block_width), F32))

    kv_spec, kv_shape = flat(cols["k"][2], head_dim)
    rnn_spec, rnn_shape = flat(cols["xr"][2], w["rnn_block"])
    out_specs = [o_spec("q"), o_spec("k"), o_spec("v"), kv_spec, kv_spec,
                 rnn_spec, rnn_spec, o_spec("ga"), o_spec("gr"), per_head()]
    out_shape = [o_shape("q", BF16), o_shape("k", BF16), o_shape("v", BF16),
                 kv_shape, kv_shape,
                 rnn_shape, rnn_shape, o_shape("ga", BF16), o_shape("gr", BF16),
                 jax.ShapeDtypeStruct((rows, n_heads), F32)]
    scratch = [pltpu.VMEM((tm, d_model), BF16)]
    if seq is not None:
        out_specs.append(pl.BlockSpec((tm, LANES), lambda i, j: (i, 0)))
        out_shape.append(jax.ShapeDtypeStruct((rows, LANES), BF16))
        scratch.insert(0, pltpu.VMEM((1, LANES), F32))
    return pl.pallas_call(
        functools.partial(_in_proj_kernel,
                          tiles_per_seq=None if seq is None else seq // tm,
                          inv_scale=head_dim ** 0.5),
        grid=grid,
        in_specs=[pl.BlockSpec((tm, d_model), lambda i, j: (i, 0)),
                  pl.BlockSpec((1, d_model), lambda i, j: (0, 0))]
                 + [w_spec(name) for name in order]
                 + [pl.BlockSpec((LANES, d_model), lambda i, j: (0, 0)),
                    pl.BlockSpec((1, LANES), lambda i, j: (0, 0))],
        out_specs=out_specs,
        out_shape=out_shape,
        scratch_shapes=scratch,
        compiler_params=_params("arbitrary", "arbitrary"),
        name="in_proj",
    )(x, g, *[w[cols[name][0]] for name in order], w["f"], w["b_f"])


def _split3_bf16(x):
    hi = x.astype(BF16)
    r = x - hi.astype(F32)
    mid = r.astype(BF16)
    lo = (r - mid.astype(F32)).astype(BF16)
    return hi.astype(F32), mid.astype(F32), lo.astype(F32)


def _attn_prompt_kernel(q_ref, k_ref, v_ref, c_ref, *rest, tq, scale, n_heads, n_casts):
    o_ref = rest[n_casts]
    for w_ref, w16_ref in zip(rest[:n_casts], rest[n_casts + 1:2 * n_casts + 1]):
        w16_ref[...] = w_ref[...].astype(w16_ref.dtype)
    qa_s, ka_s, m_s, l_s, acc_s = rest[2 * n_casts + 1:]
    seq, head_dim = q_ref.shape
    h = pl.program_id(1)
    src = lax.broadcasted_iota(jnp.int32, (LANES, LANES), 0)
    dst = lax.broadcasted_iota(jnp.int32, (LANES, LANES), 1)
    one_src = src == 3 * n_heads
    part_of_dst = jnp.where(dst < 3, dst, dst - 3) * n_heads + h
    sel_q = jnp.where((dst < 3) & (src == part_of_dst), 1.0,
                      jnp.where((dst >= 3) & (dst < 6) & one_src, 1.0, 0.0))
    sel_k = jnp.where((dst >= 3) & (dst < 6) & (src == part_of_dst), -1.0,
                      jnp.where((dst < 3) & one_src, 1.0, 0.0))
    c_rows = c_ref[...]
    qa_s[:, :head_dim] = q_ref[...]
    qa_s[:, head_dim:] = _dot(c_rows, sel_q.astype(BF16)).astype(BF16)
    ka_s[:, :head_dim] = k_ref[...]
    ka_s[:, head_dim:] = _dot(c_rows, sel_k.astype(BF16)).astype(BF16)

    sigma = scale * math.log2(math.e)
    row = lax.broadcasted_iota(jnp.int32, (tq, tq), 0)
    col = lax.broadcasted_iota(jnp.int32, (tq, tq), 1)
    visible = col <= row
    groups = tq // LANES
    n_tiles = seq // tq
    for j in range(n_tiles):
        s_all = _dot_nt(qa_s[j * tq:, :], ka_s[j * tq:(j + 1) * tq, :]) * sigma
        v_j = v_ref[j * tq:(j + 1) * tq, :]
        for i in range(j, n_tiles):
            rows = slice(i * tq, (i + 1) * tq)
            s = s_all[(i - j) * tq:(i - j + 1) * tq, :]
            if i == j:
                s = jnp.where(visible, s, NEG_BIG)
            blk_max = jnp.broadcast_to(jnp.max(s, axis=-1, keepdims=True), (tq, LANES))
            m_new = blk_max if j == 0 else jnp.maximum(m_s[rows, :], blk_max)
            parts = [jnp.exp2(s[:, g * LANES:(g + 1) * LANES] - m_new) for g in range(groups)]
            l_new = parts[0]
            for p in parts[1:]:
                l_new = l_new + p
            pv = _dot(jnp.concatenate([p.astype(BF16) for p in parts], axis=1), v_j)
            if j > 0:
                alpha = jnp.exp2(m_s[rows, :] - m_new)
                l_new = alpha * l_s[rows, :] + l_new
                pv = alpha * acc_s[rows, :] + pv
            if i == j:
                l = jnp.sum(l_new, axis=-1, keepdims=True)
                o_ref[rows, :] = (pv / l).astype(o_ref.dtype)
            else:
                m_s[rows, :] = m_new
                l_s[rows, :] = l_new
                acc_s[rows, :] = pv


def _attn_prompt(q, k, v, c, batch, seq, n_heads, head_dim, tq, weights_to_cast):
    assert head_dim == LANES, "per-row softmax state is kept one vreg lane group wide"
    head_spec = pl.BlockSpec((seq, head_dim), lambda b, h: (b, h))
    n_steps = batch * n_heads
    cast_specs = []
    for w in weights_to_cast:
        assert w.shape[0] % (n_steps * 2 * SUBLANES) == 0
        cast_specs.append(pl.BlockSpec((w.shape[0] // n_steps, w.shape[1]),
                                       lambda b, h: (b * n_heads + h, 0)))
    attn, *casts = pl.pallas_call(
        functools.partial(_attn_prompt_kernel, tq=tq, scale=head_dim ** -0.5, n_heads=n_heads,
                          n_casts=len(weights_to_cast)),
        grid=(batch, n_heads),
        in_specs=[head_spec, head_spec, head_spec,
                  pl.BlockSpec((seq, LANES), lambda b, h: (b, 0))] + cast_specs,
        out_specs=[head_spec] + cast_specs,
        out_shape=[jax.ShapeDtypeStruct(q.shape, BF16)]
                  + [jax.ShapeDtypeStruct(w.shape, BF16) for w in weights_to_cast],
        scratch_shapes=[pltpu.VMEM((seq, head_dim + LANES), BF16),
                        pltpu.VMEM((seq, head_dim + LANES), BF16),
                        pltpu.VMEM((seq, LANES), F32),
                        pltpu.VMEM((seq, LANES), F32),
                        pltpu.VMEM((seq, head_dim), F32)],
        compiler_params=_params("parallel", "parallel"),
        name="attn_prompt",
    )(q, k, v, c, *weights_to_cast)
    return attn, casts


def _paged_forget_bias(lf_pages, lf_new, n_heads):
    rows, page = lf_pages.shape
    n_pages = rows // n_heads
    within = _prefix_sum(lf_pages, axis=1)
    totals = jnp.broadcast_to(within[:, page - 1:], (rows, page))
    upto = _prefix_sum(totals, axis=0, start=n_heads)
    c_past = within + (upto - totals)
    c_new = upto[rows - n_heads:, 0:1] + lf_new
    bias = jnp.concatenate([c_new] * n_pages, axis=0) - c_past
    src = lax.broadcasted_iota(jnp.int32, (page, page * n_heads), 0)
    dst = lax.broadcasted_iota(jnp.int32, (page, page * n_heads), 1)
    expand = jnp.where(dst // n_heads == src, 1.0, 0.0).astype(BF16)
    parts = jnp.concatenate(_split3_bf16(bias), axis=0).astype(BF16)
    wide = _dot(parts, expand)
    return wide[:rows] + wide[rows:2 * rows] + wide[2 * rows:]


def _paged_attn_step(step, q_ref, kn_ref, vn_ref, lfn_ref, k_refs, v_refs, load_lf_pages, o_ref,
                     s_s, bias_s, m_s, l_s, acc_s, *, steps_per_seq, scale):
    pps = len(k_refs)
    page, n_heads, head_dim = k_refs[0].shape[1:]
    pw = page * n_heads
    part = lax.rem(step, steps_per_seq)

    head_of_row = lax.broadcasted_iota(jnp.int32, (n_heads, pw), 0)
    lane = lax.broadcasted_iota(jnp.int32, (n_heads, pw), 1)
    on_diag = head_of_row == lane % n_heads
    q = q_ref[0]

    @pl.when(part == 0)
    def _():
        bias_s[...] = _paged_forget_bias(
            jnp.concatenate(load_lf_pages(), axis=0), lfn_ref[0], n_heads)
        kn = kn_ref[0].astype(BF16).astype(F32)
        m_s[...] = jnp.sum(q.astype(F32) * kn, axis=-1, keepdims=True) * scale
        l_s[...] = jnp.ones_like(l_s)
        acc_s[...] = vn_ref[0].astype(BF16).astype(F32)

    for p in range(pps):
        kf = k_refs[p][0].reshape(pw, head_dim).astype(BF16)
        first = pl.multiple_of((part * pps + p) * n_heads, n_heads)
        s = _dot_nt(q, kf) * scale + bias_s[pl.ds(first, n_heads), :]
        s_s[:, p * pw:(p + 1) * pw] = jnp.where(on_diag, s, NEG_BIG)

    s_all = s_s[...]
    m_old = m_s[...]
    m_new = jnp.maximum(m_old, jnp.max(s_all, axis=-1, keepdims=True))
    p_all = jnp.exp(s_all - m_new)
    alpha = jnp.exp(m_old - m_new)
    l_s[...] = alpha * l_s[...] + jnp.sum(p_all, axis=-1, keepdims=True)
    m_s[...] = m_new
    p16 = p_all.astype(BF16)
    acc = alpha * acc_s[...]
    for p in range(pps):
        vf = v_refs[p][0].reshape(pw, head_dim).astype(BF16)
        acc = acc + _dot(p16[:, p * pw:(p + 1) * pw], vf)
    acc_s[...] = acc

    @pl.when(part == steps_per_seq - 1)
    def _():
        o_ref[0] = (acc_s[...] / l_s[...]).astype(o_ref.dtype)


def _attn_sample_kernel(pt_ref, q_ref, kn_ref, vn_ref, lfn_ref, lf_all_ref, *refs,
                        pages_per_step, n_pages):
    k_refs = refs[:pages_per_step]
    v_refs = refs[pages_per_step:2 * pages_per_step]
    o_ref = refs[2 * pages_per_step]
    scratch = refs[2 * pages_per_step + 1:]
    b = pl.program_id(0)

    def load_lf_pages():
        return [lf_all_ref[pt_ref[b, p]] for p in range(n_pages)]

    _paged_attn_step(b * pl.num_programs(1) + pl.program_id(1),
                     q_ref, kn_ref, vn_ref, lfn_ref, k_refs, v_refs, load_lf_pages, o_ref,
                     *scratch,
                     steps_per_seq=n_pages // pages_per_step, scale=q_ref.shape[2] ** -0.5)


def _attn_sample(q, k_new, v_new, lf_new, cache_k, cache_v, cache_lf, page_table, pages_per_step):
    batch, n_pages = page_table.shape
    _, page, n_heads, head_dim = cache_k.shape
    pw = page * n_heads
    steps_per_seq = n_pages // pages_per_step
    assert page == LANES, "one page of log-forget values per head is one lane row"

    def row_spec(shape):
        return pl.BlockSpec((1,) + shape, lambda b, s, pt: (b, 0, 0))

    def page_spec(p):
        return pl.BlockSpec((1, page, n_heads, head_dim),
                            lambda b, s, pt: (pt[b, s * pages_per_step + p], 0, 0, 0))

    lf_spec = pl.BlockSpec(cache_lf.shape, lambda b, s, pt: (0, 0, 0),
                           pipeline_mode=pl.Buffered(1))

    return pl.pallas_call(
        functools.partial(_attn_sample_kernel, pages_per_step=pages_per_step, n_pages=n_pages),
        grid_spec=pltpu.PrefetchScalarGridSpec(
            num_scalar_prefetch=1,
            grid=(batch, steps_per_seq),
            in_specs=[row_spec((n_heads, head_dim))] * 3 + [row_spec((n_heads, 1)), lf_spec]
                     + [page_spec(p) for p in range(pages_per_step)] * 2,
            out_specs=row_spec((n_heads, head_dim)),
            scratch_shapes=[pltpu.VMEM((n_heads, pages_per_step * pw), F32),
                            pltpu.VMEM((n_pages * n_heads, pw), F32),
                            pltpu.VMEM((n_heads, 1), F32),
                            pltpu.VMEM((n_heads, 1), F32),
                            pltpu.VMEM((n_heads, head_dim), F32)],
        ),
        out_shape=jax.ShapeDtypeStruct((batch, n_heads, head_dim), BF16),
        compiler_params=_params("arbitrary", "arbitrary"),
        name="attn_sample",
    )(page_table, q, k_new, v_new, lf_new, cache_lf,
      *([cache_k] * pages_per_step), *([cache_v] * pages_per_step))


def _rglru_gates(conv, r_pre, i_pre, b_rg, b_ig, neg_c_softplus):
    r = jax.nn.sigmoid(r_pre + b_rg)
    i_g = jax.nn.sigmoid(i_pre + b_ig)
    log_a = r * neg_c_softplus
    a = jnp.exp(log_a)
    b = jnp.sqrt(-jnp.tanh(log_a) * (1.0 + a * a)) * (i_g * conv)
    return a, b


def _rglru_prompt_kernel(xr_ref, yr_ref, cw_ref, cb_ref, wrg_ref, wig_ref, brg_ref, big_ref,
                         lam_ref, o_ref, hl_ref, xc_s, conv_s, rp_s, ip_s, a_s, b_s, h_s):
    c = pl.program_id(0)
    nb, tc, n_blocks, bw = xr_ref.shape
    halo = CONV_WIDTH - 1
    rows = nb * tc

    @pl.when(c == 0)
    def _():
        xc_s[:, 0:halo] = jnp.zeros((nb, halo, n_blocks, bw), F32)
        h_s[...] = jnp.zeros_like(h_s)

    @pl.when(c > 0)
    def _():
        xc_s[:, 0:halo] = xc_s[:, tc:tc + halo]

    xc_s[:, halo:halo + tc] = xr_ref[...]

    conv = cb_ref[...] + xc_s[:, 0:tc] * cw_ref[0]
    for i in range(1, CONV_WIDTH):
        conv = conv + xc_s[:, i:i + tc] * cw_ref[i]

    conv_s[...] = conv.reshape(rows * n_blocks, bw)
    for n in range(n_blocks):
        xn = conv_s[pl.ds(n, rows, stride=n_blocks), :].astype(BF16)
        rp_s[pl.ds(n, rows, stride=n_blocks), :] = _dot(xn, wrg_ref[n])
        ip_s[pl.ds(n, rows, stride=n_blocks), :] = _dot(xn, wig_ref[n])

    neg_c_softplus = -LRU_C * _softplus(-lam_ref[...])
    a, b = _rglru_gates(conv,
                        rp_s[...].reshape(nb, tc, n_blocks, bw),
                        ip_s[...].reshape(nb, tc, n_blocks, bw),
                        brg_ref[...], big_ref[...], neg_c_softplus)
    a_s[...] = a
    b_s[...] = b

    def step(t, hs):
        new = tuple(a_s[j, t] * hs[j] + b_s[j, t] for j in range(nb))
        for j in range(nb):
            b_s[j, t] = new[j]
        return new

    hs = lax.fori_loop(0, tc, step, tuple(h_s[j] for j in range(nb)), unroll=8)
    for j in range(nb):
        h_s[j] = hs[j]

    conv_s[...] = (b_s[...] * _gelu_tanh(yr_ref[...])).reshape(rows * n_blocks, bw)
    for n in range(n_blocks):
        o_ref[:, :, n * bw:(n + 1) * bw] = (
            conv_s[pl.ds(n, rows, stride=n_blocks), :].reshape(nb, tc, bw).astype(o_ref.dtype))
    hl_ref[...] = h_s[...]


def _rglru_prompt(xr, yr, w, batch, seq, tc):
    n_blocks, bw = w["n_rnn_blocks"], w["rnn_block"]
    xr4 = xr.reshape(batch, seq, n_blocks, bw)
    yr4 = yr.reshape(batch, seq, n_blocks, bw)
    rows = batch * tc
    halo = CONV_WIDTH - 1

    def const(shape):
        return pl.BlockSpec(shape, lambda c: (0,) * len(shape))

    seq_spec = pl.BlockSpec((batch, tc, n_blocks, bw), lambda c: (0, c, 0, 0))
    out, h_last = pl.pallas_call(
        _rglru_prompt_kernel,
        grid=(seq // tc,),
        in_specs=[seq_spec, seq_spec,
                  const((CONV_WIDTH, n_blocks, bw)), const((n_blocks, bw)),
                  const((n_blocks, bw, bw)), const((n_blocks, bw, bw)),
                  const((n_blocks, bw)), const((n_blocks, bw)), const((n_blocks, bw))],
        out_specs=[pl.BlockSpec((batch, tc, n_blocks * bw), lambda c: (0, c, 0)),
                   const((batch, n_blocks, bw))],
        out_shape=[jax.ShapeDtypeStruct((batch, seq, n_blocks * bw), BF16),
                   jax.ShapeDtypeStruct((batch, n_blocks, bw), F32)],
        scratch_shapes=[pltpu.VMEM((batch, tc + halo, n_blocks, bw), F32),
                        pltpu.VMEM((rows * n_blocks, bw), F32),
                        pltpu.VMEM((rows * n_blocks, bw), F32),
                        pltpu.VMEM((rows * n_blocks, bw), F32),
                        pltpu.VMEM((batch, tc, n_blocks, bw), F32),
                        pltpu.VMEM((batch, tc, n_blocks, bw), F32),
                        pltpu.VMEM((batch, n_blocks, bw), F32)],
        compiler_params=_params("arbitrary"),
        name="rglru_prompt",
    )(xr4, yr4, w["conv_w3"], w["conv_b2"], w["w_rg"], w["w_ig"], w["b_rg2"], w["b_ig2"],
      w["lam2"])
    return out.reshape(batch * seq, n_blocks * bw), h_last.reshape(batch, n_blocks * bw)


def _rglru_sample_kernel(xr_ref, yr_ref, buf_ref, h0_ref, cw_ref, cb_ref, wrg_ref, wig_ref,
                         brg_ref, big_ref, lam_ref, o_ref, h_ref, nbuf_ref):
    halo = CONV_WIDTH - 1
    n_blocks, bw = wrg_ref.shape[0], wrg_ref.shape[1]
    xr = xr_ref[...]
    conv = cb_ref[...] + xr * cw_ref[halo:halo + 1, :]
    for i in range(halo):
        conv = conv + buf_ref[i] * cw_ref[i:i + 1, :]
    r_parts, i_parts = [], []
    for n in range(n_blocks):
        xn = conv[:, n * bw:(n + 1) * bw].astype(BF16)
        r_parts.append(_dot(xn, wrg_ref[n]))
        i_parts.append(_dot(xn, wig_ref[n]))
    neg_c_softplus = -LRU_C * _softplus(-lam_ref[...])
    a, b = _rglru_gates(conv, jnp.concatenate(r_parts, axis=1), jnp.concatenate(i_parts, axis=1),
                        brg_ref[...], big_ref[...], neg_c_softplus)
    h = a * h0_ref[...] + b
    h_ref[...] = h
    o_ref[...] = h * _gelu_tanh(yr_ref[...])
    for i in range(halo - 1):
        nbuf_ref[i] = buf_ref[i + 1]
    nbuf_ref[halo - 1] = xr


def _rglru_sample(xr, yr, conv_buf_t, h0, w):
    rows, width = xr.shape
    halo = CONV_WIDTH - 1
    return pl.pallas_call(
        _rglru_sample_kernel,
        out_shape=[jax.ShapeDtypeStruct((rows, width), F32),
                   jax.ShapeDtypeStruct((rows, width), F32),
                   jax.ShapeDtypeStruct((halo, rows, width), F32)],
        compiler_params=pltpu.CompilerParams(vmem_limit_bytes=VMEM_LIMIT_BYTES),
        name="rglru_sample",
    )(xr, yr, conv_buf_t, h0, w["conv_w"], w["conv_b"], w["w_rg"], w["w_ig"],
      w["b_rg"], w["b_ig"], w["lam"])


def _out_proj_kernel(attn_ref, rnn_ref, ga_ref, gr_ref, x_ref, woa_ref, wor_ref, wout_ref,
                     gpm_ref, gpf_ref, h_o, hn_o, *, n_parts):
    part_rows = x_ref.shape[0] // n_parts
    for part in range(n_parts):
        r = slice(part * part_rows, (part + 1) * part_rows)
        o = (jax.nn.sigmoid(ga_ref[r, :].astype(F32))
             * _dot(attn_ref[r, :].astype(BF16), woa_ref[...])
             + jax.nn.sigmoid(gr_ref[r, :].astype(F32))
             * _dot(rnn_ref[r, :].astype(BF16), wor_ref[...]))
        mix = _dot(o.astype(BF16), wout_ref[...])
        h = x_ref[r, :] + _rms_norm(mix, gpm_ref[...])
        h_o[r, :] = h
        hn_o[r, :] = _rms_norm(h, gpf_ref[...]).astype(hn_o.dtype)


def _out_proj(attn, rnn, ga, gr, x, w, tm, n_parts=1):
    rows, d_model = x.shape

    def row(width):
        return pl.BlockSpec((tm, width), lambda i: (i, 0))

    def const(shape):
        return pl.BlockSpec(shape, lambda i: (0, 0), pipeline_mode=pl.Buffered(1))

    return pl.pallas_call(
        functools.partial(_out_proj_kernel, n_parts=n_parts),
        grid=(rows // tm,),
        in_specs=[row(attn.shape[1]), row(rnn.shape[1]), row(d_model), row(d_model), row(d_model),
                  const(w["o_attn"].shape), const(w["o_rnn"].shape), const(w["out"].shape),
                  const((1, d_model)), const((1, d_model))],
        out_specs=[row(d_model), row(d_model)],
        out_shape=[jax.ShapeDtypeStruct((rows, d_model), F32),
                   jax.ShapeDtypeStruct((rows, d_model), BF16)],
        compiler_params=_params("parallel"),
        name="out_proj",
    )(attn, rnn, ga, gr, x, w["o_attn"], w["o_rnn"], w["out"], w["g_post_mix"], w["g_pre_ffn"])


def _ffn_kernel(hn_ref, h_ref, wg_ref, wu_ref, wd_ref, g_ref, y_o, acc_s):
    j = pl.program_id(1)

    @pl.when(j == 0)
    def _():
        acc_s[...] = jnp.zeros_like(acc_s)

    hn = hn_ref[...]
    gate = _dot(hn, wg_ref[...])
    up = _dot(hn, wu_ref[...])
    acc_s[...] += _dot((jax.nn.silu(gate) * up).astype(BF16), wd_ref[...])

    @pl.when(j == pl.num_programs(1) - 1)
    def _():
        y_o[...] = h_ref[...] + _rms_norm(acc_s[...], g_ref[...])


def _ffn(hn, h, w, tm, tf):
    rows, d_model = h.shape
    return pl.pallas_call(
        _ffn_kernel,
        grid=(rows // tm, w["gate"].shape[1] // tf),
        in_specs=[pl.BlockSpec((tm, d_model), lambda i, j: (i, 0)),
                  pl.BlockSpec((tm, d_model), lambda i, j: (i, 0)),
                  pl.BlockSpec((d_model, tf), lambda i, j: (0, j)),
                  pl.BlockSpec((d_model, tf), lambda i, j: (0, j)),
                  pl.BlockSpec((tf, d_model), lambda i, j: (j, 0)),
                  pl.BlockSpec((1, d_model), lambda i, j: (0, 0))],
        out_specs=pl.BlockSpec((tm, d_model), lambda i, j: (i, 0)),
        out_shape=jax.ShapeDtypeStruct((rows, d_model), F32),
        scratch_shapes=[pltpu.VMEM((tm, d_model), F32)],
        compiler_params=_params("parallel", "arbitrary"),
        name="ffn",
    )(hn, h, w["gate"], w["up"], w["down"], w["g_post_ffn"])


def _layer_weights(l, n_heads, head_dim, g_pre_mix, w_in, b_f, conv_w, conv_b, w_rg, b_rg, w_ig,
                   b_ig, lru_lambda, w_o_attn, w_o_rnn, w_out, g_post_mix, g_pre_ffn, w_gate,
                   w_up, w_down, g_post_ffn):
    d_model = w_in.shape[1]
    n_blocks, bw = w_rg.shape[1], w_rg.shape[2]
    d_rnn = n_blocks * bw
    aw = n_heads * head_dim
    wi_t = jnp.swapaxes(w_in[l], 0, 1)
    f_lo, f_hi = 3 * aw, 3 * aw + n_heads
    cols = {
        "qkv": _cast_rows(wi_t, 0, f_lo, 512),
        "rest": _cast_rows(wi_t, f_hi, wi_t.shape[0] - f_hi, 512),
        "f": jnp.pad(wi_t[f_lo:f_hi].astype(BF16), ((0, LANES - n_heads), (0, 0))),
        "b_f": jnp.pad(b_f[l].reshape(1, n_heads), ((0, 0), (0, LANES - n_heads))),
        "widths": {"q": aw, "k": aw, "v": aw, "xr": d_rnn, "yr": d_rnn,
                   "ga": d_model, "gr": d_model},
        "n_heads": n_heads,
        "head_dim": head_dim,
        "rnn_block": bw,
    }
    rnn = {
        "n_rnn_blocks": n_blocks, "rnn_block": bw,
        "conv_w": conv_w[l], "conv_b": conv_b[l].reshape(1, d_rnn),
        "b_rg": b_rg[l].reshape(1, d_rnn), "b_ig": b_ig[l].reshape(1, d_rnn),
        "lam": lru_lambda[l].reshape(1, d_rnn),
        "conv_w3": conv_w[l].reshape(CONV_WIDTH, n_blocks, bw),
        "conv_b2": conv_b[l].reshape(n_blocks, bw),
        "b_rg2": b_rg[l].reshape(n_blocks, bw), "b_ig2": b_ig[l].reshape(n_blocks, bw),
        "lam2": lru_lambda[l].reshape(n_blocks, bw),
        "w_rg": w_rg[l].astype(BF16), "w_ig": w_ig[l].astype(BF16),
    }
    mix = {"o_attn": w_o_attn[l], "o_rnn": w_o_rnn[l], "out": w_out[l],
           "g_post_mix": g_post_mix[l].reshape(1, d_model),
           "g_pre_ffn": g_pre_ffn[l].reshape(1, d_model)}
    ffn = {"gate": w_gate[l], "up": w_up[l], "down": w_down[l],
           "g_post_ffn": g_post_ffn[l].reshape(1, d_model)}
    return g_pre_mix[l].reshape(1, d_model), cols, rnn, mix, ffn


def kernel(x_prompt, x_sample, cache_k, cache_v, cache_logf, state_h, state_conv, page_table,
           g_pre_mix, w_in, b_f, conv_w, conv_b, w_rg, b_rg, w_ig, b_ig, lru_lambda,
           w_o_attn, w_o_rnn, w_out, g_post_mix, g_pre_ffn, w_gate, w_up, w_down, g_post_ffn):
    batch, seq, d_model = x_prompt.shape
    dec_batch, dec_seq, _ = x_sample.shape
    assert dec_seq == 1, "sample group carries one new token per sequence"
    depth, n_phys, page, n_heads, head_dim = cache_k.shape
    aw = n_heads * head_dim
    d_rnn = state_h.shape[2]
    n_pages = page_table.shape[1]
    assert LANES % n_heads == 0 and (page * n_heads) % LANES == 0

    xp = x_prompt.reshape(batch * seq, d_model)
    xs = x_sample.reshape(dec_batch, d_model)
    outs = {name: [] for name in ("kp", "vp", "lp", "hp", "cp", "ks", "vs", "ls", "hs", "cs")}

    for l in range(depth):
        g_in, w_cols, w_rnn, w_mix, w_ffn = _layer_weights(
            l, n_heads, head_dim, g_pre_mix, w_in, b_f, conv_w, conv_b, w_rg, b_rg, w_ig, b_ig,
            lru_lambda, w_o_attn, w_o_rnn, w_out, g_post_mix, g_pre_ffn, w_gate, w_up, w_down,
            g_post_ffn)

        q, k16, v16, k, v, xr, yr, ga, gr, lf, c = _in_proj(xp, g_in, w_cols, tm=512,
                                                            n_col_steps=4, seq=seq)
        qs, _, _, ks, vs, xrs, yrs, gas, grs, lfs = _in_proj(xs, g_in, w_cols, tm=dec_batch,
                                                             n_col_steps=4)
        ks = ks.reshape(dec_batch, n_heads, head_dim)
        vs = vs.reshape(dec_batch, n_heads, head_dim)

        attn, (woa, wor, wout, wg, wu, wd) = _attn_prompt(
            q, k16, v16, c, batch, seq, n_heads, head_dim, tq=256,
            weights_to_cast=[w_mix["o_attn"], w_mix["o_rnn"], w_mix["out"],
                             w_ffn["gate"], w_ffn["up"], w_ffn["down"]])
        w_mix = dict(w_mix, o_attn=woa, o_rnn=wor, out=wout)
        w_ffn = dict(w_ffn, gate=wg, up=wu, down=wd)
        rnn, h_last = _rglru_prompt(xr, yr, w_rnn, batch, seq, tc=128)
        h, hn = _out_proj(attn, rnn, ga, gr, xp, w_mix, tm=512, n_parts=2)
        xp_next = _ffn(hn, h, w_ffn, tm=512, tf=512)
        outs["kp"].append(k.reshape(batch, seq, n_heads, head_dim))
        outs["vp"].append(v.reshape(batch, seq, n_heads, head_dim))
        outs["lp"].append(lf.reshape(batch, seq, n_heads))
        outs["hp"].append(h_last)
        outs["cp"].append(xr.reshape(batch, seq, d_rnn // w_cols["rnn_block"], w_cols["rnn_block"])
                          [:, seq - (CONV_WIDTH - 1):].reshape(batch, CONV_WIDTH - 1, d_rnn))

        attn_s = _attn_sample(
            qs.reshape(dec_batch, n_heads, head_dim), ks, vs, lfs.reshape(dec_batch, n_heads, 1),
            cache_k[l], cache_v[l], jnp.swapaxes(cache_logf[l], 1, 2),
            page_table, pages_per_step=n_pages)
        rnn, h_new, buf_new = _rglru_sample(
            xrs.reshape(dec_batch, d_rnn), yrs.reshape(dec_batch, d_rnn),
            jnp.swapaxes(state_conv[l], 0, 1), state_h[l], w_rnn)
        h, hn = _out_proj(attn_s.reshape(dec_batch, aw), rnn, gas, grs, xs, w_mix, tm=dec_batch)
        xs_next = _ffn(hn, h, w_ffn, tm=dec_batch, tf=w_gate.shape[2] // 4)
        outs["ks"].append(ks.reshape(dec_batch, 1, n_heads, head_dim))
        outs["vs"].append(vs.reshape(dec_batch, 1, n_heads, head_dim))
        outs["ls"].append(lfs.reshape(dec_batch, 1, n_heads))
        outs["hs"].append(h_new)
        outs["cs"].append(jnp.swapaxes(buf_new, 0, 1))

        xp, xs = xp_next, xs_next

    st = {name: jnp.stack(vals) for name, vals in outs.items()}
    return (xp.reshape(batch, seq, d_model), xs.reshape(dec_batch, 1, d_model),
            st["kp"], st["vp"], st["lp"], st["hp"], st["cp"],
            st["ks"], st["vs"], st["ls"], st["hs"], st["cs"])
```

```python
import functools
import math

import jax
import jax.numpy as jnp
from jax import lax
from jax.experimental import pallas as pl
from jax.experimental.pallas import tpu as pltpu

F32 = jnp.float32
BF16 = jnp.bfloat16

EPS = 1e-6
LRU_C = 8.0
CONV_WIDTH = 4
NEG_BIG = -1e30

SUBLANES = 8
LANES = 128
VMEM_LIMIT_BYTES = 56 * 1024 * 1024


def _params(*semantics):
    return pltpu.CompilerParams(dimension_semantics=semantics,
                                vmem_limit_bytes=VMEM_LIMIT_BYTES)


def _rms_norm(x, g):
    inv = lax.rsqrt(jnp.mean(x * x, axis=-1, keepdims=True) + EPS)
    return (x * inv) * g


def _log_sigmoid(x):
    return jnp.minimum(x, 0.0) - jnp.log1p(jnp.exp(-jnp.abs(x)))


def _softplus(x):
    return jnp.maximum(x, 0.0) + jnp.log1p(jnp.exp(-jnp.abs(x)))


def _gelu_tanh(x):
    c = math.sqrt(2.0 / math.pi)
    half = 0.5 * x
    return half + half * jnp.tanh(x * (c + (c * 0.044715) * (x * x)))


def _dot(a, b):
    return jnp.dot(a, b, preferred_element_type=F32)


def _dot_nt(a, b):
    return lax.dot_general(a, b, (((1,), (1,)), ((), ())), preferred_element_type=F32)


def _cast_kernel(x_ref, o_ref):
    o_ref[...] = x_ref[...].astype(o_ref.dtype)


def _cast_shifted_kernel(a_ref, b_ref, o_ref, *, shift):
    o_ref[...] = jnp.concatenate([a_ref[shift:, :], b_ref[:shift, :]], axis=0).astype(o_ref.dtype)


def _cast_rows(w_t, first_row, n_rows, tile):
    width = w_t.shape[1]
    assert n_rows % tile == 0 and tile % SUBLANES == 0
    shift = first_row % tile
    base = first_row // tile
    out_spec = pl.BlockSpec((tile, width), lambda j: (j, 0))
    out_shape = jax.ShapeDtypeStruct((n_rows, width), BF16)
    if shift == 0:
        return pl.pallas_call(
            _cast_kernel, grid=(n_rows // tile,),
            in_specs=[pl.BlockSpec((tile, width), lambda j: (base + j, 0))],
            out_specs=out_spec, out_shape=out_shape,
            compiler_params=_params("parallel"), name="cast",
        )(w_t)
    assert shift == SUBLANES and first_row + n_rows <= w_t.shape[0]
    per_tile = tile // SUBLANES
    return pl.pallas_call(
        functools.partial(_cast_shifted_kernel, shift=shift), grid=(n_rows // tile,),
        in_specs=[pl.BlockSpec((tile, width), lambda j: (base + j, 0)),
                  pl.BlockSpec((SUBLANES, width), lambda j: ((base + j + 1) * per_tile, 0))],
        out_specs=out_spec, out_shape=out_shape,
        compiler_params=_params("parallel"), name="cast_shifted",
    )(w_t, w_t)


def _prefix_sum(x, axis, start=1):
    idx = lax.broadcasted_iota(jnp.int32, x.shape, axis)
    shift = start
    while shift < x.shape[axis]:
        x = x + jnp.where(idx >= shift, pltpu.roll(x, shift, axis=axis), 0.0)
        shift *= 2
    return x


def _in_proj_kernel(x_ref, g_ref, wq, wk, wv, wxr, wyr, wga, wgr, wf, bf_ref,
                    q_o, k16_o, v16_o, k32_o, v32_o, xr_o, yr_o, ga_o, gr_o, lf_o, *rest,
                    tiles_per_seq, inv_scale):
    i, j = pl.program_id(0), pl.program_id(1)
    xn_s = rest[-1]
    n_heads = lf_o.shape[1]

    @pl.when(j == 0)
    def _():
        xn = _rms_norm(x_ref[...], g_ref[...]).astype(BF16)
        xn_s[...] = xn
        lf = _log_sigmoid(_dot_nt(xn, wf[...]) + bf_ref[...])
        lf_o[...] = lf[:, :n_heads]
        if tiles_per_seq is not None:
            c_o, carry_s = rest[0], rest[1]

            @pl.when(lax.rem(i, tiles_per_seq) == 0)
            def _():
                carry_s[...] = jnp.zeros_like(carry_s)

            c = _prefix_sum(lf, axis=0) + carry_s[...]
            carry_s[...] = c[c.shape[0] - 1:, :]
            hi, mid, lo = _split3_bf16(c * inv_scale)
            lane = lax.broadcasted_iota(jnp.int32, c.shape, 1)
            packed = jnp.where(
                lane < n_heads, hi,
                jnp.where(lane < 2 * n_heads, pltpu.roll(mid, n_heads, axis=1),
                          jnp.where(lane < 3 * n_heads, pltpu.roll(lo, 2 * n_heads, axis=1),
                                    jnp.where(lane == 3 * n_heads, 1.0, 0.0))))
            c_o[...] = packed.astype(c_o.dtype)

    xn = xn_s[...]
    for w, o in ((wq, q_o), (wga, ga_o), (wgr, gr_o)):
        o[...] = _dot_nt(xn, w[...]).astype(o.dtype)

    tm = xn.shape[0]
    for w, o16, o32 in ((wk, k16_o, k32_o), (wv, v16_o, v32_o), (wxr, None, xr_o),
                        (wyr, None, yr_o)):
        res = _dot_nt(xn, w[...])
        if o16 is not None:
            o16[...] = res.astype(o16.dtype)
        width = o32.shape[1]
        n_blocks = o32.shape[0] // tm
        blocks_per_step = w.shape[0] // width
        for bb in range(blocks_per_step):
            block = j * blocks_per_step + bb
            o32[pl.ds(block, tm, stride=n_blocks), :] = res[:, bb * width:(bb + 1) * width]


def _in_proj(x, g, w, tm, n_col_steps, seq=None):
    rows, d_model = x.shape
    n_heads, head_dim = w["n_heads"], w["head_dim"]
    cols = {}
    for arr, names in (("qkv", ("q", "k", "v")), ("rest", ("xr", "yr", "ga", "gr"))):
        start = 0
        for name in names:
            cols[name] = (arr, start, w["widths"][name])
            start += w["widths"][name]
    order = ("q", "k", "v", "xr", "yr", "ga", "gr")
    tiles = {name: cols[name][2] // n_col_steps for name in order}
    grid = (rows // tm, n_col_steps)

    def w_spec(name):
        first_block = cols[name][1] // tiles[name]
        return pl.BlockSpec((tiles[name], d_model), lambda i, j: (first_block + j, 0))

    def o_spec(name):
        return pl.BlockSpec((tm, tiles[name]), lambda i, j: (i, j))

    def o_shape(name, dtype):
        return jax.ShapeDtypeStruct((rows, cols[name][2]), dtype)

    def per_head():
        return pl.BlockSpec((tm, n_heads), lambda i, j: (i, 0))

    def flat(width, block_width):
        n_blocks = width // block_width
        return (pl.BlockSpec((tm * n_blocks, block_width), lambda i, j: (i, 0)),
                jax.ShapeDtypeStruct((rows * n_blocks, block_width), F32))

    kv_spec, kv_shape = flat(cols["k"][2], head_dim)
    rnn_spec, rnn_shape = flat(cols["xr"][2], w["rnn_block"])
    out_specs = [o_spec("q"), o_spec("k"), o_spec("v"), kv_spec, kv_spec,
                 rnn_spec, rnn_spec, o_spec("ga"), o_spec("gr"), per_head()]
    out_shape = [o_shape("q", BF16), o_shape("k", BF16), o_shape("v", BF16),
                 kv_shape, kv_shape,
                 rnn_shape, rnn_shape, o_shape("ga", BF16), o_shape("gr", BF16),
                 jax.ShapeDtypeStruct((rows, n_heads), F32)]
    scratch = [pltpu.VMEM((tm, d_model), BF16)]
    if seq is not None:
        out_specs.append(pl.BlockSpec((tm, LANES), lambda i, j: (i, 0)))
        out_shape.append(jax.ShapeDtypeStruct((rows, LANES), BF16))
        scratch.insert(0, pltpu.VMEM((1, LANES), F32))
    return pl.pallas_call(
        functools.partial(_in_proj_kernel,
                          tiles_per_seq=None if seq is None else seq // tm,
                          inv_scale=head_dim ** 0.5),
        grid=grid,
        in_specs=[pl.BlockSpec((tm, d_model), lambda i, j: (i, 0)),
                  pl.BlockSpec((1, d_model), lambda i, j: (0, 0))]
                 + [w_spec(name) for name in order]
                 + [pl.BlockSpec((LANES, d_model), lambda i, j: (0, 0)),
                    pl.BlockSpec((1, LANES), lambda i, j: (0, 0))],
        out_specs=out_specs,
        out_shape=out_shape,
        scratch_shapes=scratch,
        compiler_params=_params("arbitrary", "arbitrary"),
        name="in_proj",
    )(x, g, *[w[cols[name][0]] for name in order], w["f"], w["b_f"])


def _split3_bf16(x):
    hi = x.astype(BF16)
    r = x - hi.astype(F32)
    mid = r.astype(BF16)
    lo = (r - mid.astype(F32)).astype(BF16)
    return hi.astype(F32), mid.astype(F32), lo.astype(F32)


def _attn_prompt_kernel(q_ref, k_ref, v_ref, c_ref, *rest, tq, scale, n_heads, n_casts):
    o_ref = rest[n_casts]
    for w_ref, w16_ref in zip(rest[:n_casts], rest[n_casts + 1:2 * n_casts + 1]):
        w16_ref[...] = w_ref[...].astype(w16_ref.dtype)
    qa_s, ka_s, m_s, l_s, acc_s = rest[2 * n_casts + 1:]
    seq, head_dim = q_ref.shape
    h = pl.program_id(1)
    src = lax.broadcasted_iota(jnp.int32, (LANES, LANES), 0)
    dst = lax.broadcasted_iota(jnp.int32, (LANES, LANES), 1)
    one_src = src == 3 * n_heads
    part_of_dst = jnp.where(dst < 3, dst, dst - 3) * n_heads + h
    sel_q = jnp.where((dst < 3) & (src == part_of_dst), 1.0,
                      jnp.where((dst >= 3) & (dst < 6) & one_src, 1.0, 0.0))
    sel_k = jnp.where((dst >= 3) & (dst < 6) & (src == part_of_dst), -1.0,
                      jnp.where((dst < 3) & one_src, 1.0, 0.0))
    c_rows = c_ref[...]
    qa_s[:, :head_dim] = q_ref[...]
    qa_s[:, head_dim:] = _dot(c_rows, sel_q.astype(BF16)).astype(BF16)
    ka_s[:, :head_dim] = k_ref[...]
    ka_s[:, head_dim:] = _dot(c_rows, sel_k.astype(BF16)).astype(BF16)

    sigma = scale * math.log2(math.e)
    row = lax.broadcasted_iota(jnp.int32, (tq, tq), 0)
    col = lax.broadcasted_iota(jnp.int32, (tq, tq), 1)
    visible = col <= row
    groups = tq // LANES
    n_tiles = seq // tq
    for j in range(n_tiles):
        s_all = _dot_nt(qa_s[j * tq:, :], ka_s[j * tq:(j + 1) * tq, :]) * sigma
        v_j = v_ref[j * tq:(j + 1) * tq, :]
        for i in range(j, n_tiles):
            rows = slice(i * tq, (i + 1) * tq)
            s = s_all[(i - j) * tq:(i - j + 1) * tq, :]
            if i == j:
                s = jnp.where(visible, s, NEG_BIG)
            blk_max = jnp.broadcast_to(jnp.max(s, axis=-1, keepdims=True), (tq, LANES))
            m_new = blk_max if j == 0 else jnp.maximum(m_s[rows, :], blk_max)
            parts = [jnp.exp2(s[:, g * LANES:(g + 1) * LANES] - m_new) for g in range(groups)]
            l_new = parts[0]
            for p in parts[1:]:
                l_new = l_new + p
            pv = _dot(jnp.concatenate([p.astype(BF16) for p in parts], axis=1), v_j)
            if j > 0:
                alpha = jnp.exp2(m_s[rows, :] - m_new)
                l_new = alpha * l_s[rows, :] + l_new
                pv = alpha * acc_s[rows, :] + pv
            if i == j:
                l = jnp.sum(l_new, axis=-1, keepdims=True)
                o_ref[rows, :] = (pv / l).astype(o_ref.dtype)
            else:
                m_s[rows, :] = m_new
                l_s[rows, :] = l_new
                acc_s[rows, :] = pv


def _attn_prompt(q, k, v, c, batch, seq, n_heads, head_dim, tq, weights_to_cast):
    assert head_dim == LANES, "per-row softmax state is kept one vreg lane group wide"
    head_spec = pl.BlockSpec((seq, head_dim), lambda b, h: (b, h))
    n_steps = batch * n_heads
    cast_specs = []
    for w in weights_to_cast:
        assert w.shape[0] % (n_steps * 2 * SUBLANES) == 0
        cast_specs.append(pl.BlockSpec((w.shape[0] // n_steps, w.shape[1]),
                                       lambda b, h: (b * n_heads + h, 0)))
    attn, *casts = pl.pallas_call(
        functools.partial(_attn_prompt_kernel, tq=tq, scale=head_dim ** -0.5, n_heads=n_heads,
                          n_casts=len(weights_to_cast)),
        grid=(batch, n_heads),
        in_specs=[head_spec, head_spec, head_spec,
                  pl.BlockSpec((seq, LANES), lambda b, h: (b, 0))] + cast_specs,
        out_specs=[head_spec] + cast_specs,
        out_shape=[jax.ShapeDtypeStruct(q.shape, BF16)]
                  + [jax.ShapeDtypeStruct(w.shape, BF16) for w in weights_to_cast],
        scratch_shapes=[pltpu.VMEM((seq, head_dim + LANES), BF16),
                        pltpu.VMEM((seq, head_dim + LANES), BF16),
                        pltpu.VMEM((seq, LANES), F32),
                        pltpu.VMEM((seq, LANES), F32),
                        pltpu.VMEM((seq, head_dim), F32)],
        compiler_params=_params("parallel", "parallel"),
        name="attn_prompt",
    )(q, k, v, c, *weights_to_cast)
    return attn, casts


def _paged_forget_bias(lf_pages, lf_new, n_heads):
    rows, page = lf_pages.shape
    n_pages = rows // n_heads
    within = _prefix_sum(lf_pages, axis=1)
    totals = jnp.broadcast_to(within[:, page - 1:], (rows, page))
    upto = _prefix_sum(totals, axis=0, start=n_heads)
    c_past = within + (upto - totals)
    c_new = upto[rows - n_heads:, 0:1] + lf_new
    bias = jnp.concatenate([c_new] * n_pages, axis=0) - c_past
    src = lax.broadcasted_iota(jnp.int32, (page, page * n_heads), 0)
    dst = lax.broadcasted_iota(jnp.int32, (page, page * n_heads), 1)
    expand = jnp.where(dst // n_heads == src, 1.0, 0.0).astype(BF16)
    parts = jnp.concatenate(_split3_bf16(bias), axis=0).astype(BF16)
    wide = _dot(parts, expand)
    return wide[:rows] + wide[rows:2 * rows] + wide[2 * rows:]


def _paged_attn_step(step, q_ref, kn_ref, vn_ref, lfn_ref, k_refs, v_refs, load_lf_pages, o_ref,
                     s_s, bias_s, m_s, l_s, acc_s, *, steps_per_seq, scale):
    pps = len(k_refs)
    page, n_heads, head_dim = k_refs[0].shape[1:]
    pw = page * n_heads
    part = lax.rem(step, steps_per_seq)

    head_of_row = lax.broadcasted_iota(jnp.int32, (n_heads, pw), 0)
    lane = lax.broadcasted_iota(jnp.int32, (n_heads, pw), 1)
    on_diag = head_of_row == lane % n_heads
    q = q_ref[0]

    @pl.when(part == 0)
    def _():
        bias_s[...] = _paged_forget_bias(
            jnp.concatenate(load_lf_pages(), axis=0), lfn_ref[0], n_heads)
        kn = kn_ref[0].astype(BF16).astype(F32)
        m_s[...] = jnp.sum(q.astype(F32) * kn, axis=-1, keepdims=True) * scale
        l_s[...] = jnp.ones_like(l_s)
        acc_s[...] = vn_ref[0].astype(BF16).astype(F32)

    for p in range(pps):
        kf = k_refs[p][0].reshape(pw, head_dim).astype(BF16)
        first = pl.multiple_of((part * pps + p) * n_heads, n_heads)
        s = _dot_nt(q, kf) * scale + bias_s[pl.ds(first, n_heads), :]
        s_s[:, p * pw:(p + 1) * pw] = jnp.where(on_diag, s, NEG_BIG)

    s_all = s_s[...]
    m_old = m_s[...]
    m_new = jnp.maximum(m_old, jnp.max(s_all, axis=-1, keepdims=True))
    p_all = jnp.exp(s_all - m_new)
    alpha = jnp.exp(m_old - m_new)
    l_s[...] = alpha * l_s[...] + jnp.sum(p_all, axis=-1, keepdims=True)
    m_s[...] = m_new
    p16 = p_all.astype(BF16)
    acc = alpha * acc_s[...]
    for p in range(pps):
        vf = v_refs[p][0].reshape(pw, head_dim).astype(BF16)
        acc = acc + _dot(p16[:, p * pw:(p + 1) * pw], vf)
    acc_s[...] = acc

    @pl.when(part == steps_per_seq - 1)
    def _():
        o_ref[0] = (acc_s[...] / l_s[...]).astype(o_ref.dtype)


def _attn_sample_kernel(pt_ref, q_ref, kn_ref, vn_ref, lfn_ref, lf_all_ref, *refs,
                        pages_per_step, n_pages):
    k_refs = refs[:pages_per_step]
    v_refs = refs[pages_per_step:2 * pages_per_step]
    o_ref = refs[2 * pages_per_step]
    scratch = refs[2 * pages_per_step + 1:]
    b = pl.program_id(0)

    def load_lf_pages():
        return [lf_all_ref[pt_ref[b, p]] for p in range(n_pages)]

    _paged_attn_step(b * pl.num_programs(1) + pl.program_id(1),
                     q_ref, kn_ref, vn_ref, lfn_ref, k_refs, v_refs, load_lf_pages, o_ref,
                     *scratch,
                     steps_per_seq=n_pages // pages_per_step, scale=q_ref.shape[2] ** -0.5)


def _attn_sample(q, k_new, v_new, lf_new, cache_k, cache_v, cache_lf, page_table, pages_per_step):
    batch, n_pages = page_table.shape
    _, page, n_heads, head_dim = cache_k.shape
    pw = page * n_heads
    steps_per_seq = n_pages // pages_per_step
    assert page == LANES, "one page of log-forget values per head is one lane row"

    def row_spec(shape):
        return pl.BlockSpec((1,) + shape, lambda b, s, pt: (b, 0, 0))

    def page_spec(p):
        return pl.BlockSpec((1, page, n_heads, head_dim),
                            lambda b, s, pt: (pt[b, s * pages_per_step + p], 0, 0, 0))

    lf_spec = pl.BlockSpec(cache_lf.shape, lambda b, s, pt: (0, 0, 0),
                           pipeline_mode=pl.Buffered(1))

    return pl.pallas_call(
        functools.partial(_attn_sample_kernel, pages_per_step=pages_per_step, n_pages=n_pages),
        grid_spec=pltpu.PrefetchScalarGridSpec(
            num_scalar_prefetch=1,
            grid=(batch, steps_per_seq),
            in_specs=[row_spec((n_heads, head_dim))] * 3 + [row_spec((n_heads, 1)), lf_spec]
                     + [page_spec(p) for p in range(pages_per_step)] * 2,
            out_specs=row_spec((n_heads, head_dim)),
            scratch_shapes=[pltpu.VMEM((n_heads, pages_per_step * pw), F32),
                            pltpu.VMEM((n_pages * n_heads, pw), F32),
                            pltpu.VMEM((n_heads, 1), F32),
                            pltpu.VMEM((n_heads, 1), F32),
                            pltpu.VMEM((n_heads, head_dim), F32)],
        ),
        out_shape=jax.ShapeDtypeStruct((batch, n_heads, head_dim), BF16),
        compiler_params=_params("arbitrary", "arbitrary"),
        name="attn_sample",
    )(page_table, q, k_new, v_new, lf_new, cache_lf,
      *([cache_k] * pages_per_step), *([cache_v] * pages_per_step))


def _rglru_gates(conv, r_pre, i_pre, b_rg, b_ig, neg_c_softplus):
    r = jax.nn.sigmoid(r_pre + b_rg)
    i_g = jax.nn.sigmoid(i_pre + b_ig)
    log_a = r * neg_c_softplus
    a = jnp.exp(log_a)
    b = jnp.sqrt(jnp.tanh(log_a) * (-1.0 - a * a)) * (i_g * conv)
    return a, b


def _rglru_prompt_kernel(xr_ref, yr_ref, cw_ref, cb_ref, wrg_ref, wig_ref, brg_ref, big_ref,
                         lam_ref, o_ref, hl_ref, xc_s, conv_s, rp_s, ip_s, a_s, b_s, h_s):
    c = pl.program_id(0)
    nb, tc, n_blocks, bw = xr_ref.shape
    halo = CONV_WIDTH - 1
    rows = nb * tc

    @pl.when(c == 0)
    def _():
        xc_s[:, 0:halo] = jnp.zeros((nb, halo, n_blocks, bw), F32)
        h_s[...] = jnp.zeros_like(h_s)

    @pl.when(c > 0)
    def _():
        xc_s[:, 0:halo] = xc_s[:, tc:tc + halo]

    xc_s[:, halo:halo + tc] = xr_ref[...]

    conv = cb_ref[...] + xc_s[:, 0:tc] * cw_ref[0]
    for i in range(1, CONV_WIDTH):
        conv = conv + xc_s[:, i:i + tc] * cw_ref[i]

    conv_s[...] = conv.reshape(rows * n_blocks, bw)
    for n in range(n_blocks):
        xn = conv_s[pl.ds(n, rows, stride=n_blocks), :].astype(BF16)
        rp_s[pl.ds(n, rows, stride=n_blocks), :] = _dot(xn, wrg_ref[n])
        ip_s[pl.ds(n, rows, stride=n_blocks), :] = _dot(xn, wig_ref[n])

    neg_c_softplus = -LRU_C * _softplus(-lam_ref[...])
    a, b = _rglru_gates(conv,
                        rp_s[...].reshape(nb, tc, n_blocks, bw),
                        ip_s[...].reshape(nb, tc, n_blocks, bw),
                        brg_ref[...], big_ref[...], neg_c_softplus)
    a_s[...] = a
    b_s[...] = b

    def step(t, hs):
        new = tuple(a_s[j, t] * hs[j] + b_s[j, t] for j in range(nb))
        for j in range(nb):
            b_s[j, t] = new[j]
        return new

    hs = lax.fori_loop(0, tc, step, tuple(h_s[j] for j in range(nb)), unroll=8)
    for j in range(nb):
        h_s[j] = hs[j]

    conv_s[...] = (b_s[...] * _gelu_tanh(yr_ref[...])).reshape(rows * n_blocks, bw)
    for n in range(n_blocks):
        o_ref[:, :, n * bw:(n + 1) * bw] = (
            conv_s[pl.ds(n, rows, stride=n_blocks), :].reshape(nb, tc, bw).astype(o_ref.dtype))
    hl_ref[...] = h_s[...]


def _rglru_prompt(xr, yr, w, batch, seq, tc):
    n_blocks, bw = w["n_rnn_blocks"], w["rnn_block"]
    xr4 = xr.reshape(batch, seq, n_blocks, bw)
    yr4 = yr.reshape(batch, seq, n_blocks, bw)
    rows = batch * tc
    halo = CONV_WIDTH - 1

    def const(shape):
        return pl.BlockSpec(shape, lambda c: (0,) * len(shape))

    seq_spec = pl.BlockSpec((batch, tc, n_blocks, bw), lambda c: (0, c, 0, 0))
    out, h_last = pl.pallas_call(
        _rglru_prompt_kernel,
        grid=(seq // tc,),
        in_specs=[seq_spec, seq_spec,
                  const((CONV_WIDTH, n_blocks, bw)), const((n_blocks, bw)),
                  const((n_blocks, bw, bw)), const((n_blocks, bw, bw)),
                  const((n_blocks, bw)), const((n_blocks, bw)), const((n_blocks, bw))],
        out_specs=[pl.BlockSpec((batch, tc, n_blocks * bw), lambda c: (0, c, 0)),
                   const((batch, n_blocks, bw))],
        out_shape=[jax.ShapeDtypeStruct((batch, seq, n_blocks * bw), BF16),
                   jax.ShapeDtypeStruct((batch, n_blocks, bw), F32)],
        scratch_shapes=[pltpu.VMEM((batch, tc + halo, n_blocks, bw), F32),
                        pltpu.VMEM((rows * n_blocks, bw), F32),
                        pltpu.VMEM((rows * n_blocks, bw), F32),
                        pltpu.VMEM((rows * n_blocks, bw), F32),
                        pltpu.VMEM((batch, tc, n_blocks, bw), F32),
                        pltpu.VMEM((batch, tc, n_blocks, bw), F32),
                        pltpu.VMEM((batch, n_blocks, bw), F32)],
        compiler_params=_params("arbitrary"),
        name="rglru_prompt",
    )(xr4, yr4, w["conv_w3"], w["conv_b2"], w["w_rg"], w["w_ig"], w["b_rg2"], w["b_ig2"],
      w["lam2"])
    return out.reshape(batch * seq, n_blocks * bw), h_last.reshape(batch, n_blocks * bw)


def _rglru_sample_kernel(xr_ref, yr_ref, buf_ref, h0_ref, cw_ref, cb_ref, wrg_ref, wig_ref,
                         brg_ref, big_ref, lam_ref, o_ref, h_ref, nbuf_ref):
    halo = CONV_WIDTH - 1
    n_blocks, bw = wrg_ref.shape[0], wrg_ref.shape[1]
    xr = xr_ref[...]
    conv = cb_ref[...] + xr * cw_ref[halo:halo + 1, :]
    for i in range(halo):
        conv = conv + buf_ref[i] * cw_ref[i:i + 1, :]
    r_parts, i_parts = [], []
    for n in range(n_blocks):
        xn = conv[:, n * bw:(n + 1) * bw].astype(BF16)
        r_parts.append(_dot(xn, wrg_ref[n]))
        i_parts.append(_dot(xn, wig_ref[n]))
    neg_c_softplus = -LRU_C * _softplus(-lam_ref[...])
    a, b = _rglru_gates(conv, jnp.concatenate(r_parts, axis=1), jnp.concatenate(i_parts, axis=1),
                        brg_ref[...], big_ref[...], neg_c_softplus)
    h = a * h0_ref[...] + b
    h_ref[...] = h
    o_ref[...] = h * _gelu_tanh(yr_ref[...])
    for i in range(halo - 1):
        nbuf_ref[i] = buf_ref[i + 1]
    nbuf_ref[halo - 1] = xr


def _rglru_sample(xr, yr, conv_buf_t, h0, w):
    rows, width = xr.shape
    halo = CONV_WIDTH - 1
    return pl.pallas_call(
        _rglru_sample_kernel,
        out_shape=[jax.ShapeDtypeStruct((rows, width), F32),
                   jax.ShapeDtypeStruct((rows, width), F32),
                   jax.ShapeDtypeStruct((halo, rows, width), F32)],
        compiler_params=pltpu.CompilerParams(vmem_limit_bytes=VMEM_LIMIT_BYTES),
        name="rglru_sample",
    )(xr, yr, conv_buf_t, h0, w["conv_w"], w["conv_b"], w["w_rg"], w["w_ig"],
      w["b_rg"], w["b_ig"], w["lam"])


def _out_proj_kernel(attn_ref, rnn_ref, ga_ref, gr_ref, x_ref, woa_ref, wor_ref, wout_ref,
                     gpm_ref, gpf_ref, h_o, hn_o, *, n_parts):
    part_rows = x_ref.shape[0] // n_parts
    for part in range(n_parts):
        r = slice(part * part_rows, (part + 1) * part_rows)
        o = (jax.nn.sigmoid(ga_ref[r, :].astype(F32))
             * _dot(attn_ref[r, :].astype(BF16), woa_ref[...])
             + jax.nn.sigmoid(gr_ref[r, :].astype(F32))
             * _dot(rnn_ref[r, :].astype(BF16), wor_ref[...]))
        mix = _dot(o.astype(BF16), wout_ref[...])
        h = x_ref[r, :] + _rms_norm(mix, gpm_ref[...])
        h_o[r, :] = h
        hn_o[r, :] = _rms_norm(h, gpf_ref[...]).astype(hn_o.dtype)


def _out_proj(attn, rnn, ga, gr, x, w, tm, n_parts=1):
    rows, d_model = x.shape

    def row(width):
        return pl.BlockSpec((tm, width), lambda i: (i, 0))

    def const(shape):
        return pl.BlockSpec(shape, lambda i: (0, 0), pipeline_mode=pl.Buffered(1))

    return pl.pallas_call(
        functools.partial(_out_proj_kernel, n_parts=n_parts),
        grid=(rows // tm,),
        in_specs=[row(attn.shape[1]), row(rnn.shape[1]), row(d_model), row(d_model), row(d_model),
                  const(w["o_attn"].shape), const(w["o_rnn"].shape), const(w["out"].shape),
                  const((1, d_model)), const((1, d_model))],
        out_specs=[row(d_model), row(d_model)],
        out_shape=[jax.ShapeDtypeStruct((rows, d_model), F32),
                   jax.ShapeDtypeStruct((rows, d_model), BF16)],
        compiler_params=_params("parallel"),
        name="out_proj",
    )(attn, rnn, ga, gr, x, w["o_attn"], w["o_rnn"], w["out"], w["g_post_mix"], w["g_pre_ffn"])


def _ffn_kernel(hn_ref, h_ref, wg_ref, wu_ref, wd_ref, g_ref, y_o, acc_s):
    j = pl.program_id(1)

    @pl.when(j == 0)
    def _():
        acc_s[...] = jnp.zeros_like(acc_s)

    hn = hn_ref[...]
    gate = _dot(hn, wg_ref[...])
    up = _dot(hn, wu_ref[...])
    acc_s[...] += _dot((jax.nn.silu(gate) * up).astype(BF16), wd_ref[...])

    @pl.when(j == pl.num_programs(1) - 1)
    def _():
        y_o[...] = h_ref[...] + _rms_norm(acc_s[...], g_ref[...])


def _ffn(hn, h, w, tm, tf):
    rows, d_model = h.shape
    return pl.pallas_call(
        _ffn_kernel,
        grid=(rows // tm, w["gate"].shape[1] // tf),
        in_specs=[pl.BlockSpec((tm, d_model), lambda i, j: (i, 0)),
                  pl.BlockSpec((tm, d_model), lambda i, j: (i, 0)),
                  pl.BlockSpec((d_model, tf), lambda i, j: (0, j)),
                  pl.BlockSpec((d_model, tf), lambda i, j: (0, j)),
                  pl.BlockSpec((tf, d_model), lambda i, j: (j, 0)),
                  pl.BlockSpec((1, d_model), lambda i, j: (0, 0))],
        out_specs=pl.BlockSpec((tm, d_model), lambda i, j: (i, 0)),
        out_shape=jax.ShapeDtypeStruct((rows, d_model), F32),
        scratch_shapes=[pltpu.VMEM((tm, d_model), F32)],
        compiler_params=_params("parallel", "arbitrary"),
        name="ffn",
    )(hn, h, w["gate"], w["up"], w["down"], w["g_post_ffn"])


def _layer_weights(l, n_heads, head_dim, g_pre_mix, w_in, b_f, conv_w, conv_b, w_rg, b_rg, w_ig,
                   b_ig, lru_lambda, w_o_attn, w_o_rnn, w_out, g_post_mix, g_pre_ffn, w_gate,
                   w_up, w_down, g_post_ffn):
    d_model = w_in.shape[1]
    n_blocks, bw = w_rg.shape[1], w_rg.shape[2]
    d_rnn = n_blocks * bw
    aw = n_heads * head_dim
    wi_t = jnp.swapaxes(w_in[l], 0, 1)
    f_lo, f_hi = 3 * aw, 3 * aw + n_heads
    cols = {
        "qkv": _cast_rows(wi_t, 0, f_lo, 512),
        "rest": _cast_rows(wi_t, f_hi, wi_t.shape[0] - f_hi, 512),
        "f": jnp.pad(wi_t[f_lo:f_hi].astype(BF16), ((0, LANES - n_heads), (0, 0))),
        "b_f": jnp.pad(b_f[l].reshape(1, n_heads), ((0, 0), (0, LANES - n_heads))),
        "widths": {"q": aw, "k": aw, "v": aw, "xr": d_rnn, "yr": d_rnn,
                   "ga": d_model, "gr": d_model},
        "n_heads": n_heads,
        "head_dim": head_dim,
        "rnn_block": bw,
    }
    rnn = {
        "n_rnn_blocks": n_blocks, "rnn_block": bw,
        "conv_w": conv_w[l], "conv_b": conv_b[l].reshape(1, d_rnn),
        "b_rg": b_rg[l].reshape(1, d_rnn), "b_ig": b_ig[l].reshape(1, d_rnn),
        "lam": lru_lambda[l].reshape(1, d_rnn),
        "conv_w3": conv_w[l].reshape(CONV_WIDTH, n_blocks, bw),
        "conv_b2": conv_b[l].reshape(n_blocks, bw),
        "b_rg2": b_rg[l].reshape(n_blocks, bw), "b_ig2": b_ig[l].reshape(n_blocks, bw),
        "lam2": lru_lambda[l].reshape(n_blocks, bw),
        "w_rg": w_rg[l].astype(BF16), "w_ig": w_ig[l].astype(BF16),
    }
    mix = {"o_attn": w_o_attn[l], "o_rnn": w_o_rnn[l], "out": w_out[l],
           "g_post_mix": g_post_mix[l].reshape(1, d_model),
           "g_pre_ffn": g_pre_ffn[l].reshape(1, d_model)}
    ffn = {"gate": w_gate[l], "up": w_up[l], "down": w_down[l],
           "g_post_ffn": g_post_ffn[l].reshape(1, d_model)}
    return g_pre_mix[l].reshape(1, d_model), cols, rnn, mix, ffn


def kernel(x_prompt, x_sample, cache_k, cache_v, cache_logf, state_h, state_conv, page_table,
           g_pre_mix, w_in, b_f, conv_w, conv_b, w_rg, b_rg, w_ig, b_ig, lru_lambda,
           w_o_attn, w_o_rnn, w_out, g_post_mix, g_pre_ffn, w_gate, w_up, w_down, g_post_ffn):
    batch, seq, d_model = x_prompt.shape
    dec_batch, dec_seq, _ = x_sample.shape
    assert dec_seq == 1, "sample group carries one new token per sequence"
    depth, n_phys, page, n_heads, head_dim = cache_k.shape
    aw = n_heads * head_dim
    d_rnn = state_h.shape[2]
    n_pages = page_table.shape[1]
    assert LANES % n_heads == 0 and (page * n_heads) % LANES == 0

    xp = x_prompt.reshape(batch * seq, d_model)
    xs = x_sample.reshape(dec_batch, d_model)
    outs = {name: [] for name in ("kp", "vp", "lp", "hp", "cp", "ks", "vs", "ls", "hs", "cs")}

    for l in range(depth):
        g_in, w_cols, w_rnn, w_mix, w_ffn = _layer_weights(
            l, n_heads, head_dim, g_pre_mix, w_in, b_f, conv_w, conv_b, w_rg, b_rg, w_ig, b_ig,
            lru_lambda, w_o_attn, w_o_rnn, w_out, g_post_mix, g_pre_ffn, w_gate, w_up, w_down,
            g_post_ffn)

        q, k16, v16, k, v, xr, yr, ga, gr, lf, c = _in_proj(xp, g_in, w_cols, tm=512,
                                                            n_col_steps=4, seq=seq)
        qs, _, _, ks, vs, xrs, yrs, gas, grs, lfs = _in_proj(xs, g_in, w_cols, tm=dec_batch,
                                                             n_col_steps=2)
        ks = ks.reshape(dec_batch, n_heads, head_dim)
        vs = vs.reshape(dec_batch, n_heads, head_dim)

        attn, (woa, wor, wout, wg, wu, wd) = _attn_prompt(
            q, k16, v16, c, batch, seq, n_heads, head_dim, tq=256,
            weights_to_cast=[w_mix["o_attn"], w_mix["o_rnn"], w_mix["out"],
                             w_ffn["gate"], w_ffn["up"], w_ffn["down"]])
        w_mix = dict(w_mix, o_attn=woa, o_rnn=wor, out=wout)
        w_ffn = dict(w_ffn, gate=wg, up=wu, down=wd)
        rnn, h_last = _rglru_prompt(xr, yr, w_rnn, batch, seq, tc=256)
        h, hn = _out_proj(attn, rnn, ga, gr, xp, w_mix, tm=512, n_parts=2)
        xp_next = _ffn(hn, h, w_ffn, tm=512, tf=512)
        outs["kp"].append(k.reshape(batch, seq, n_heads, head_dim))
        outs["vp"].append(v.reshape(batch, seq, n_heads, head_dim))
        outs["lp"].append(lf.reshape(batch, seq, n_heads))
        outs["hp"].append(h_last)
        outs["cp"].append(xr.reshape(batch, seq, d_rnn // w_cols["rnn_block"], w_cols["rnn_block"])
                          [:, seq - (CONV_WIDTH - 1):].reshape(batch, CONV_WIDTH - 1, d_rnn))

        attn_s = _attn_sample(
            qs.reshape(dec_batch, n_heads, head_dim), ks, vs, lfs.reshape(dec_batch, n_heads, 1),
            cache_k[l], cache_v[l], jnp.swapaxes(cache_logf[l], 1, 2),
            page_table, pages_per_step=n_pages)
        rnn, h_new, buf_new = _rglru_sample(
            xrs.reshape(dec_batch, d_rnn), yrs.reshape(dec_batch, d_rnn),
            jnp.swapaxes(state_conv[l], 0, 1), state_h[l], w_rnn)
        h, hn = _out_proj(attn_s.reshape(dec_batch, aw), rnn, gas, grs, xs, w_mix, tm=dec_batch)
        xs_next = _ffn(hn, h, w_ffn, tm=dec_batch, tf=w_gate.shape[2] // 4)
        outs["ks"].append(ks.reshape(dec_batch, 1, n_heads, head_dim))
        outs["vs"].append(vs.reshape(dec_batch, 1, n_heads, head_dim))
        outs["ls"].append(lfs.reshape(dec_batch, 1, n_heads))
        outs["hs"].append(h_new)
        outs["cs"].append(jnp.swapaxes(buf_new, 0, 1))

        xp, xs = xp_next, xs_next

    st = {name: jnp.stack(vals) for name, vals in outs.items()}
    return (xp.reshape(batch, seq, d_model), xs.reshape(dec_batch, 1, d_model),
            st["kp"], st["vp"], st["lp"], st["hp"], st["cp"],
            st["ks"], st["vs"], st["ls"], st["hs"], st["cs"])
```

```python
import functools
import math

import jax
import jax.numpy as jnp
from jax import lax
from jax.experimental import pallas as pl
from jax.experimental.pallas import tpu as pltpu

F32 = jnp.float32
BF16 = jnp.bfloat16

EPS = 1e-6
LRU_C = 8.0
CONV_WIDTH = 4
NEG_BIG = -1e30

SUBLANES = 8
LANES = 128
VMEM_LIMIT_BYTES = 56 * 1024 * 1024


def _params(*semantics):
    return pltpu.CompilerParams(dimension_semantics=semantics,
                                vmem_limit_bytes=VMEM_LIMIT_BYTES)


def _rms_norm(x, g):
    inv = lax.rsqrt(jnp.mean(x * x, axis=-1, keepdims=True) + EPS)
    return (x * inv) * g


def _log_sigmoid(x):
    return jnp.minimum(x, 0.0) - jnp.log1p(jnp.exp(-jnp.abs(x)))


def _softplus(x):
    return jnp.maximum(x, 0.0) + jnp.log1p(jnp.exp(-jnp.abs(x)))


def _gelu_tanh(x):
    c = math.sqrt(2.0 / math.pi)
    half = 0.5 * x
    return half + half * jnp.tanh(x * (c + (c * 0.044715) * (x * x)))


def _dot(a, b):
    return jnp.dot(a, b, preferred_element_type=F32)


def _dot_nt(a, b):
    return lax.dot_general(a, b, (((1,), (1,)), ((), ())), preferred_element_type=F32)


def _cast_kernel(x_ref, o_ref):
    o_ref[...] = x_ref[...].astype(o_ref.dtype)


def _cast_shifted_kernel(a_ref, b_ref, o_ref, *, shift):
    o_ref[...] = jnp.concatenate([a_ref[shift:, :], b_ref[:shift, :]], axis=0).astype(o_ref.dtype)


def _cast_rows(w_t, first_row, n_rows, tile):
    width = w_t.shape[1]
    assert n_rows % tile == 0 and tile % SUBLANES == 0
    shift = first_row % tile
    base = first_row // tile
    out_spec = pl.BlockSpec((tile, width), lambda j: (j, 0))
    out_shape = jax.ShapeDtypeStruct((n_rows, width), BF16)
    if shift == 0:
        return pl.pallas_call(
            _cast_kernel, grid=(n_rows // tile,),
            in_specs=[pl.BlockSpec((tile, width), lambda j: (base + j, 0))],
            out_specs=out_spec, out_shape=out_shape,
            compiler_params=_params("parallel"), name="cast",
        )(w_t)
    assert shift == SUBLANES and first_row + n_rows <= w_t.shape[0]
    per_tile = tile // SUBLANES
    return pl.pallas_call(
        functools.partial(_cast_shifted_kernel, shift=shift), grid=(n_rows // tile,),
        in_specs=[pl.BlockSpec((tile, width), lambda j: (base + j, 0)),
                  pl.BlockSpec((SUBLANES, width), lambda j: ((base + j + 1) * per_tile, 0))],
        out_specs=out_spec, out_shape=out_shape,
        compiler_params=_params("parallel"), name="cast_shifted",
    )(w_t, w_t)


def _prefix_sum(x, axis, start=1):
    idx = lax.broadcasted_iota(jnp.int32, x.shape, axis)
    shift = start
    while shift < x.shape[axis]:
        x = x + jnp.where(idx >= shift, pltpu.roll(x, shift, axis=axis), 0.0)
        shift *= 2
    return x


def _in_proj_kernel(x_ref, g_ref, wq, wk, wv, wxr, wyr, wga, wgr, wf, bf_ref,
                    q_o, k16_o, v16_o, k32_o, v32_o, xr_o, yr_o, ga_o, gr_o, lf_o, *rest,
                    tiles_per_seq, inv_scale):
    i, j = pl.program_id(0), pl.program_id(1)
    xn_s = rest[-1]
    n_heads = lf_o.shape[1]

    @pl.when(j == 0)
    def _():
        xn = _rms_norm(x_ref[...], g_ref[...]).astype(BF16)
        xn_s[...] = xn
        lf = _log_sigmoid(_dot_nt(xn, wf[...]) + bf_ref[...])
        lf_o[...] = lf[:, :n_heads]
        if tiles_per_seq is not None:
            c_o, carry_s = rest[0], rest[1]

            @pl.when(lax.rem(i, tiles_per_seq) == 0)
            def _():
                carry_s[...] = jnp.zeros_like(carry_s)

            c = _prefix_sum(lf, axis=0) + carry_s[...]
            carry_s[...] = c[c.shape[0] - 1:, :]
            hi, mid, lo = _split3_bf16(c * inv_scale)
            lane = lax.broadcasted_iota(jnp.int32, c.shape, 1)
            packed = jnp.where(
                lane < n_heads, hi,
                jnp.where(lane < 2 * n_heads, pltpu.roll(mid, n_heads, axis=1),
                          jnp.where(lane < 3 * n_heads, pltpu.roll(lo, 2 * n_heads, axis=1),
                                    jnp.where(lane == 3 * n_heads, 1.0, 0.0))))
            c_o[...] = packed.astype(c_o.dtype)

    xn = xn_s[...]
    for w, o in ((wq, q_o), (wga, ga_o), (wgr, gr_o)):
        o[...] = _dot_nt(xn, w[...]).astype(o.dtype)

    tm = xn.shape[0]
    for w, o16, o32 in ((wk, k16_o, k32_o), (wv, v16_o, v32_o), (wxr, None, xr_o),
                        (wyr, None, yr_o)):
        res = _dot_nt(xn, w[...])
        if o16 is not None:
            o16[...] = res.astype(o16.dtype)
        width = o32.shape[1]
        n_blocks = o32.shape[0] // tm
        blocks_per_step = w.shape[0] // width
        for bb in range(blocks_per_step):
            block = j * blocks_per_step + bb
            o32[pl.ds(block, tm, stride=n_blocks), :] = res[:, bb * width:(bb + 1) * width]


def _in_proj(x, g, w, tm, n_col_steps, seq=None):
    rows, d_model = x.shape
    n_heads, head_dim = w["n_heads"], w["head_dim"]
    cols = {}
    for arr, names in (("qkv", ("q", "k", "v")), ("rest", ("xr", "yr", "ga", "gr"))):
        start = 0
        for name in names:
            cols[name] = (arr, start, w["widths"][name])
            start += w["widths"][name]
    order = ("q", "k", "v", "xr", "yr", "ga", "gr")
    tiles = {name: cols[name][2] // n_col_steps for name in order}
    grid = (rows // tm, n_col_steps)

    def w_spec(name):
        first_block = cols[name][1] // tiles[name]
        return pl.BlockSpec((tiles[name], d_model), lambda i, j: (first_block + j, 0))

    def o_spec(name):
        return pl.BlockSpec((tm, tiles[name]), lambda i, j: (i, j))

    def o_shape(name, dtype):
        return jax.ShapeDtypeStruct((rows, cols[name][2]), dtype)

    def per_head():
        return pl.BlockSpec((tm, n_heads), lambda i, j: (i, 0))

    def flat(width, block_width):
        n_blocks = width // block_width
        return (pl.BlockSpec((tm * n_blocks, block_width), lambda i, j: (i, 0)),
                jax.ShapeDtypeStruct((rows * n_blocks, block_width), F32))

    kv_spec, kv_shape = flat(cols["k"][2], head_dim)
    rnn_spec, rnn_shape = flat(cols["xr"][2], w["rnn_block"])
    out_specs = [o_spec("q"), o_spec("k"), o_spec("v"), kv_spec, kv_spec,
                 rnn_spec, rnn_spec, o_spec("ga"), o_spec("gr"), per_head()]
    out_shape = [o_shape("q", BF16), o_shape("k", BF16), o_shape("v", BF16),
                 kv_shape, kv_shape,
                 rnn_shape, rnn_shape, o_shape("ga", BF16), o_shape("gr", BF16),
                 jax.ShapeDtypeStruct((rows, n_heads), F32)]
    scratch = [pltpu.VMEM((tm, d_model), BF16)]
    if seq is not None:
        out_specs.append(pl.BlockSpec((tm, LANES), lambda i, j: (i, 0)))
        out_shape.append(jax.ShapeDtypeStruct((rows, LANES), BF16))
        scratch.insert(0, pltpu.VMEM((1, LANES), F32))
    return pl.pallas_call(
        functools.partial(_in_proj_kernel,
                          tiles_per_seq=None if seq is None else seq // tm,
                          inv_scale=head_dim ** 0.5),
        grid=grid,
        in_specs=[pl.BlockSpec((tm, d_model), lambda i, j: (i, 0)),
                  pl.BlockSpec((1, d_model), lambda i, j: (0, 0))]
                 + [w_spec(name) for name in order]
                 + [pl.BlockSpec((LANES, d_model), lambda i, j: (0, 0)),
                    pl.BlockSpec((1, LANES), lambda i, j: (0, 0))],
        out_specs=out_specs,
        out_shape=out_shape,
        scratch_shapes=scratch,
        compiler_params=_params("arbitrary", "arbitrary"),
        name="in_proj",
    )(x, g, *[w[cols[name][0]] for name in order], w["f"], w["b_f"])


def _split3_bf16(x):
    hi = x.astype(BF16)
    r = x - hi.astype(F32)
    mid = r.astype(BF16)
    lo = (r - mid.astype(F32)).astype(BF16)
    return hi.astype(F32), mid.astype(F32), lo.astype(F32)


def _attn_prompt_kernel(q_ref, k_ref, v_ref, c_ref, *rest, tq, scale, n_heads, n_casts):
    o_ref = rest[n_casts]
    for w_ref, w16_ref in zip(rest[:n_casts], rest[n_casts + 1:2 * n_casts + 1]):
        w16_ref[...] = w_ref[...].astype(w16_ref.dtype)
    qa_s, ka_s, m_s, l_s, acc_s = rest[2 * n_casts + 1:]
    seq, head_dim = q_ref.shape
    h = pl.program_id(1)
    src = lax.broadcasted_iota(jnp.int32, (LANES, LANES), 0)
    dst = lax.broadcasted_iota(jnp.int32, (LANES, LANES), 1)
    one_src = src == 3 * n_heads
    part_of_dst = jnp.where(dst < 3, dst, dst - 3) * n_heads + h
    sel_q = jnp.where((dst < 3) & (src == part_of_dst), 1.0,
                      jnp.where((dst >= 3) & (dst < 6) & one_src, 1.0, 0.0))
    sel_k = jnp.where((dst >= 3) & (dst < 6) & (src == part_of_dst), -1.0,
                      jnp.where((dst < 3) & one_src, 1.0, 0.0))
    c_rows = c_ref[...]
    qa_s[:, :head_dim] = q_ref[...]
    qa_s[:, head_dim:] = _dot(c_rows, sel_q.astype(BF16)).astype(BF16)
    ka_s[:, :head_dim] = k_ref[...]
    ka_s[:, head_dim:] = _dot(c_rows, sel_k.astype(BF16)).astype(BF16)

    sigma = scale * math.log2(math.e)
    row = lax.broadcasted_iota(jnp.int32, (tq, tq), 0)
    col = lax.broadcasted_iota(jnp.int32, (tq, tq), 1)
    visible = col <= row
    groups = tq // LANES
    n_tiles = seq // tq
    for j in range(n_tiles):
        s_all = _dot_nt(qa_s[j * tq:, :], ka_s[j * tq:(j + 1) * tq, :]) * sigma
        v_j = v_ref[j * tq:(j + 1) * tq, :]
        for i in range(j, n_tiles):
            rows = slice(i * tq, (i + 1) * tq)
            s = s_all[(i - j) * tq:(i - j + 1) * tq, :]
            if i == j:
                s = jnp.where(visible, s, NEG_BIG)
            blk_max = jnp.broadcast_to(jnp.max(s, axis=-1, keepdims=True), (tq, LANES))
            m_new = blk_max if j == 0 else jnp.maximum(m_s[rows, :], blk_max)
            parts = [jnp.exp2(s[:, g * LANES:(g + 1) * LANES] - m_new) for g in range(groups)]
            l_new = parts[0]
            for p in parts[1:]:
                l_new = l_new + p
            pv = _dot(jnp.concatenate([p.astype(BF16) for p in parts], axis=1), v_j)
            if j > 0:
                alpha = jnp.exp2(m_s[rows, :] - m_new)
                l_new = alpha * l_s[rows, :] + l_new
                pv = alpha * acc_s[rows, :] + pv
            if i == j:
                l = jnp.sum(l_new, axis=-1, keepdims=True)
                o_ref[rows, :] = (pv / l).astype(o_ref.dtype)
            else:
                m_s[rows, :] = m_new
                l_s[rows, :] = l_new
                acc_s[rows, :] = pv


def _attn_prompt(q, k, v, c, batch, seq, n_heads, head_dim, tq, weights_to_cast):
    assert head_dim == LANES, "per-row softmax state is kept one vreg lane group wide"
    head_spec = pl.BlockSpec((seq, head_dim), lambda b, h: (b, h))
    n_steps = batch * n_heads
    cast_specs = []
    for w in weights_to_cast:
        assert w.shape[0] % (n_steps * 2 * SUBLANES) == 0
        cast_specs.append(pl.BlockSpec((w.shape[0] // n_steps, w.shape[1]),
                                       lambda b, h: (b * n_heads + h, 0)))
    attn, *casts = pl.pallas_call(
        functools.partial(_attn_prompt_kernel, tq=tq, scale=head_dim ** -0.5, n_heads=n_heads,
                          n_casts=len(weights_to_cast)),
        grid=(batch, n_heads),
        in_specs=[head_spec, head_spec, head_spec,
                  pl.BlockSpec((seq, LANES), lambda b, h: (b, 0))] + cast_specs,
        out_specs=[head_spec] + cast_specs,
        out_shape=[jax.ShapeDtypeStruct(q.shape, BF16)]
                  + [jax.ShapeDtypeStruct(w.shape, BF16) for w in weights_to_cast],
        scratch_shapes=[pltpu.VMEM((seq, head_dim + LANES), BF16),
                        pltpu.VMEM((seq, head_dim + LANES), BF16),
                        pltpu.VMEM((seq, LANES), F32),
                        pltpu.VMEM((seq, LANES), F32),
                        pltpu.VMEM((seq, head_dim), F32)],
        compiler_params=_params("parallel", "parallel"),
        name="attn_prompt",
    )(q, k, v, c, *weights_to_cast)
    return attn, casts


def _paged_forget_bias(lf_pages, lf_new, n_heads):
    rows, page = lf_pages.shape
    n_pages = rows // n_heads
    within = _prefix_sum(lf_pages, axis=1)
    totals = jnp.broadcast_to(within[:, page - 1:], (rows, page))
    upto = _prefix_sum(totals, axis=0, start=n_heads)
    c_past = within + (upto - totals)
    c_new = upto[rows - n_heads:, 0:1] + lf_new
    bias = jnp.concatenate([c_new] * n_pages, axis=0) - c_past
    src = lax.broadcasted_iota(jnp.int32, (page, page * n_heads), 0)
    dst = lax.broadcasted_iota(jnp.int32, (page, page * n_heads), 1)
    expand = jnp.where(dst // n_heads == src, 1.0, 0.0).astype(BF16)
    parts = jnp.concatenate(_split3_bf16(bias), axis=0).astype(BF16)
    wide = _dot(parts, expand)
    return wide[:rows] + wide[rows:2 * rows] + wide[2 * rows:]


def _paged_attn_step(step, q_ref, kn_ref, vn_ref, lfn_ref, k_refs, v_refs, load_lf_pages, o_ref,
                     s_s, bias_s, m_s, l_s, acc_s, *, steps_per_seq, scale):
    pps = len(k_refs)
    page, n_heads, head_dim = k_refs[0].shape[1:]
    pw = page * n_heads
    part = lax.rem(step, steps_per_seq)

    head_of_row = lax.broadcasted_iota(jnp.int32, (n_heads, pw), 0)
    lane = lax.broadcasted_iota(jnp.int32, (n_heads, pw), 1)
    on_diag = head_of_row == lane % n_heads
    q = q_ref[0]

    @pl.when(part == 0)
    def _():
        bias_s[...] = _paged_forget_bias(
            jnp.concatenate(load_lf_pages(), axis=0), lfn_ref[0], n_heads)
        kn = kn_ref[0].astype(BF16).astype(F32)
        m_s[...] = jnp.sum(q.astype(F32) * kn, axis=-1, keepdims=True) * scale
        l_s[...] = jnp.ones_like(l_s)
        acc_s[...] = vn_ref[0].astype(BF16).astype(F32)

    for p in range(pps):
        kf = k_refs[p][0].reshape(pw, head_dim).astype(BF16)
        first = pl.multiple_of((part * pps + p) * n_heads, n_heads)
        s = _dot_nt(q, kf) * scale + bias_s[pl.ds(first, n_heads), :]
        s_s[:, p * pw:(p + 1) * pw] = jnp.where(on_diag, s, NEG_BIG)

    s_all = s_s[...]
    m_old = m_s[...]
    m_new = jnp.maximum(m_old, jnp.max(s_all, axis=-1, keepdims=True))
    p_all = jnp.exp(s_all - m_new)
    alpha = jnp.exp(m_old - m_new)
    l_s[...] = alpha * l_s[...] + jnp.sum(p_all, axis=-1, keepdims=True)
    m_s[...] = m_new
    p16 = p_all.astype(BF16)
    acc = alpha * acc_s[...]
    for p in range(pps):
        vf = v_refs[p][0].reshape(pw, head_dim).astype(BF16)
        acc = acc + _dot(p16[:, p * pw:(p + 1) * pw], vf)
    acc_s[...] = acc

    @pl.when(part == steps_per_seq - 1)
    def _():
        o_ref[0] = (acc_s[...] / l_s[...]).astype(o_ref.dtype)


def _attn_sample_kernel(pt_ref, q_ref, kn_ref, vn_ref, lfn_ref, lf_all_ref, *refs,
                        pages_per_step, n_pages):
    k_refs = refs[:pages_per_step]
    v_refs = refs[pages_per_step:2 * pages_per_step]
    o_ref = refs[2 * pages_per_step]
    scratch = refs[2 * pages_per_step + 1:]
    b = pl.program_id(0)

    def load_lf_pages():
        return [lf_all_ref[pt_ref[b, p]] for p in range(n_pages)]

    _paged_attn_step(b * pl.num_programs(1) + pl.program_id(1),
                     q_ref, kn_ref, vn_ref, lfn_ref, k_refs, v_refs, load_lf_pages, o_ref,
                     *scratch,
                     steps_per_seq=n_pages // pages_per_step, scale=q_ref.shape[2] ** -0.5)


def _attn_sample(q, k_new, v_new, lf_new, cache_k, cache_v, cache_lf, page_table, pages_per_step):
    batch, n_pages = page_table.shape
    _, page, n_heads, head_dim = cache_k.shape
    pw = page * n_heads
    steps_per_seq = n_pages // pages_per_step
    assert page == LANES, "one page of log-forget values per head is one lane row"

    def row_spec(shape):
        return pl.BlockSpec((1,) + shape, lambda b, s, pt: (b, 0, 0))

    def page_spec(p):
        return pl.BlockSpec((1, page, n_heads, head_dim),
                            lambda b, s, pt: (pt[b, s * pages_per_step + p], 0, 0, 0))

    lf_spec = pl.BlockSpec(cache_lf.shape, lambda b, s, pt: (0, 0, 0),
                           pipeline_mode=pl.Buffered(1))

    return pl.pallas_call(
        functools.partial(_attn_sample_kernel, pages_per_step=pages_per_step, n_pages=n_pages),
        grid_spec=pltpu.PrefetchScalarGridSpec(
            num_scalar_prefetch=1,
            grid=(batch, steps_per_seq),
            in_specs=[row_spec((n_heads, head_dim))] * 3 + [row_spec((n_heads, 1)), lf_spec]
                     + [page_spec(p) for p in range(pages_per_step)] * 2,
            out_specs=row_spec((n_heads, head_dim)),
            scratch_shapes=[pltpu.VMEM((n_heads, pages_per_step * pw), F32),
                            pltpu.VMEM((n_pages * n_heads, pw), F32),
                            pltpu.VMEM((n_heads, 1), F32),
                            pltpu.VMEM((n_heads, 1), F32),
                            pltpu.VMEM((n_heads, head_dim), F32)],
        ),
        out_shape=jax.ShapeDtypeStruct((batch, n_heads, head_dim), BF16),
        compiler_params=_params("arbitrary", "arbitrary"),
        name="attn_sample",
    )(page_table, q, k_new, v_new, lf_new, cache_lf,
      *([cache_k] * pages_per_step), *([cache_v] * pages_per_step))


def _rglru_gates(conv, r_pre, i_pre, b_rg, b_ig, neg_c_softplus):
    r = jax.nn.sigmoid(r_pre + b_rg)
    i_g = jax.nn.sigmoid(i_pre + b_ig)
    log_a = r * neg_c_softplus
    a = jnp.exp(log_a)
    b = jnp.sqrt(jnp.tanh(log_a) * (-1.0 - a * a)) * (i_g * conv)
    return a, b


def _rglru_prompt_kernel(xr_ref, yr_ref, cw_ref, cb_ref, wrg_ref, wig_ref, brg_ref, big_ref,
                         lam_ref, o_ref, hl_ref, xc_s, conv_s, rp_s, ip_s, a_s, b_s, h_s):
    c = pl.program_id(0)
    nb, tc, n_blocks, bw = xr_ref.shape
    halo = CONV_WIDTH - 1
    rows = nb * tc

    @pl.when(c == 0)
    def _():
        xc_s[:, 0:halo] = jnp.zeros((nb, halo, n_blocks, bw), F32)
        h_s[...] = jnp.zeros_like(h_s)

    @pl.when(c > 0)
    def _():
        xc_s[:, 0:halo] = xc_s[:, tc:tc + halo]

    xc_s[:, halo:halo + tc] = xr_ref[...]

    conv = cb_ref[...] + xc_s[:, 0:tc] * cw_ref[0]
    for i in range(1, CONV_WIDTH):
        conv = conv + xc_s[:, i:i + tc] * cw_ref[i]

    conv_s[...] = conv.reshape(rows * n_blocks, bw)
    for n in range(n_blocks):
        xn = conv_s[pl.ds(n, rows, stride=n_blocks), :].astype(BF16)
        rp_s[pl.ds(n, rows, stride=n_blocks), :] = _dot(xn, wrg_ref[n])
        ip_s[pl.ds(n, rows, stride=n_blocks), :] = _dot(xn, wig_ref[n])

    neg_c_softplus = -LRU_C * _softplus(-lam_ref[...])
    a, b = _rglru_gates(conv,
                        rp_s[...].reshape(nb, tc, n_blocks, bw),
                        ip_s[...].reshape(nb, tc, n_blocks, bw),
                        brg_ref[...], big_ref[...], neg_c_softplus)
    a_s[...] = a
    b_s[...] = b

    def step(t, hs):
        new = tuple(a_s[j, t] * hs[j] + b_s[j, t] for j in range(nb))
        for j in range(nb):
            b_s[j, t] = new[j]
        return new

    hs = lax.fori_loop(0, tc, step, tuple(h_s[j] for j in range(nb)), unroll=8)
    for j in range(nb):
        h_s[j] = hs[j]

    conv_s[...] = (b_s[...] * _gelu_tanh(yr_ref[...])).reshape(rows * n_blocks, bw)
    for n in range(n_blocks):
        o_ref[:, :, n * bw:(n + 1) * bw] = (
            conv_s[pl.ds(n, rows, stride=n_blocks), :].reshape(nb, tc, bw).astype(o_ref.dtype))
    hl_ref[...] = h_s[...]


def _rglru_prompt(xr, yr, w, batch, seq, tc):
    n_blocks, bw = w["n_rnn_blocks"], w["rnn_block"]
    xr4 = xr.reshape(batch, seq, n_blocks, bw)
    yr4 = yr.reshape(batch, seq, n_blocks, bw)
    rows = batch * tc
    halo = CONV_WIDTH - 1

    def const(shape):
        return pl.BlockSpec(shape, lambda c: (0,) * len(shape))

    seq_spec = pl.BlockSpec((batch, tc, n_blocks, bw), lambda c: (0, c, 0, 0))
    out, h_last = pl.pallas_call(
        _rglru_prompt_kernel,
        grid=(seq // tc,),
        in_specs=[seq_spec, seq_spec,
                  const((CONV_WIDTH, n_blocks, bw)), const((n_blocks, bw)),
                  const((n_blocks, bw, bw)), const((n_blocks, bw, bw)),
                  const((n_blocks, bw)), const((n_blocks, bw)), const((n_blocks, bw))],
        out_specs=[pl.BlockSpec((batch, tc, n_blocks * bw), lambda c: (0, c, 0)),
                   const((batch, n_blocks, bw))],
        out_shape=[jax.ShapeDtypeStruct((batch, seq, n_blocks * bw), BF16),
                   jax.ShapeDtypeStruct((batch, n_blocks, bw), F32)],
        scratch_shapes=[pltpu.VMEM((batch, tc + halo, n_blocks, bw), F32),
                        pltpu.VMEM((rows * n_blocks, bw), F32),
                        pltpu.VMEM((rows * n_blocks, bw), F32),
                        pltpu.VMEM((rows * n_blocks, bw), F32),
                        pltpu.VMEM((batch, tc, n_blocks, bw), F32),
                        pltpu.VMEM((batch, tc, n_blocks, bw), F32),
                        pltpu.VMEM((batch, n_blocks, bw), F32)],
        compiler_params=_params("arbitrary"),
        name="rglru_prompt",
    )(xr4, yr4, w["conv_w3"], w["conv_b2"], w["w_rg"], w["w_ig"], w["b_rg2"], w["b_ig2"],
      w["lam2"])
    return out.reshape(batch * seq, n_blocks * bw), h_last.reshape(batch, n_blocks * bw)


def _rglru_sample_kernel(xr_ref, yr_ref, buf_ref, h0_ref, cw_ref, cb_ref, wrg_ref, wig_ref,
                         brg_ref, big_ref, lam_ref, o_ref, h_ref, nbuf_ref):
    halo = CONV_WIDTH - 1
    n_blocks, bw = wrg_ref.shape[0], wrg_ref.shape[1]
    xr = xr_ref[...]
    conv = cb_ref[...] + xr * cw_ref[halo:halo + 1, :]
    for i in range(halo):
        conv = conv + buf_ref[i] * cw_ref[i:i + 1, :]
    r_parts, i_parts = [], []
    for n in range(n_blocks):
        xn = conv[:, n * bw:(n + 1) * bw].astype(BF16)
        r_parts.append(_dot(xn, wrg_ref[n]))
        i_parts.append(_dot(xn, wig_ref[n]))
    neg_c_softplus = -LRU_C * _softplus(-lam_ref[...])
    a, b = _rglru_gates(conv, jnp.concatenate(r_parts, axis=1), jnp.concatenate(i_parts, axis=1),
                        brg_ref[...], big_ref[...], neg_c_softplus)
    h = a * h0_ref[...] + b
    h_ref[...] = h
    o_ref[...] = h * _gelu_tanh(yr_ref[...])
    for i in range(halo - 1):
        nbuf_ref[i] = buf_ref[i + 1]
    nbuf_ref[halo - 1] = xr


def _rglru_sample(xr, yr, conv_buf_t, h0, w):
    rows, width = xr.shape
    halo = CONV_WIDTH - 1
    return pl.pallas_call(
        _rglru_sample_kernel,
        out_shape=[jax.ShapeDtypeStruct((rows, width), F32),
                   jax.ShapeDtypeStruct((rows, width), F32),
                   jax.ShapeDtypeStruct((halo, rows, width), F32)],
        compiler_params=pltpu.CompilerParams(vmem_limit_bytes=VMEM_LIMIT_BYTES),
        name="rglru_sample",
    )(xr, yr, conv_buf_t, h0, w["conv_w"], w["conv_b"], w["w_rg"], w["w_ig"],
      w["b_rg"], w["b_ig"], w["lam"])


def _out_proj_kernel(attn_ref, rnn_ref, ga_ref, gr_ref, x_ref, woa_ref, wor_ref, wout_ref,
                     gpm_ref, gpf_ref, h_o, hn_o, *, n_parts):
    part_rows = x_ref.shape[0] // n_parts
    for part in range(n_parts):
        r = slice(part * part_rows, (part + 1) * part_rows)
        o = (jax.nn.sigmoid(ga_ref[r, :].astype(F32))
             * _dot(attn_ref[r, :].astype(BF16), woa_ref[...])
             + jax.nn.sigmoid(gr_ref[r, :].astype(F32))
             * _dot(rnn_ref[r, :].astype(BF16), wor_ref[...]))
        mix = _dot(o.astype(BF16), wout_ref[...])
        h = x_ref[r, :] + _rms_norm(mix, gpm_ref[...])
        h_o[r, :] = h
        hn_o[r, :] = _rms_norm(h, gpf_ref[...]).astype(hn_o.dtype)


def _out_proj(attn, rnn, ga, gr, x, w, tm, n_parts=1):
    rows, d_model = x.shape

    def row(width):
        return pl.BlockSpec((tm, width), lambda i: (i, 0))

    def const(shape):
        return pl.BlockSpec(shape, lambda i: (0, 0), pipeline_mode=pl.Buffered(1))

    return pl.pallas_call(
        functools.partial(_out_proj_kernel, n_parts=n_parts),
        grid=(rows // tm,),
        in_specs=[row(attn.shape[1]), row(rnn.shape[1]), row(d_model), row(d_model), row(d_model),
                  const(w["o_attn"].shape), const(w["o_rnn"].shape), const(w["out"].shape),
                  const((1, d_model)), const((1, d_model))],
        out_specs=[row(d_model), row(d_model)],
        out_shape=[jax.ShapeDtypeStruct((rows, d_model), F32),
                   jax.ShapeDtypeStruct((rows, d_model), BF16)],
        compiler_params=_params("parallel"),
        name="out_proj",
    )(attn, rnn, ga, gr, x, w["o_attn"], w["o_rnn"], w["out"], w["g_post_mix"], w["g_pre_ffn"])


def _ffn_kernel(hn_ref, h_ref, wg_ref, wu_ref, wd_ref, g_ref, y_o, acc_s):
    j = pl.program_id(1)

    @pl.when(j == 0)
    def _():
        acc_s[...] = jnp.zeros_like(acc_s)

    hn = hn_ref[...]
    gate = _dot(hn, wg_ref[...])
    up = _dot(hn, wu_ref[...])
    acc_s[...] += _dot((jax.nn.silu(gate) * up).astype(BF16), wd_ref[...])

    @pl.when(j == pl.num_programs(1) - 1)
    def _():
        y_o[...] = h_ref[...] + _rms_norm(acc_s[...], g_ref[...])


def _ffn(hn, h, w, tm, tf):
    rows, d_model = h.shape
    return pl.pallas_call(
        _ffn_kernel,
        grid=(rows // tm, w["gate"].shape[1] // tf),
        in_specs=[pl.BlockSpec((tm, d_model), lambda i, j: (i, 0)),
                  pl.BlockSpec((tm, d_model), lambda i, j: (i, 0)),
                  pl.BlockSpec((d_model, tf), lambda i, j: (0, j)),
                  pl.BlockSpec((d_model, tf), lambda i, j: (0, j)),
                  pl.BlockSpec((tf, d_model), lambda i, j: (j, 0)),
                  pl.BlockSpec((1, d_model), lambda i, j: (0, 0))],
        out_specs=pl.BlockSpec((tm, d_model), lambda i, j: (i, 0)),
        out_shape=jax.ShapeDtypeStruct((rows, d_model), F32),
        scratch_shapes=[pltpu.VMEM((tm, d_model), F32)],
        compiler_params=_params("parallel", "arbitrary"),
        name="ffn",
    )(hn, h, w["gate"], w["up"], w["down"], w["g_post_ffn"])


def _layer_weights(l, n_heads, head_dim, g_pre_mix, w_in, b_f, conv_w, conv_b, w_rg, b_rg, w_ig,
                   b_ig, lru_lambda, w_o_attn, w_o_rnn, w_out, g_post_mix, g_pre_ffn, w_gate,
                   w_up, w_down, g_post_ffn):
    d_model = w_in.shape[1]
    n_blocks, bw = w_rg.shape[1], w_rg.shape[2]
    d_rnn = n_blocks * bw
    aw = n_heads * head_dim
    wi_t = jnp.swapaxes(w_in[l], 0, 1)
    f_lo, f_hi = 3 * aw, 3 * aw + n_heads
    cols = {
        "qkv": _cast_rows(wi_t, 0, f_lo, 512),
        "rest": _cast_rows(wi_t, f_hi, wi_t.shape[0] - f_hi, 512),
        "f": jnp.pad(wi_t[f_lo:f_hi].astype(BF16), ((0, LANES - n_heads), (0, 0))),
        "b_f": jnp.pad(b_f[l].reshape(1, n_heads), ((0, 0), (0, LANES - n_heads))),
        "widths": {"q": aw, "k": aw, "v": aw, "xr": d_rnn, "yr": d_rnn,
                   "ga": d_model, "gr": d_model},
        "n_heads": n_heads,
        "head_dim": head_dim,
        "rnn_block": bw,
    }
    rnn = {
        "n_rnn_blocks": n_blocks, "rnn_block": bw,
        "conv_w": conv_w[l], "conv_b": conv_b[l].reshape(1, d_rnn),
        "b_rg": b_rg[l].reshape(1, d_rnn), "b_ig": b_ig[l].reshape(1, d_rnn),
        "lam": lru_lambda[l].reshape(1, d_rnn),
        "conv_w3": conv_w[l].reshape(CONV_WIDTH, n_blocks, bw),
        "conv_b2": conv_b[l].reshape(n_blocks, bw),
        "b_rg2": b_rg[l].reshape(n_blocks, bw), "b_ig2": b_ig[l].reshape(n_blocks, bw),
        "lam2": lru_lambda[l].reshape(n_blocks, bw),
        "w_rg": w_rg[l].astype(BF16), "w_ig": w_ig[l].astype(BF16),
    }
    mix = {"o_attn": w_o_attn[l], "o_rnn": w_o_rnn[l], "out": w_out[l],
           "g_post_mix": g_post_mix[l].reshape(1, d_model),
           "g_pre_ffn": g_pre_ffn[l].reshape(1, d_model)}
    ffn = {"gate": w_gate[l], "up": w_up[l], "down": w_down[l],
           "g_post_ffn": g_post_ffn[l].reshape(1, d_model)}
    return g_pre_mix[l].reshape(1, d_model), cols, rnn, mix, ffn


def kernel(x_prompt, x_sample, cache_k, cache_v, cache_logf, state_h, state_conv, page_table,
           g_pre_mix, w_in, b_f, conv_w, conv_b, w_rg, b_rg, w_ig, b_ig, lru_lambda,
           w_o_attn, w_o_rnn, w_out, g_post_mix, g_pre_ffn, w_gate, w_up, w_down, g_post_ffn):
    batch, seq, d_model = x_prompt.shape
    dec_batch, dec_seq, _ = x_sample.shape
    assert dec_seq == 1, "sample group carries one new token per sequence"
    depth, n_phys, page, n_heads, head_dim = cache_k.shape
    aw = n_heads * head_dim
    d_rnn = state_h.shape[2]
    n_pages = page_table.shape[1]
    assert LANES % n_heads == 0 and (page * n_heads) % LANES == 0

    xp = x_prompt.reshape(batch * seq, d_model)
    xs = x_sample.reshape(dec_batch, d_model)
    outs = {name: [] for name in ("kp", "vp", "lp", "hp", "cp", "ks", "vs", "ls", "hs", "cs")}

    for l in range(depth):
        g_in, w_cols, w_rnn, w_mix, w_ffn = _layer_weights(
            l, n_heads, head_dim, g_pre_mix, w_in, b_f, conv_w, conv_b, w_rg, b_rg, w_ig, b_ig,
            lru_lambda, w_o_attn, w_o_rnn, w_out, g_post_mix, g_pre_ffn, w_gate, w_up, w_down,
            g_post_ffn)

        q, k16, v16, k, v, xr, yr, ga, gr, lf, c = _in_proj(xp, g_in, w_cols, tm=512,
                                                            n_col_steps=4, seq=seq)
        qs, _, _, ks, vs, xrs, yrs, gas, grs, lfs = _in_proj(xs, g_in, w_cols, tm=dec_batch,
                                                             n_col_steps=4)
        ks = ks.reshape(dec_batch, n_heads, head_dim)
        vs = vs.reshape(dec_batch, n_heads, head_dim)

        attn, (woa, wor, wout, wg, wu, wd) = _attn_prompt(
            q, k16, v16, c, batch, seq, n_heads, head_dim, tq=256,
            weights_to_cast=[w_mix["o_attn"], w_mix["o_rnn"], w_mix["out"],
                             w_ffn["gate"], w_ffn["up"], w_ffn["down"]])
        w_mix = dict(w_mix, o_attn=woa, o_rnn=wor, out=wout)
        w_ffn = dict(w_ffn, gate=wg, up=wu, down=wd)
        rnn, h_last = _rglru_prompt(xr, yr, w_rnn, batch, seq, tc=256)
        h, hn = _out_proj(attn, rnn, ga, gr, xp, w_mix, tm=512, n_parts=2)
        xp_next = _ffn(hn, h, w_ffn, tm=512, tf=512)
        outs["kp"].append(k.reshape(batch, seq, n_heads, head_dim))
        outs["vp"].append(v.reshape(batch, seq, n_heads, head_dim))
        outs["lp"].append(lf.reshape(batch, seq, n_heads))
        outs["hp"].append(h_last)
        outs["cp"].append(xr.reshape(batch, seq, d_rnn // w_cols["rnn_block"], w_cols["rnn_block"])
                          [:, seq - (CONV_WIDTH - 1):].reshape(batch, CONV_WIDTH - 1, d_rnn))

        attn_s = _attn_sample(
            qs.reshape(dec_batch, n_heads, head_dim), ks, vs, lfs.reshape(dec_batch, n_heads, 1),
            cache_k[l], cache_v[l], jnp.swapaxes(cache_logf[l], 1, 2),
            page_table, pages_per_step=n_pages)
        rnn, h_new, buf_new = _rglru_sample(
            xrs.reshape(dec_batch, d_rnn), yrs.reshape(dec_batch, d_rnn),
            jnp.swapaxes(state_conv[l], 0, 1), state_h[l], w_rnn)
        h, hn = _out_proj(attn_s.reshape(dec_batch, aw), rnn, gas, grs, xs, w_mix, tm=dec_batch)
        xs_next = _ffn(hn, h, w_ffn, tm=dec_batch, tf=w_gate.shape[2] // 4)
        outs["ks"].append(ks.reshape(dec_batch, 1, n_heads, head_dim))
        outs["vs"].append(vs.reshape(dec_batch, 1, n_heads, head_dim))
        outs["ls"].append(lfs.reshape(dec_batch, 1, n_heads))
        outs["hs"].append(h_new)
        outs["cs"].append(jnp.swapaxes(buf_new, 0, 1))

        xp, xs = xp_next, xs_next

    st = {name: jnp.stack(vals) for name, vals in outs.items()}
    return (xp.reshape(batch, seq, d_model), xs.reshape(dec_batch, 1, d_model),
            st["kp"], st["vp"], st["lp"], st["hp"], st["cp"],
            st["ks"], st["vs"], st["ls"], st["hs"], st["cs"])
```
